```python
import math, functools
import jax, jax.numpy as jnp
from jax import lax
import numpy as np

D_MODEL = 1024
BATCH = 8
SEQ = 4096
DEPTH = 2

EPS = 1e-6
D_FF = 2816
FFN_RES = 0.5
SHORT_CONV = 4

POOL_WIDTH = 512
POOL_WINDOWS = (2, 4, 8, 16)
POOL_GROUPS = len(POOL_WINDOWS)
POOL_GROUP_DIM = POOL_WIDTH // POOL_GROUPS
MLSTM_HEADS = 4
MLSTM_HEAD_DIM = 128
MLSTM_WIDTH = MLSTM_HEADS * MLSTM_HEAD_DIM
MLSTM_CHUNK = 64
EVEN_IN = POOL_WIDTH + 4 * MLSTM_WIDTH + 2 * MLSTM_HEADS
EVEN_MIX = POOL_WIDTH + MLSTM_WIDTH

SSD_HEADS = 16
SSD_HEAD_DIM = 64
SSD_WIDTH = SSD_HEADS * SSD_HEAD_DIM
SSD_GROUPS = 4
SSD_STATE = 128
SSD_CHUNK = 128
SB_HEADS = 8
SB_HEAD_DIM = 64
SB_WIDTH = SB_HEADS * SB_HEAD_DIM
SB_BLOCK = 128
SSD_XBC = SSD_WIDTH + 2 * SSD_GROUPS * SSD_STATE
ODD_IN = SSD_WIDTH + SSD_XBC + SSD_HEADS + 3 * SB_WIDTH
ODD_MIX = SSD_WIDTH + SB_WIDTH

kernel_name = 'hybrid_pool_mlstm_ssd_stickbreak_macaron'

F32 = jnp.float32


def rmsnorm(x, w):
    xf = x.astype(F32)
    y = xf * lax.rsqrt(jnp.mean(xf * xf, axis=-1, keepdims=True) + EPS)
    return (y * w.astype(F32)).astype(x.dtype)


def swiglu(x, wg, wu, wd):
    return (jax.nn.silu(x @ wg) * (x @ wu)) @ wd


def half_ffn(x, norm_w, wg, wu, wd):
    return x + FFN_RES * swiglu(rmsnorm(x, norm_w), wg, wu, wd)


def causal_dwconv(x, w, b):
    K, C = w.shape
    y = lax.conv_general_dilated(x, w[:, None, :].astype(x.dtype), window_strides=(1,),
                                 padding=[(K - 1, 0)], dimension_numbers=('NWC', 'WIO', 'NWC'),
                                 feature_group_count=C)
    return y + b.astype(x.dtype)


def pool_mixer(u, w_grp, scale):
    Bsz, S, _ = u.shape
    uf = u.astype(F32)
    pad = POOL_WINDOWS[-1]
    csum = jnp.pad(jnp.cumsum(uf, axis=1), ((0, 0), (pad, 0), (0, 0)))
    pos = jnp.arange(1, S + 1, dtype=F32)[None, :, None]
    groups = []
    for g, win in enumerate(POOL_WINDOWS):
        lo, hi = g * POOL_GROUP_DIM, (g + 1) * POOL_GROUP_DIM
        win_sum = csum[:, pad:, lo:hi] - csum[:, pad - win:pad - win + S, lo:hi]
        groups.append(win_sum / jnp.minimum(pos, float(win)) - uf[:, :, lo:hi])
    pooled = jnp.stack(groups, axis=2)
    mixed = jnp.einsum('bsgc,gcd->bsgd', pooled, w_grp.astype(F32))
    return (mixed.reshape(Bsz, S, POOL_WIDTH) * scale.astype(F32)).astype(u.dtype)


def mlstm(q, k, v, log_i, log_f):
    Bsz, S, H, Dh = q.shape
    L = MLSTM_CHUNK
    nc = S // L

    def chunks(t):
        return t.reshape(Bsz, nc, L, H, Dh).transpose(1, 0, 3, 2, 4)

    def gchunks(t):
        return t.reshape(Bsz, nc, L, H).transpose(1, 0, 3, 2)

    k = k * Dh ** -0.5
    causal = jnp.tril(jnp.ones((L, L), dtype=bool))

    def step(carry, inp):
        C, n, m = carry
        qc, kc, vc, li, lf = inp
        b = jnp.cumsum(lf, axis=-1)
        dmat = jnp.where(causal, b[..., :, None] - b[..., None, :] + li[..., None, :], -jnp.inf)
        m_inter = b + m[..., None]
        m_t = jnp.maximum(m_inter, jnp.max(dmat, axis=-1))
        w_intra = jnp.exp(dmat - m_t[..., None])
        w_inter = jnp.exp(m_inter - m_t)
        qk = jnp.einsum('bhtd,bhsd->bhts', qc, kc) * w_intra
        num = (w_inter[..., None] * jnp.einsum('bhvk,bhtk->bhtv', C, qc)
               + jnp.einsum('bhts,bhsv->bhtv', qk, vc))
        den = w_inter * jnp.einsum('bhk,bhtk->bht', n, qc) + jnp.sum(qk, axis=-1)
        h = num / jnp.maximum(jnp.abs(den), jnp.exp(-m_t))[..., None]
        m_new = m_t[..., -1]
        decay_state = jnp.exp(b[..., -1] + m - m_new)
        w_s = jnp.exp(b[..., -1:] - b + li - m_new[..., None])
        kw = kc * w_s[..., None]
        C_new = decay_state[..., None, None] * C + jnp.einsum('bhsv,bhsk->bhvk', vc, kw)
        n_new = decay_state[..., None] * n + jnp.sum(kw, axis=2)
        return (C_new, n_new, m_new), h

    init = (jnp.zeros((Bsz, H, Dh, Dh), F32), jnp.zeros((Bsz, H, Dh), F32), jnp.zeros((Bsz, H), F32))
    _, h = lax.scan(step, init, (chunks(q), chunks(k), chunks(v), gchunks(log_i), gchunks(log_f)))
    return h.transpose(1, 0, 3, 2, 4).reshape(Bsz, S, H, Dh)


def even_mixer(h, w_in, pool_w, pool_scale, qk_conv_w, qk_conv_b, gate_bias, mlstm_norm, w_out):
    Bsz, S, _ = h.shape
    proj = h @ w_in
    c0 = POOL_WIDTH
    c1 = c0 + 2 * MLSTM_WIDTH
    c2 = c1 + MLSTM_WIDTH
    c3 = c2 + MLSTM_WIDTH
    u, qk, v, o, gates = jnp.split(proj, [c0, c1, c2, c3], axis=-1)
    a_out = pool_mixer(u, pool_w, pool_scale)
    qk = jax.nn.silu(causal_dwconv(qk, qk_conv_w, qk_conv_b))
    q, k = jnp.split(qk, 2, axis=-1)

    def heads(t):
        return t.astype(F32).reshape(Bsz, S, MLSTM_HEADS, MLSTM_HEAD_DIM)

    gates = gates.astype(F32) + gate_bias.astype(F32)
    log_i = gates[..., :MLSTM_HEADS]
    log_f = jax.nn.log_sigmoid(gates[..., MLSTM_HEADS:])
    hm = mlstm(heads(q), heads(k), heads(v), log_i, log_f)
    hm = rmsnorm(hm, mlstm_norm.reshape(MLSTM_HEADS, MLSTM_HEAD_DIM))
    b_out = (jax.nn.sigmoid(o.astype(F32)) * hm.reshape(Bsz, S, MLSTM_WIDTH)).astype(h.dtype)
    return jnp.concatenate([a_out, b_out], axis=-1) @ w_out


def ssd_chunked(x, a, Bm, Cm):
    Bsz, S, H, P = x.shape
    G, N = Bm.shape[2], Bm.shape[3]
    Hg = H // G
    L = SSD_CHUNK
    nc = S // L
    x = x.reshape(Bsz, nc, L, G, Hg, P)
    a = a.reshape(Bsz, nc, L, G, Hg).transpose(0, 1, 3, 4, 2)
    Bm = Bm.reshape(Bsz, nc, L, G, N)
    Cm = Cm.reshape(Bsz, nc, L, G, N)
    a_cum = jnp.cumsum(a, axis=-1)
    causal = jnp.tril(jnp.ones((L, L), dtype=bool))
    decay = jnp.exp(jnp.where(causal, a_cum[..., :, None] - a_cum[..., None, :], -jnp.inf))
    cb = jnp.einsum('bctgn,bcsgn->bcgts', Cm, Bm)
    y_diag = jnp.einsum('bcghts,bcsghp->bctghp', cb[:, :, :, None] * decay, x)
    to_end = jnp.exp(a_cum[..., -1:] - a_cum).transpose(0, 1, 4, 2, 3)
    states = jnp.einsum('bcsgn,bcsghp->bcghpn', Bm, x * to_end[..., None])
    chunk_decay = jnp.exp(a_cum[..., -1])

    def step(hstate, inp):
        st, dec = inp
        return dec[..., None, None] * hstate + st, hstate

    h0 = jnp.zeros((Bsz, G, Hg, P, N), F32)
    _, h_prev = lax.scan(step, h0, (states.swapaxes(0, 1), chunk_decay.swapaxes(0, 1)))
    h_prev = h_prev.swapaxes(0, 1)
    from_start = jnp.exp(a_cum).transpose(0, 1, 4, 2, 3)
    y_off = jnp.einsum('bctgn,bcghpn->bctghp', Cm, h_prev) * from_start[..., None]
    return (y_diag + y_off).reshape(Bsz, S, H, P)


def stick_breaking(q, k, v):
    Bsz, S, H, Dh = q.shape
    nb = S // SB_BLOCK
    scale = Dh ** -0.5
    qb = q.reshape(Bsz, nb, SB_BLOCK, H, Dh).transpose(1, 0, 3, 2, 4)
    kh = k.transpose(0, 2, 1, 3)
    vh = v.transpose(0, 2, 1, 3)
    key_pos = jnp.arange(S)

    def block(args):
        qblk, i = args
        q_pos = i * SB_BLOCK + jnp.arange(SB_BLOCK)
        z = jnp.einsum('bhtd,bhsd->bhts', qblk, kh).astype(F32) * scale
        mask = key_pos[None, :] < q_pos[:, None]
        log_beta = jax.nn.log_sigmoid(z)
        log_1m = jnp.where(mask, log_beta - z, 0.0)
        suffix = lax.cumsum(log_1m, axis=3, reverse=True)
        w = jnp.where(mask, jnp.exp(log_beta + suffix - log_1m), 0.0)
        return jnp.einsum('bhts,bhsd->bhtd', w, vh)

    out = lax.map(block, (qb, jnp.arange(nb)))
    return out.transpose(1, 0, 3, 2, 4).reshape(Bsz, S, H * Dh)


def odd_mixer(h, w_in, conv_w, conv_b, dt_bias, A_log, D_skip, ssd_norm, q_norm, k_norm, w_out):
    Bsz, S, _ = h.shape
    proj = h @ w_in
    c0 = SSD_WIDTH
    c1 = c0 + SSD_XBC
    c2 = c1 + SSD_HEADS
    c3 = c2 + SB_WIDTH
    c4 = c3 + SB_WIDTH
    z, xbc, dt, q, k, v = jnp.split(proj, [c0, c1, c2, c3, c4], axis=-1)
    xbc = jax.nn.silu(causal_dwconv(xbc, conv_w, conv_b)).astype(F32)
    xs, Bm, Cm = jnp.split(xbc, [SSD_WIDTH, SSD_WIDTH + SSD_GROUPS * SSD_STATE], axis=-1)
    xs = xs.reshape(Bsz, S, SSD_HEADS, SSD_HEAD_DIM)
    Bm = Bm.reshape(Bsz, S, SSD_GROUPS, SSD_STATE)
    Cm = Cm.reshape(Bsz, S, SSD_GROUPS, SSD_STATE)
    dt = jax.nn.softplus(dt.astype(F32) + dt_bias.astype(F32))
    A = -jnp.exp(A_log.astype(F32))
    y = ssd_chunked(xs * dt[..., None], dt * A, Bm, Cm) + D_skip.astype(F32)[:, None] * xs
    gsz = SSD_WIDTH // SSD_GROUPS
    y = y.reshape(Bsz, S, SSD_GROUPS, gsz) * jax.nn.silu(z.astype(F32)).reshape(Bsz, S, SSD_GROUPS, gsz)
    c_out = rmsnorm(y, ssd_norm.reshape(SSD_GROUPS, gsz)).reshape(Bsz, S, SSD_WIDTH).astype(h.dtype)
    def heads(t):
        return t.astype(F32).reshape(Bsz, S, SB_HEADS, SB_HEAD_DIM)

    qh = rmsnorm(heads(q), q_norm)
    kh = rmsnorm(heads(k), k_norm)
    d_out = stick_breaking(qh, kh, heads(v)).astype(h.dtype)
    return jnp.concatenate([c_out, d_out], axis=-1) @ w_out


def setup_inputs(seed: int = 0) -> dict:
    key = jax.random.key(seed)
    keys = iter(jax.random.split(key, 64))

    def nk():
        return next(keys)

    def dense(fan_in, fan_out):
        return jax.random.normal(nk(), (fan_in, fan_out), F32) * fan_in ** -0.5

    def gain(n):
        return 1.0 + 0.02 * jax.random.normal(nk(), (n,), F32)

    def small(shape):
        return 0.02 * jax.random.normal(nk(), shape, F32)

    inp = {}
    inp['x'] = jax.random.normal(nk(), (BATCH, SEQ, D_MODEL), F32)

    def ffn(prefix):
        inp[prefix + '_norm'] = gain(D_MODEL)
        inp[prefix + '_wg'] = dense(D_MODEL, D_FF)
        inp[prefix + '_wu'] = dense(D_MODEL, D_FF)
        inp[prefix + '_wd'] = dense(D_FF, D_MODEL)

    ffn('l0_ffn1')
    inp['l0_mix_norm'] = gain(D_MODEL)
    inp['l0_w_in'] = dense(D_MODEL, EVEN_IN)
    inp['l0_pool_w'] = jax.random.normal(nk(), (POOL_GROUPS, POOL_GROUP_DIM, POOL_GROUP_DIM), F32) * POOL_GROUP_DIM ** -0.5
    inp['l0_pool_scale'] = 1.0 + 0.1 * jax.random.normal(nk(), (POOL_WIDTH,), F32)
    inp['l0_qk_conv_w'] = jax.random.normal(nk(), (SHORT_CONV, 2 * MLSTM_WIDTH), F32) * SHORT_CONV ** -0.5
    inp['l0_qk_conv_b'] = small((2 * MLSTM_WIDTH,))
    i_bias = 0.1 * jax.random.normal(nk(), (MLSTM_HEADS,), F32)
    f_bias = jnp.linspace(3.0, 6.0, MLSTM_HEADS, dtype=F32) + 0.1 * jax.random.normal(nk(), (MLSTM_HEADS,), F32)
    inp['l0_gate_bias'] = jnp.concatenate([i_bias, f_bias])
    inp['l0_mlstm_norm'] = gain(MLSTM_WIDTH)
    inp['l0_w_out'] = dense(EVEN_MIX, D_MODEL)
    ffn('l0_ffn2')
    ffn('l1_ffn1')
    inp['l1_mix_norm'] = gain(D_MODEL)
    inp['l1_w_in'] = dense(D_MODEL, ODD_IN)
    inp['l1_ssd_conv_w'] = jax.random.normal(nk(), (SHORT_CONV, SSD_XBC), F32) * SHORT_CONV ** -0.5
    inp['l1_ssd_conv_b'] = small((SSD_XBC,))
    dt0 = jnp.exp(jax.random.uniform(nk(), (SSD_HEADS,), F32, minval=math.log(1e-3), maxval=math.log(1e-1)))
    inp['l1_ssd_dt_bias'] = dt0 + jnp.log(-jnp.expm1(-dt0))
    inp['l1_ssd_A_log'] = jnp.log(jax.random.uniform(nk(), (SSD_HEADS,), F32, minval=1.0, maxval=16.0))
    inp['l1_ssd_D'] = 1.0 + 0.1 * jax.random.normal(nk(), (SSD_HEADS,), F32)
    inp['l1_ssd_norm'] = gain(SSD_WIDTH)
    inp['l1_sb_q_norm'] = gain(SB_HEAD_DIM)
    inp['l1_sb_k_norm'] = gain(SB_HEAD_DIM)
    inp['l1_w_out'] = dense(ODD_MIX, D_MODEL)
    ffn('l1_ffn2')
    return inp


def reference(x,
              l0_ffn1_norm, l0_ffn1_wg, l0_ffn1_wu, l0_ffn1_wd,
              l0_mix_norm, l0_w_in, l0_pool_w, l0_pool_scale, l0_qk_conv_w, l0_qk_conv_b,
              l0_gate_bias, l0_mlstm_norm, l0_w_out,
              l0_ffn2_norm, l0_ffn2_wg, l0_ffn2_wu, l0_ffn2_wd,
              l1_ffn1_norm, l1_ffn1_wg, l1_ffn1_wu, l1_ffn1_wd,
              l1_mix_norm, l1_w_in, l1_ssd_conv_w, l1_ssd_conv_b, l1_ssd_dt_bias, l1_ssd_A_log,
              l1_ssd_D, l1_ssd_norm, l1_sb_q_norm, l1_sb_k_norm, l1_w_out,
              l1_ffn2_norm, l1_ffn2_wg, l1_ffn2_wu, l1_ffn2_wd):
    layers = (
        ((l0_ffn1_norm, l0_ffn1_wg, l0_ffn1_wu, l0_ffn1_wd),
         l0_mix_norm,
         functools.partial(even_mixer, w_in=l0_w_in, pool_w=l0_pool_w, pool_scale=l0_pool_scale,
                           qk_conv_w=l0_qk_conv_w, qk_conv_b=l0_qk_conv_b, gate_bias=l0_gate_bias,
                           mlstm_norm=l0_mlstm_norm, w_out=l0_w_out),
         (l0_ffn2_norm, l0_ffn2_wg, l0_ffn2_wu, l0_ffn2_wd)),
        ((l1_ffn1_norm, l1_ffn1_wg, l1_ffn1_wu, l1_ffn1_wd),
         l1_mix_norm,
         functools.partial(odd_mixer, w_in=l1_w_in, conv_w=l1_ssd_conv_w, conv_b=l1_ssd_conv_b,
                           dt_bias=l1_ssd_dt_bias, A_log=l1_ssd_A_log, D_skip=l1_ssd_D,
                           ssd_norm=l1_ssd_norm, q_norm=l1_sb_q_norm, k_norm=l1_sb_k_norm,
                           w_out=l1_w_out),
         (l1_ffn2_norm, l1_ffn2_wg, l1_ffn2_wu, l1_ffn2_wd)),
    )
    for layer in range(DEPTH):
        ffn1, mix_norm, mixer, ffn2 = layers[layer]
        x = half_ffn(x, *ffn1)
        x = x + mixer(rmsnorm(x, mix_norm))
        x = half_ffn(x, *ffn2)
    return x
```

```python
import functools
import math

import jax
import jax.numpy as jnp
from jax import lax
from jax.experimental import pallas as pl
from jax.experimental.pallas import tpu as pltpu

F32 = jnp.float32
BF16 = jnp.bfloat16

EPS = 1e-6
FFN_RES = 0.5
SHORT_CONV = 4
LANES = 128

POOL_WINDOWS = (2, 4, 8, 16)
POOL_WIDTH = 512
MLSTM_HEADS = 4
MLSTM_HEAD_DIM = 128
MLSTM_WIDTH = MLSTM_HEADS * MLSTM_HEAD_DIM
CHUNK = 128

SSD_HEADS = 16
SSD_HEAD_DIM = 64
SSD_WIDTH = SSD_HEADS * SSD_HEAD_DIM
SSD_GROUPS = 4
SSD_STATE = 128
SSD_GROUP_WIDTH = SSD_WIDTH // SSD_GROUPS
SB_HEADS = 8
SB_HEAD_DIM = 64
SB_WIDTH = SB_HEADS * SB_HEAD_DIM
SB_BLOCK = 128

VMEM_LIMIT_BYTES = 56 * 1024 * 1024


def _cparams(semantics):
    return pltpu.CompilerParams(dimension_semantics=semantics, vmem_limit_bytes=VMEM_LIMIT_BYTES)


def _rmsnorm(x, w):
    return x * lax.rsqrt(jnp.mean(x * x, axis=-1, keepdims=True) + EPS) * w


def _sigmoid(x):
    return 1.0 / (1.0 + jnp.exp(-x))


def _softplus(x):
    return jnp.maximum(x, 0.0) + jnp.log(1.0 + jnp.exp(-jnp.abs(x)))


def _dot(a, b):
    return jnp.dot(a, b, preferred_element_type=F32)


def _dot_nt(a, b):
    return lax.dot_general(a, b, (((1,), (1,)), ((), ())), preferred_element_type=F32)


def _split_bf16(x):
    hi = x.astype(BF16)
    lo = (x - hi.astype(F32)).astype(BF16)
    return hi, lo


def _row_iota(shape):
    return lax.broadcasted_iota(jnp.int32, shape, 0)


def _lane_iota(shape):
    return lax.broadcasted_iota(jnp.int32, shape, 1)


def _chunk_scan(x, op, fill):
    rows = _row_iota(x.shape)
    sh = 1
    while sh < x.shape[0]:
        shifted = jnp.where(rows >= sh, pltpu.roll(x, sh, 0), fill)
        x = op(x, shifted)
        sh *= 2
    return x


def _expand_heads(slab, n_heads, head_dim):
    per = LANES // head_dim
    m = slab.shape[0]
    lane = _lane_iota((m, LANES))
    pieces = []
    for p in range(n_heads // per):
        piece = jnp.broadcast_to(slab[:, p * per:p * per + 1], (m, LANES))
        for j in range(1, per):
            piece = jnp.where(lane >= j * head_dim, slab[:, p * per + j:p * per + j + 1], piece)
        pieces.append(piece)
    return jnp.concatenate(pieces, axis=1)


def _ffn_kernel(*refs, n_mix, ff_chunk):
    x_ref = refs[0]
    mix_refs = refs[1:1 + 2 * n_mix]
    nw_ref, wg_ref, wu_ref, wd_ref, o_ref = refs[1 + 2 * n_mix:]
    x = x_ref[...]
    for i in range(n_mix):
        x = x + _dot(mix_refs[2 * i][...], mix_refs[2 * i + 1][...])
    h = _rmsnorm(x, nw_ref[...]).astype(BF16)
    d_ff = wg_ref.shape[1]
    y = None
    for c0 in range(0, d_ff, ff_chunk):
        c1 = min(c0 + ff_chunk, d_ff)
        g = _dot(h, wg_ref[:, c0:c1])
        u = _dot(h, wu_ref[:, c0:c1])
        a = (g * _sigmoid(g) * u).astype(BF16)
        part = _dot(a, wd_ref[c0:c1, :])
        y = part if y is None else y + part
    o_ref[...] = x + FFN_RES * y


def _ffn(x, mixes, norm_w, wg, wu, wd, *, tm=512, ff_chunk=512):
    t, d = x.shape
    d_ff = wg.shape[1]
    const = lambda i: (0, 0)
    in_specs = [pl.BlockSpec((tm, d), lambda i: (i, 0))]
    args = [x]
    for mix, w_out in mixes:
        in_specs += [pl.BlockSpec((tm, mix.shape[1]), lambda i: (i, 0)),
                     pl.BlockSpec(w_out.shape, const)]
        args += [mix, w_out]
    in_specs += [pl.BlockSpec((1, d), const), pl.BlockSpec((d, d_ff), const),
                 pl.BlockSpec((d, d_ff), const), pl.BlockSpec((d_ff, d), const)]
    args += [norm_w, wg, wu, wd]
    return pl.pallas_call(
        functools.partial(_ffn_kernel, n_mix=len(mixes), ff_chunk=ff_chunk),
        grid=(t // tm,),
        in_specs=in_specs,
        out_specs=pl.BlockSpec((tm, d), lambda i: (i, 0)),
        out_shape=jax.ShapeDtypeStruct((t, d), F32),
        compiler_params=_cparams(("parallel",)),
        name="ffn",
    )(*args)


EVEN_U = 0
EVEN_QK = EVEN_U + POOL_WIDTH
EVEN_V = EVEN_QK + 2 * MLSTM_WIDTH
EVEN_O = EVEN_V + MLSTM_WIDTH
EVEN_GI = EVEN_O + MLSTM_WIDTH
EVEN_GF = EVEN_GI + LANES
EVEN_COLS = EVEN_GF + LANES
POOL_HALO = 16
CONV_HALO = 8


def _even_kernel(x_ref, nw_ref, win_ref, poolw_ref, pscale_ref, convw_ref, convb_ref,
                 gbias_ref, mnorm_ref, o_ref,
                 halo_u, halo_qk, state, m_state, *, ts):
    s_idx = pl.program_id(1)

    @pl.when(s_idx == 0)
    def _():
        halo_u[...] = jnp.zeros_like(halo_u)
        halo_qk[...] = jnp.zeros_like(halo_qk)
        state[...] = jnp.zeros_like(state)
        m_state[...] = jnp.zeros_like(m_state)

    h = _rmsnorm(x_ref[...], nw_ref[...]).astype(BF16)
    proj = _dot(h, win_ref[...])

    u = proj[:, EVEN_U:EVEN_U + POOL_WIDTH]
    ue = jnp.concatenate([halo_u[...], u], axis=0)
    halo_u[...] = u[ts - POOL_HALO:, :]
    pos = (s_idx * ts + 1 + _row_iota((ts, 1))).astype(F32)
    for g, win in enumerate(POOL_WINDOWS):
        acc = ue[:, g * LANES:(g + 1) * LANES]
        sh = 1
        while sh < win:
            acc = acc + pltpu.roll(acc, sh, 0)
            sh *= 2
        win_sum = acc[POOL_HALO:, :]
        pooled = win_sum / jnp.minimum(pos, float(win)) - u[:, g * LANES:(g + 1) * LANES]
        mixed = _dot(pooled.astype(BF16), poolw_ref[g])
        o_ref[:, g * LANES:(g + 1) * LANES] = (
            mixed * pscale_ref[:, g * LANES:(g + 1) * LANES]).astype(o_ref.dtype)

    qk_raw = proj[:, EVEN_QK:EVEN_QK + 2 * MLSTM_WIDTH]
    qe = jnp.concatenate([halo_qk[...], qk_raw], axis=0)
    halo_qk[...] = qk_raw[ts - CONV_HALO:, :]
    conv = qe * convw_ref[SHORT_CONV - 1:SHORT_CONV, :]
    for k in range(SHORT_CONV - 1):
        conv = conv + pltpu.roll(qe, SHORT_CONV - 1 - k, 0) * convw_ref[k:k + 1, :]
    conv = conv[CONV_HALO:, :] + convb_ref[...]
    qk = conv * _sigmoid(conv)
    q_all = qk[:, :MLSTM_WIDTH].astype(BF16)
    k_all = qk[:, MLSTM_WIDTH:] * (MLSTM_HEAD_DIM ** -0.5)
    v_all = proj[:, EVEN_V:EVEN_V + MLSTM_WIDTH].astype(BF16)
    o_gate = _sigmoid(proj[:, EVEN_O:EVEN_O + MLSTM_WIDTH])

    g_i = proj[:, EVEN_GI:EVEN_GI + LANES] + gbias_ref[0:1, :]
    g_f = -_softplus(-(proj[:, EVEN_GF:EVEN_GF + LANES] + gbias_ref[1:2, :]))

    ones_blk = jnp.ones((CHUNK, MLSTM_HEAD_DIM), BF16)
    causal = _row_iota((CHUNK, CHUNK)) >= _lane_iota((CHUNK, CHUNK))
    for c in range(ts // CHUNK):
        r0, r1 = c * CHUNK, (c + 1) * CHUNK
        li = g_i[r0:r1, :]
        b = _chunk_scan(g_f[r0:r1, :], jnp.add, 0.0)
        a = li - b
        m_prev = m_state[...]
        big_m = jnp.maximum(m_prev, _chunk_scan(a, jnp.maximum, -jnp.inf))
        b_last = b[CHUNK - 1:CHUNK, :]
        m_new = b_last + big_m[CHUNK - 1:CHUNK, :]
        w_inter = jnp.exp(m_prev - big_m)
        e_negm = jnp.exp(-(b + big_m))
        w_state = jnp.exp(b_last + a - m_new)
        decay = jnp.exp(b_last + m_prev - m_new)
        m_state[...] = m_new
        a_t = a.T
        for hd in range(MLSTM_HEADS):
            c0, c1 = hd * MLSTM_HEAD_DIM, (hd + 1) * MLSTM_HEAD_DIM
            qc = q_all[r0:r1, c0:c1]
            kc = k_all[r0:r1, c0:c1]
            v_ext = jnp.concatenate([v_all[r0:r1, c0:c1], ones_blk], axis=1)
            w_intra = jnp.exp(jnp.where(causal, a_t[hd:hd + 1, :] - big_m[:, hd:hd + 1], -jnp.inf))
            p = (_dot_nt(qc, kc.astype(BF16)) * w_intra).astype(BF16)
            st = state[hd]
            numden = w_inter[:, hd:hd + 1] * _dot(qc, st.astype(BF16)) + _dot(p, v_ext)
            den = jnp.maximum(jnp.abs(numden[:, MLSTM_HEAD_DIM:MLSTM_HEAD_DIM + 1]),
                              e_negm[:, hd:hd + 1])
            hh = numden[:, :MLSTM_HEAD_DIM] / den
            hn = _rmsnorm(hh, mnorm_ref[:, c0:c1])
            o_ref[r0:r1, POOL_WIDTH + c0:POOL_WIDTH + c1] = (
                o_gate[r0:r1, c0:c1] * hn).astype(o_ref.dtype)
            kw_t = (kc * w_state[:, hd:hd + 1]).T.astype(BF16)
            state[hd] = decay[:, hd:hd + 1] * st + _dot(kw_t, v_ext)


def _even_mixer(x, norm_w, w_in, pool_w, pool_scale, conv_w, conv_b, gate_bias, mnorm, *, ts=512):
    bsz, s, d = x.shape
    const2 = lambda b, i: (0, 0)
    return pl.pallas_call(
        functools.partial(_even_kernel, ts=ts),
        grid=(bsz, s // ts),
        in_specs=[
            pl.BlockSpec((None, ts, d), lambda b, i: (b, i, 0)),
            pl.BlockSpec((1, d), const2),
            pl.BlockSpec(w_in.shape, const2),
            pl.BlockSpec(pool_w.shape, lambda b, i: (0, 0, 0)),
            pl.BlockSpec(pool_scale.shape, const2),
            pl.BlockSpec(conv_w.shape, const2),
            pl.BlockSpec(conv_b.shape, const2),
            pl.BlockSpec(gate_bias.shape, const2),
            pl.BlockSpec(mnorm.shape, const2),
        ],
        out_specs=pl.BlockSpec((None, ts, POOL_WIDTH + MLSTM_WIDTH), lambda b, i: (b, i, 0)),
        out_shape=jax.ShapeDtypeStruct((bsz, s, POOL_WIDTH + MLSTM_WIDTH), BF16),
        scratch_shapes=[
            pltpu.VMEM((POOL_HALO, POOL_WIDTH), F32),
            pltpu.VMEM((CONV_HALO, 2 * MLSTM_WIDTH), F32),
            pltpu.VMEM((MLSTM_HEADS, MLSTM_HEAD_DIM, 2 * MLSTM_HEAD_DIM), F32),
            pltpu.VMEM((1, LANES), F32),
        ],
        compiler_params=_cparams(("parallel", "arbitrary")),
        name="even_mixer",
    )(x, norm_w, w_in, pool_w, pool_scale, conv_w, conv_b, gate_bias, mnorm)


ODD_Z = 0
ODD_XBC = ODD_Z + SSD_WIDTH
ODD_XS = ODD_XBC
ODD_B = ODD_XS + SSD_WIDTH
ODD_C = ODD_B + SSD_GROUPS * SSD_STATE
ODD_Q = ODD_C + SSD_GROUPS * SSD_STATE
ODD_K = ODD_Q + SB_WIDTH
ODD_V = ODD_K + SB_WIDTH
ODD_DT = ODD_V + SB_WIDTH
ODD_COLS = ODD_DT + LANES
SSD_XBC = SSD_WIDTH + 2 * SSD_GROUPS * SSD_STATE


def _odd_kernel(x_ref, nw_ref, win_ref, convw_ref, convb_ref, dtb_ref, alog_ref, dskip_ref,
                snorm_ref, qn_ref, kn_ref, seg_ref,
                c_ref, q_ref, k_ref, v_ref,
                halo, hstate, *, ts):
    s_idx = pl.program_id(1)

    @pl.when(s_idx == 0)
    def _():
        halo[...] = jnp.zeros_like(halo)
        hstate[...] = jnp.zeros_like(hstate)

    h = _rmsnorm(x_ref[...], nw_ref[...]).astype(BF16)
    proj = _dot(h, win_ref[...])

    seg = seg_ref[...]
    for src, nref, dst, scale in ((ODD_Q, qn_ref, q_ref, SB_HEAD_DIM ** -0.5), (ODD_K, kn_ref, k_ref, 1.0)):
        t = proj[:, src:src + SB_WIDTH]
        hi, lo = _split_bf16(t * t)
        ssq = _dot(hi, seg) + _dot(lo, seg)
        dst[...] = (t * lax.rsqrt(ssq * (1.0 / SB_HEAD_DIM) + EPS) * (nref[...] * scale)).astype(dst.dtype)
    v_ref[...] = proj[:, ODD_V:ODD_V + SB_WIDTH].astype(v_ref.dtype)

    raw = proj[:, ODD_XBC:ODD_XBC + SSD_XBC]
    xe = jnp.concatenate([halo[...], raw], axis=0)
    halo[...] = raw[ts - CONV_HALO:, :]
    conv = xe * convw_ref[SHORT_CONV - 1:SHORT_CONV, :]
    for k in range(SHORT_CONV - 1):
        conv = conv + pltpu.roll(xe, SHORT_CONV - 1 - k, 0) * convw_ref[k:k + 1, :]
    conv = conv[CONV_HALO:, :] + convb_ref[...]
    xbc = conv * _sigmoid(conv)
    xs_all = xbc[:, :SSD_WIDTH]
    bm_all = xbc[:, SSD_WIDTH:SSD_WIDTH + SSD_GROUPS * SSD_STATE].astype(BF16)
    cm_all = xbc[:, SSD_WIDTH + SSD_GROUPS * SSD_STATE:].astype(BF16)

    dt_all = _softplus(proj[:, ODD_DT:ODD_DT + LANES] + dtb_ref[...])
    a_all = dt_all * (-jnp.exp(alog_ref[...]))
    zg = proj[:, ODD_Z:ODD_Z + SSD_WIDTH]
    zg = zg * _sigmoid(zg)

    causal = _row_iota((CHUNK, CHUNK)) >= _lane_iota((CHUNK, CHUNK))
    lane = _lane_iota((CHUNK, LANES))
    heads_per_group = SSD_HEADS // SSD_GROUPS
    for c in range(ts // CHUNK):
        r0, r1 = c * CHUNK, (c + 1) * CHUNK
        xs = xs_all[r0:r1, :]
        a_cum = _chunk_scan(a_all[r0:r1, :], jnp.add, 0.0)
        a_last = a_cum[CHUNK - 1:CHUNK, :]
        a_t = a_cum.T
        xdt = xs * _expand_heads(dt_all[r0:r1, :], SSD_HEADS, SSD_HEAD_DIM)
        xw = (xdt * _expand_heads(jnp.exp(a_last - a_cum), SSD_HEADS, SSD_HEAD_DIM)).astype(BF16)
        xdt = xdt.astype(BF16)
        from_start = _expand_heads(jnp.exp(a_cum), SSD_HEADS, SSD_HEAD_DIM)
        chunk_decay = _expand_heads(jnp.exp(a_last), SSD_HEADS, SSD_HEAD_DIM)
        y_parts = []
        for g in range(SSD_GROUPS):
            bg = bm_all[r0:r1, g * SSD_STATE:(g + 1) * SSD_STATE]
            cg = cm_all[r0:r1, g * SSD_STATE:(g + 1) * SSD_STATE]
            cb = _dot_nt(cg, bg)
            gs = slice(g * SSD_GROUP_WIDTH, (g + 1) * SSD_GROUP_WIDTH)
            hprev = hstate[g]
            y_off = _dot(cg, hprev.astype(BF16))
            bg_t = bm_all[r0:r1, g * SSD_STATE:(g + 1) * SSD_STATE].astype(F32).T.astype(BF16)
            hstate[g] = chunk_decay[:, gs] * hprev + _dot(bg_t, xw[:, gs])
            diag = []
            for pair in range(heads_per_group // 2):
                h0 = g * heads_per_group + 2 * pair
                ps = slice(g * SSD_GROUP_WIDTH + pair * LANES, g * SSD_GROUP_WIDTH + (pair + 1) * LANES)
                outs = []
                for hh in (h0, h0 + 1):
                    dec = jnp.exp(jnp.where(causal, a_cum[:, hh:hh + 1] - a_t[hh:hh + 1, :], -jnp.inf))
                    outs.append(_dot((cb * dec).astype(BF16), xdt[:, ps]))
                diag.append(jnp.where(lane < SSD_HEAD_DIM, outs[0], outs[1]))
            y_parts.append(jnp.concatenate(diag, axis=1) + y_off * from_start[:, gs])
        y = jnp.concatenate(y_parts, axis=1) + dskip_ref[...] * xs
        y = y * zg[r0:r1, :]
        for g in range(SSD_GROUPS):
            gs = slice(g * SSD_GROUP_WIDTH, (g + 1) * SSD_GROUP_WIDTH)
            c_ref[r0:r1, gs] = _rmsnorm(y[:, gs], snorm_ref[:, gs]).astype(c_ref.dtype)


def _odd_mixer(x, norm_w, w_in, conv_w, conv_b, dt_bias, a_log, d_skip, snorm, qn, kn, seg, *, ts=256):
    bsz, s, d = x.shape
    const2 = lambda b, i: (0, 0)
    tile = lambda w: pl.BlockSpec((None, ts, w), lambda b, i: (b, i, 0))
    return pl.pallas_call(
        functools.partial(_odd_kernel, ts=ts),
        grid=(bsz, s // ts),
        in_specs=[tile(d)] + [pl.BlockSpec(a.shape, const2) for a in
                              (norm_w, w_in, conv_w, conv_b, dt_bias, a_log, d_skip, snorm, qn, kn, seg)],
        out_specs=[tile(SSD_WIDTH), tile(SB_WIDTH), tile(SB_WIDTH), tile(SB_WIDTH)],
        out_shape=[jax.ShapeDtypeStruct((bsz, s, SSD_WIDTH), BF16)]
        + [jax.ShapeDtypeStruct((bsz, s, SB_WIDTH), BF16)] * 3,
        scratch_shapes=[
            pltpu.VMEM((CONV_HALO, SSD_XBC), F32),
            pltpu.VMEM((SSD_GROUPS, SSD_STATE, SSD_GROUP_WIDTH), F32),
        ],
        compiler_params=_cparams(("parallel", "arbitrary")),
        name="odd_mixer",
    )(x, norm_w, w_in, conv_w, conv_b, dt_bias, a_log, d_skip, snorm, qn, kn, seg)


def _sb_kernel(q_ref, k_ref, v_ref, tri_ref, o_ref):
    qi = pl.program_id(2)
    blk = SB_BLOCK
    lane = _lane_iota((blk, LANES))
    q = q_ref[...]
    zero = jnp.zeros_like(q)
    q_heads = (jnp.where(lane < SB_HEAD_DIM, q, zero), jnp.where(lane >= SB_HEAD_DIM, q, zero))
    tri = tri_ref[...]
    strict = _row_iota((blk, blk)) > _lane_iota((blk, blk))

    def block(j, carry, masked):
        k0 = pl.multiple_of(j * blk, blk)
        kb = k_ref[pl.ds(k0, blk), :]
        vb = v_ref[pl.ds(k0, blk), :]
        new = []
        for hd in range(2):
            acc, r_sp = carry[hd]
            z = _dot_nt(q_heads[hd], kb)
            sp = _softplus(z)
            if masked:
                sp = jnp.where(strict, sp, 0.0)
            hi, lo = _split_bf16(sp)
            cs = _dot(hi, tri) + _dot(lo, tri)
            w = jnp.exp(z - cs[:, :blk] - r_sp)
            if masked:
                w = jnp.where(strict, w, 0.0)
            new.append((acc + _dot(w.astype(BF16), vb), r_sp + cs[:, blk:]))
        return tuple(new)

    init = tuple((jnp.zeros((blk, LANES), F32), jnp.zeros((blk, blk), F32)) for _ in range(2))
    carry = block(qi, init, True)
    carry = lax.fori_loop(0, qi, lambda it, cr: block(qi - 1 - it, cr, False), carry)
    o_ref[...] = jnp.where(lane < SB_HEAD_DIM, carry[0][0], carry[1][0]).astype(o_ref.dtype)


def _stickbreak(q, k, v, tri):
    bsz, s, w = q.shape
    pairs = w // LANES
    return pl.pallas_call(
        _sb_kernel,
        grid=(bsz, pairs, s // SB_BLOCK),
        in_specs=[
            pl.BlockSpec((None, SB_BLOCK, LANES), lambda b, p, i: (b, i, p)),
            pl.BlockSpec((None, s, LANES), lambda b, p, i: (b, 0, p)),
            pl.BlockSpec((None, s, LANES), lambda b, p, i: (b, 0, p)),
            pl.BlockSpec(tri.shape, lambda b, p, i: (0, 0)),
        ],
        out_specs=pl.BlockSpec((None, SB_BLOCK, LANES), lambda b, p, i: (b, i, p)),
        out_shape=jax.ShapeDtypeStruct((bsz, s, w), BF16),
        compiler_params=_cparams(("parallel", "parallel", "arbitrary")),
        name="stickbreak",
    )(q, k, v, tri)


def _pad_cols(w, n):
    return jnp.pad(w, ((0, 0), (0, n - w.shape[1])))


def kernel(x, l0_ffn1_norm, l0_ffn1_wg, l0_ffn1_wu, l0_ffn1_wd, l0_mix_norm, l0_w_in, l0_pool_w, l0_pool_scale, l0_qk_conv_w, l0_qk_conv_b, l0_gate_bias, l0_mlstm_norm, l0_w_out, l0_ffn2_norm, l0_ffn2_wg, l0_ffn2_wu, l0_ffn2_wd, l1_ffn1_norm, l1_ffn1_wg, l1_ffn1_wu, l1_ffn1_wd, l1_mix_norm, l1_w_in, l1_ssd_conv_w, l1_ssd_conv_b, l1_ssd_dt_bias, l1_ssd_A_log, l1_ssd_D, l1_ssd_norm, l1_sb_q_norm, l1_sb_k_norm, l1_w_out, l1_ffn2_norm, l1_ffn2_wg, l1_ffn2_wu, l1_ffn2_wd):
    bsz, s, d = x.shape
    t = bsz * s
    row = lambda a: a.reshape(1, -1).astype(F32)
    bf = lambda a: a.astype(BF16)

    def ffn(xf, mixes, norm_w, wg, wu, wd):
        return _ffn(xf, mixes, row(norm_w), bf(wg), bf(wu), bf(wd))

    xf = ffn(x.reshape(t, d), [], l0_ffn1_norm, l0_ffn1_wg, l0_ffn1_wu, l0_ffn1_wd)
    n_main = EVEN_GI
    w_in0 = jnp.concatenate([
        l0_w_in[:, :n_main],
        _pad_cols(l0_w_in[:, n_main:n_main + MLSTM_HEADS], LANES),
        _pad_cols(l0_w_in[:, n_main + MLSTM_HEADS:], LANES)], axis=1)
    gbias = jnp.stack([jnp.pad(l0_gate_bias[:MLSTM_HEADS], (0, LANES - MLSTM_HEADS)),
                       jnp.pad(l0_gate_bias[MLSTM_HEADS:], (0, LANES - MLSTM_HEADS))]).astype(F32)
    mix0 = _even_mixer(xf.reshape(bsz, s, d), row(l0_mix_norm), bf(w_in0), bf(l0_pool_w),
                       row(l0_pool_scale), l0_qk_conv_w.astype(F32), row(l0_qk_conv_b), gbias,
                       row(l0_mlstm_norm))
    xf = ffn(xf, [(mix0.reshape(t, -1), bf(l0_w_out))], l0_ffn2_norm, l0_ffn2_wg, l0_ffn2_wu, l0_ffn2_wd)

    xf = ffn(xf, [], l1_ffn1_norm, l1_ffn1_wg, l1_ffn1_wu, l1_ffn1_wd)
    c_dt = SSD_WIDTH + SSD_XBC
    c_q = c_dt + SSD_HEADS
    w_in1 = jnp.concatenate([l1_w_in[:, :c_dt], l1_w_in[:, c_q:],
                             _pad_cols(l1_w_in[:, c_dt:c_q], LANES)], axis=1)
    pad_heads = lambda a: jnp.pad(a.astype(F32), (0, LANES - SSD_HEADS)).reshape(1, LANES)
    seg_id = jnp.arange(SB_WIDTH) // SB_HEAD_DIM
    seg = (seg_id[:, None] == seg_id[None, :]).astype(BF16)
    c_out, qn, kn, vv = _odd_mixer(
        xf.reshape(bsz, s, d), row(l1_mix_norm), bf(w_in1), l1_ssd_conv_w.astype(F32), row(l1_ssd_conv_b),
        pad_heads(l1_ssd_dt_bias), pad_heads(l1_ssd_A_log), row(jnp.repeat(l1_ssd_D, SSD_HEAD_DIM)),
        row(l1_ssd_norm), row(jnp.tile(l1_sb_q_norm, SB_HEADS)), row(jnp.tile(l1_sb_k_norm, SB_HEADS)), seg)
    idx = jnp.arange(SB_BLOCK)
    tri = jnp.concatenate([(idx[:, None] >= idx[None, :]).astype(BF16),
                           jnp.ones((SB_BLOCK, SB_BLOCK), BF16)], axis=1)
    d_out = _stickbreak(qn, kn, vv, tri)
    w_out1 = bf(l1_w_out)
    xf = ffn(xf, [(c_out.reshape(t, -1), w_out1[:SSD_WIDTH]), (d_out.reshape(t, -1), w_out1[SSD_WIDTH:])],
             l1_ffn2_norm, l1_ffn2_wg, l1_ffn2_wu, l1_ffn2_wd)
    return xf.reshape(bsz, s, d)
```

```python
import functools
import math

import jax
import jax.numpy as jnp
from jax import lax
from jax.experimental import pallas as pl
from jax.experimental.pallas import tpu as pltpu

F32 = jnp.float32
BF16 = jnp.bfloat16

EPS = 1e-6
FFN_RES = 0.5
SHORT_CONV = 4
LANES = 128

POOL_WINDOWS = (2, 4, 8, 16)
POOL_WIDTH = 512
MLSTM_HEADS = 4
MLSTM_HEAD_DIM = 128
MLSTM_WIDTH = MLSTM_HEADS * MLSTM_HEAD_DIM
CHUNK = 128

SSD_HEADS = 16
SSD_HEAD_DIM = 64
SSD_WIDTH = SSD_HEADS * SSD_HEAD_DIM
SSD_GROUPS = 4
SSD_STATE = 128
SSD_GROUP_WIDTH = SSD_WIDTH // SSD_GROUPS
SB_HEADS = 8
SB_HEAD_DIM = 64
SB_WIDTH = SB_HEADS * SB_HEAD_DIM
SB_BLOCK = 128
SB_EXIT = 104.0

VMEM_LIMIT_BYTES = 56 * 1024 * 1024


def _cparams(semantics):
    return pltpu.CompilerParams(dimension_semantics=semantics, vmem_limit_bytes=VMEM_LIMIT_BYTES)


def _rmsnorm(x, w):
    return x * lax.rsqrt(jnp.mean(x * x, axis=-1, keepdims=True) + EPS) * w


def _sigmoid(x):
    return 1.0 / (1.0 + jnp.exp(-x))


def _softplus(x):
    return jnp.maximum(x, 0.0) + jnp.log(1.0 + jnp.exp(-jnp.abs(x)))


def _dot(a, b):
    return jnp.dot(a, b, preferred_element_type=F32)


def _dot_nt(a, b):
    return lax.dot_general(a, b, (((1,), (1,)), ((), ())), preferred_element_type=F32)


def _split_bf16(x):
    hi = x.astype(BF16)
    lo = (x - hi.astype(F32)).astype(BF16)
    return hi, lo


def _row_iota(shape):
    return lax.broadcasted_iota(jnp.int32, shape, 0)


def _lane_iota(shape):
    return lax.broadcasted_iota(jnp.int32, shape, 1)


def _chunk_scan(x, op, fill):
    rows = _row_iota(x.shape)
    sh = 1
    while sh < x.shape[0]:
        shifted = jnp.where(rows >= sh, pltpu.roll(x, sh, 0), fill)
        x = op(x, shifted)
        sh *= 2
    return x


def _expand_heads(slab, n_heads, head_dim):
    per = LANES // head_dim
    m = slab.shape[0]
    lane = _lane_iota((m, LANES))
    pieces = []
    for p in range(n_heads // per):
        piece = jnp.broadcast_to(slab[:, p * per:p * per + 1], (m, LANES))
        for j in range(1, per):
            piece = jnp.where(lane >= j * head_dim, slab[:, p * per + j:p * per + j + 1], piece)
        pieces.append(piece)
    return jnp.concatenate(pieces, axis=1)


def _ffn_kernel(*refs, n_mix, ff_chunk):
    x_ref = refs[0]
    mix_refs = refs[1:1 + 2 * n_mix]
    nw_ref, wg_ref, wu_ref, wd_ref, o_ref = refs[1 + 2 * n_mix:]
    x = x_ref[...]
    for i in range(n_mix):
        x = x + _dot(mix_refs[2 * i][...], mix_refs[2 * i + 1][...])
    h = _rmsnorm(x, nw_ref[...]).astype(BF16)
    d_ff = wg_ref.shape[1]
    y = None
    for c0 in range(0, d_ff, ff_chunk):
        c1 = min(c0 + ff_chunk, d_ff)
        g = _dot(h, wg_ref[:, c0:c1])
        u = _dot(h, wu_ref[:, c0:c1])
        a = (g * _sigmoid(g) * u).astype(BF16)
        part = _dot(a, wd_ref[c0:c1, :])
        y = part if y is None else y + part
    o_ref[...] = x + FFN_RES * y


def _ffn(x, mixes, norm_w, wg, wu, wd, *, tm=512, ff_chunk=512):
    t, d = x.shape
    d_ff = wg.shape[1]
    const = lambda i: (0, 0)
    in_specs = [pl.BlockSpec((tm, d), lambda i: (i, 0))]
    args = [x]
    for mix, w_out in mixes:
        in_specs += [pl.BlockSpec((tm, mix.shape[1]), lambda i: (i, 0)),
                     pl.BlockSpec(w_out.shape, const)]
        args += [mix, w_out]
    in_specs += [pl.BlockSpec((1, d), const), pl.BlockSpec((d, d_ff), const),
                 pl.BlockSpec((d, d_ff), const), pl.BlockSpec((d_ff, d), const)]
    args += [norm_w, wg, wu, wd]
    return pl.pallas_call(
        functools.partial(_ffn_kernel, n_mix=len(mixes), ff_chunk=ff_chunk),
        grid=(t // tm,),
        in_specs=in_specs,
        out_specs=pl.BlockSpec((tm, d), lambda i: (i, 0)),
        out_shape=jax.ShapeDtypeStruct((t, d), F32),
        compiler_params=_cparams(("parallel",)),
        name="ffn",
    )(*args)


EVEN_U = 0
EVEN_QK = EVEN_U + POOL_WIDTH
EVEN_V = EVEN_QK + 2 * MLSTM_WIDTH
EVEN_O = EVEN_V + MLSTM_WIDTH
EVEN_GI = EVEN_O + MLSTM_WIDTH
EVEN_GF = EVEN_GI + LANES
EVEN_COLS = EVEN_GF + LANES
POOL_HALO = 16
CONV_HALO = 8


def _even_kernel(x_ref, nw_ref, win_ref, poolw_ref, pscale_ref, convw_ref, convb_ref,
                 gbias_ref, mnorm_ref, o_ref,
                 halo_u, halo_qk, state, m_state, *, ts):
    s_idx = pl.program_id(1)

    @pl.when(s_idx == 0)
    def _():
        halo_u[...] = jnp.zeros_like(halo_u)
        halo_qk[...] = jnp.zeros_like(halo_qk)
        state[...] = jnp.zeros_like(state)
        m_state[...] = jnp.zeros_like(m_state)

    h = _rmsnorm(x_ref[...], nw_ref[...]).astype(BF16)
    proj = _dot(h, win_ref[...])

    u = proj[:, EVEN_U:EVEN_U + POOL_WIDTH]
    ue = jnp.concatenate([halo_u[...], u], axis=0)
    halo_u[...] = u[ts - POOL_HALO:, :]
    pos = (s_idx * ts + 1 + _row_iota((ts, 1))).astype(F32)
    for g, win in enumerate(POOL_WINDOWS):
        acc = ue[:, g * LANES:(g + 1) * LANES]
        sh = 1
        while sh < win:
            acc = acc + pltpu.roll(acc, sh, 0)
            sh *= 2
        win_sum = acc[POOL_HALO:, :]
        pooled = win_sum / jnp.minimum(pos, float(win)) - u[:, g * LANES:(g + 1) * LANES]
        mixed = _dot(pooled.astype(BF16), poolw_ref[g])
        o_ref[:, g * LANES:(g + 1) * LANES] = (
            mixed * pscale_ref[:, g * LANES:(g + 1) * LANES]).astype(o_ref.dtype)

    qk_raw = proj[:, EVEN_QK:EVEN_QK + 2 * MLSTM_WIDTH]
    qe = jnp.concatenate([halo_qk[...], qk_raw], axis=0)
    halo_qk[...] = qk_raw[ts - CONV_HALO:, :]
    conv = qe * convw_ref[SHORT_CONV - 1:SHORT_CONV, :]
    for k in range(SHORT_CONV - 1):
        conv = conv + pltpu.roll(qe, SHORT_CONV - 1 - k, 0) * convw_ref[k:k + 1, :]
    conv = conv[CONV_HALO:, :] + convb_ref[...]
    qk = conv * _sigmoid(conv)
    q_all = qk[:, :MLSTM_WIDTH].astype(BF16)
    k_all = qk[:, MLSTM_WIDTH:] * (MLSTM_HEAD_DIM ** -0.5)
    v_all = proj[:, EVEN_V:EVEN_V + MLSTM_WIDTH].astype(BF16)
    o_gate = _sigmoid(proj[:, EVEN_O:EVEN_O + MLSTM_WIDTH])

    g_i = proj[:, EVEN_GI:EVEN_GI + LANES] + gbias_ref[0:1, :]
    g_f = -_softplus(-(proj[:, EVEN_GF:EVEN_GF + LANES] + gbias_ref[1:2, :]))

    ones_blk = jnp.ones((CHUNK, MLSTM_HEAD_DIM), BF16)
    causal = _row_iota((CHUNK, CHUNK)) >= _lane_iota((CHUNK, CHUNK))
    for c in range(ts // CHUNK):
        r0, r1 = c * CHUNK, (c + 1) * CHUNK
        li = g_i[r0:r1, :]
        b = _chunk_scan(g_f[r0:r1, :], jnp.add, 0.0)
        a = li - b
        m_prev = m_state[...]
        big_m = jnp.maximum(m_prev, _chunk_scan(a, jnp.maximum, -jnp.inf))
        b_last = b[CHUNK - 1:CHUNK, :]
        m_new = b_last + big_m[CHUNK - 1:CHUNK, :]
        w_inter = jnp.exp(m_prev - big_m)
        e_negm = jnp.exp(-(b + big_m))
        w_state = jnp.exp(b_last + a - m_new)
        decay = jnp.exp(b_last + m_prev - m_new)
        m_state[...] = m_new
        a_t = a.T
        for hd in range(MLSTM_HEADS):
            c0, c1 = hd * MLSTM_HEAD_DIM, (hd + 1) * MLSTM_HEAD_DIM
            qc = q_all[r0:r1, c0:c1]
            kc = k_all[r0:r1, c0:c1]
            v_ext = jnp.concatenate([v_all[r0:r1, c0:c1], ones_blk], axis=1)
            w_intra = jnp.exp(jnp.where(causal, a_t[hd:hd + 1, :] - big_m[:, hd:hd + 1], -jnp.inf))
            p = (_dot_nt(qc, kc.astype(BF16)) * w_intra).astype(BF16)
            st = state[hd]
            numden = w_inter[:, hd:hd + 1] * _dot(qc, st.astype(BF16)) + _dot(p, v_ext)
            den = jnp.maximum(jnp.abs(numden[:, MLSTM_HEAD_DIM:MLSTM_HEAD_DIM + 1]),
                              e_negm[:, hd:hd + 1])
            hh = numden[:, :MLSTM_HEAD_DIM] / den
            hn = _rmsnorm(hh, mnorm_ref[:, c0:c1])
            o_ref[r0:r1, POOL_WIDTH + c0:POOL_WIDTH + c1] = (
                o_gate[r0:r1, c0:c1] * hn).astype(o_ref.dtype)
            kw_t = (kc * w_state[:, hd:hd + 1]).T.astype(BF16)
            state[hd] = decay[:, hd:hd + 1] * st + _dot(kw_t, v_ext)


def _even_mixer(x, norm_w, w_in, pool_w, pool_scale, conv_w, conv_b, gate_bias, mnorm, *, ts=512):
    bsz, s, d = x.shape
    const2 = lambda b, i: (0, 0)
    return pl.pallas_call(
        functools.partial(_even_kernel, ts=ts),
        grid=(bsz, s // ts),
        in_specs=[
            pl.BlockSpec((None, ts, d), lambda b, i: (b, i, 0)),
            pl.BlockSpec((1, d), const2),
            pl.BlockSpec(w_in.shape, const2),
            pl.BlockSpec(pool_w.shape, lambda b, i: (0, 0, 0)),
            pl.BlockSpec(pool_scale.shape, const2),
            pl.BlockSpec(conv_w.shape, const2),
            pl.BlockSpec(conv_b.shape, const2),
            pl.BlockSpec(gate_bias.shape, const2),
            pl.BlockSpec(mnorm.shape, const2),
        ],
        out_specs=pl.BlockSpec((None, ts, POOL_WIDTH + MLSTM_WIDTH), lambda b, i: (b, i, 0)),
        out_shape=jax.ShapeDtypeStruct((bsz, s, POOL_WIDTH + MLSTM_WIDTH), BF16),
        scratch_shapes=[
            pltpu.VMEM((POOL_HALO, POOL_WIDTH), F32),
            pltpu.VMEM((CONV_HALO, 2 * MLSTM_WIDTH), F32),
            pltpu.VMEM((MLSTM_HEADS, MLSTM_HEAD_DIM, 2 * MLSTM_HEAD_DIM), F32),
            pltpu.VMEM((1, LANES), F32),
        ],
        compiler_params=_cparams(("parallel", "arbitrary")),
        name="even_mixer",
    )(x, norm_w, w_in, pool_w, pool_scale, conv_w, conv_b, gate_bias, mnorm)


ODD_Z = 0
ODD_XBC = ODD_Z + SSD_WIDTH
ODD_XS = ODD_XBC
ODD_B = ODD_XS + SSD_WIDTH
ODD_C = ODD_B + SSD_GROUPS * SSD_STATE
ODD_Q = ODD_C + SSD_GROUPS * SSD_STATE
ODD_K = ODD_Q + SB_WIDTH
ODD_V = ODD_K + SB_WIDTH
ODD_DT = ODD_V + SB_WIDTH
ODD_COLS = ODD_DT + LANES
SSD_XBC = SSD_WIDTH + 2 * SSD_GROUPS * SSD_STATE


def _odd_kernel(x_ref, nw_ref, win_ref, convw_ref, convb_ref, dtb_ref, alog_ref, dskip_ref,
                snorm_ref, qn_ref, kn_ref, seg_ref,
                c_ref, q_ref, k_ref, v_ref,
                halo, hstate, *, ts):
    s_idx = pl.program_id(1)

    @pl.when(s_idx == 0)
    def _():
        halo[...] = jnp.zeros_like(halo)
        hstate[...] = jnp.zeros_like(hstate)

    h = _rmsnorm(x_ref[...], nw_ref[...]).astype(BF16)
    proj = _dot(h, win_ref[...])

    seg = seg_ref[...]
    for src, nref, dst, scale in ((ODD_Q, qn_ref, q_ref, SB_HEAD_DIM ** -0.5), (ODD_K, kn_ref, k_ref, 1.0)):
        t = proj[:, src:src + SB_WIDTH]
        hi, lo = _split_bf16(t * t)
        ssq = _dot(hi, seg) + _dot(lo, seg)
        dst[...] = (t * lax.rsqrt(ssq * (1.0 / SB_HEAD_DIM) + EPS) * (nref[...] * scale)).astype(dst.dtype)
    v_ref[...] = proj[:, ODD_V:ODD_V + SB_WIDTH].astype(v_ref.dtype)

    raw = proj[:, ODD_XBC:ODD_XBC + SSD_XBC]
    xe = jnp.concatenate([halo[...], raw], axis=0)
    halo[...] = raw[ts - CONV_HALO:, :]
    conv = xe * convw_ref[SHORT_CONV - 1:SHORT_CONV, :]
    for k in range(SHORT_CONV - 1):
        conv = conv + pltpu.roll(xe, SHORT_CONV - 1 - k, 0) * convw_ref[k:k + 1, :]
    conv = conv[CONV_HALO:, :] + convb_ref[...]
    xbc = conv * _sigmoid(conv)
    xs_all = xbc[:, :SSD_WIDTH]
    bm_all = xbc[:, SSD_WIDTH:SSD_WIDTH + SSD_GROUPS * SSD_STATE].astype(BF16)
    cm_all = xbc[:, SSD_WIDTH + SSD_GROUPS * SSD_STATE:].astype(BF16)

    dt_all = _softplus(proj[:, ODD_DT:ODD_DT + LANES] + dtb_ref[...])
    a_all = dt_all * (-jnp.exp(alog_ref[...]))
    zg = proj[:, ODD_Z:ODD_Z + SSD_WIDTH]
    zg = zg * _sigmoid(zg)

    causal = _row_iota((CHUNK, CHUNK)) >= _lane_iota((CHUNK, CHUNK))
    lane = _lane_iota((CHUNK, LANES))
    heads_per_group = SSD_HEADS // SSD_GROUPS
    for c in range(ts // CHUNK):
        r0, r1 = c * CHUNK, (c + 1) * CHUNK
        xs = xs_all[r0:r1, :]
        a_cum = _chunk_scan(a_all[r0:r1, :], jnp.add, 0.0)
        a_last = a_cum[CHUNK - 1:CHUNK, :]
        a_t = a_cum.T
        xdt = xs * _expand_heads(dt_all[r0:r1, :], SSD_HEADS, SSD_HEAD_DIM)
        xw = (xdt * _expand_heads(jnp.exp(a_last - a_cum), SSD_HEADS, SSD_HEAD_DIM)).astype(BF16)
        xdt = xdt.astype(BF16)
        from_start = _expand_heads(jnp.exp(a_cum), SSD_HEADS, SSD_HEAD_DIM)
        chunk_decay = _expand_heads(jnp.exp(a_last), SSD_HEADS, SSD_HEAD_DIM)
        y_parts = []
        for g in range(SSD_GROUPS):
            bg = bm_all[r0:r1, g * SSD_STATE:(g + 1) * SSD_STATE]
            cg = cm_all[r0:r1, g * SSD_STATE:(g + 1) * SSD_STATE]
            cb = _dot_nt(cg, bg)
            gs = slice(g * SSD_GROUP_WIDTH, (g + 1) * SSD_GROUP_WIDTH)
            hprev = hstate[g]
            y_off = _dot(cg, hprev.astype(BF16))
            bg_t = bm_all[r0:r1, g * SSD_STATE:(g + 1) * SSD_STATE].astype(F32).T.astype(BF16)
            hstate[g] = chunk_decay[:, gs] * hprev + _dot(bg_t, xw[:, gs])
            diag = []
            for pair in range(heads_per_group // 2):
                h0 = g * heads_per_group + 2 * pair
                ps = slice(g * SSD_GROUP_WIDTH + pair * LANES, g * SSD_GROUP_WIDTH + (pair + 1) * LANES)
                outs = []
                for hh in (h0, h0 + 1):
                    dec = jnp.exp(jnp.where(causal, a_cum[:, hh:hh + 1] - a_t[hh:hh + 1, :], -jnp.inf))
                    outs.append(_dot((cb * dec).astype(BF16), xdt[:, ps]))
                diag.append(jnp.where(lane < SSD_HEAD_DIM, outs[0], outs[1]))
            y_parts.append(jnp.concatenate(diag, axis=1) + y_off * from_start[:, gs])
        y = jnp.concatenate(y_parts, axis=1) + dskip_ref[...] * xs
        y = y * zg[r0:r1, :]
        for g in range(SSD_GROUPS):
            gs = slice(g * SSD_GROUP_WIDTH, (g + 1) * SSD_GROUP_WIDTH)
            c_ref[r0:r1, gs] = _rmsnorm(y[:, gs], snorm_ref[:, gs]).astype(c_ref.dtype)


def _odd_mixer(x, norm_w, w_in, conv_w, conv_b, dt_bias, a_log, d_skip, snorm, qn, kn, seg, *, ts=256):
    bsz, s, d = x.shape
    const2 = lambda b, i: (0, 0)
    tile = lambda w: pl.BlockSpec((None, ts, w), lambda b, i: (b, i, 0))
    return pl.pallas_call(
        functools.partial(_odd_kernel, ts=ts),
        grid=(bsz, s // ts),
        in_specs=[tile(d)] + [pl.BlockSpec(a.shape, const2) for a in
                              (norm_w, w_in, conv_w, conv_b, dt_bias, a_log, d_skip, snorm, qn, kn, seg)],
        out_specs=[tile(SSD_WIDTH), tile(SB_WIDTH), tile(SB_WIDTH), tile(SB_WIDTH)],
        out_shape=[jax.ShapeDtypeStruct((bsz, s, SSD_WIDTH), BF16)]
        + [jax.ShapeDtypeStruct((bsz, s, SB_WIDTH), BF16)] * 3,
        scratch_shapes=[
            pltpu.VMEM((CONV_HALO, SSD_XBC), F32),
            pltpu.VMEM((SSD_GROUPS, SSD_STATE, SSD_GROUP_WIDTH), F32),
        ],
        compiler_params=_cparams(("parallel", "arbitrary")),
        name="odd_mixer",
    )(x, norm_w, w_in, conv_w, conv_b, dt_bias, a_log, d_skip, snorm, qn, kn, seg)


def _sb_kernel(q_ref, k_ref, v_ref, tri_ref, o_ref):
    qi = pl.program_id(1)
    blk = SB_BLOCK
    pairs = SB_WIDTH // LANES
    rows = 2 * pairs * blk
    first = _lane_iota((blk, LANES)) < SB_HEAD_DIM
    q = q_ref[...]
    zero = jnp.zeros((blk, LANES), q.dtype)
    q_stack = []
    for p in range(pairs):
        qp = q[:, p * LANES:(p + 1) * LANES]
        q_stack.append(jnp.concatenate([jnp.where(first, qp, zero), jnp.where(first, zero, qp)], axis=0))
    tri = tri_ref[...]
    strict = _row_iota((rows, blk)) % blk > _lane_iota((rows, blk))

    def block(j, accs, r_sp, masked):
        k0 = pl.multiple_of(j * blk, blk)
        kb = k_ref[pl.ds(k0, blk), :]
        vb = v_ref[pl.ds(k0, blk), :]
        z = jnp.concatenate([_dot_nt(q_stack[p], kb[:, p * LANES:(p + 1) * LANES]) for p in range(pairs)],
                            axis=0)
        sp = _softplus(z)
        if masked:
            sp = jnp.where(strict, sp, 0.0)
        hi, lo = _split_bf16(sp)
        cs = _dot(jnp.concatenate([hi, lo], axis=0), tri)
        cs = cs[:rows] + cs[rows:]
        w = jnp.exp(z - cs[:, :blk] - r_sp)
        if masked:
            w = jnp.where(strict, w, 0.0)
        w = w.astype(BF16)
        new_accs = []
        for p in range(pairs):
            pv = _dot(w[2 * p * blk:2 * (p + 1) * blk], vb[:, p * LANES:(p + 1) * LANES])
            new_accs.append(accs[p] + jnp.where(first, pv[:blk], pv[blk:]))
        return tuple(new_accs), r_sp + cs[:, blk:]

    accs = tuple(jnp.zeros((blk, LANES), F32) for _ in range(pairs))
    accs, r_sp = block(qi, accs, jnp.zeros((rows, blk), F32), True)

    def cond(c):
        return jnp.logical_and(c[0] >= 0, c[1] < SB_EXIT)

    def body(c):
        j, _, accs, r_sp = c
        accs, r_sp = block(j, accs, r_sp, False)
        return j - 1, jnp.min(r_sp), accs, r_sp

    _, _, accs, _ = lax.while_loop(cond, body, (qi - 1, jnp.min(r_sp), accs, r_sp))
    for p in range(pairs):
        o_ref[:, p * LANES:(p + 1) * LANES] = accs[p].astype(o_ref.dtype)


def _stickbreak(q, k, v, tri):
    bsz, s, w = q.shape
    return pl.pallas_call(
        _sb_kernel,
        grid=(bsz, s // SB_BLOCK),
        in_specs=[
            pl.BlockSpec((None, SB_BLOCK, w), lambda b, i: (b, i, 0)),
            pl.BlockSpec((None, s, w), lambda b, i: (b, 0, 0)),
            pl.BlockSpec((None, s, w), lambda b, i: (b, 0, 0)),
            pl.BlockSpec(tri.shape, lambda b, i: (0, 0)),
        ],
        out_specs=pl.BlockSpec((None, SB_BLOCK, w), lambda b, i: (b, i, 0)),
        out_shape=jax.ShapeDtypeStruct((bsz, s, w), BF16),
        compiler_params=_cparams(("parallel", "arbitrary")),
        name="stickbreak",
    )(q, k, v, tri)


def _pad_cols(w, n):
    return jnp.pad(w, ((0, 0), (0, n - w.shape[1])))


def kernel(x, l0_ffn1_norm, l0_ffn1_wg, l0_ffn1_wu, l0_ffn1_wd, l0_mix_norm, l0_w_in, l0_pool_w, l0_pool_scale, l0_qk_conv_w, l0_qk_conv_b, l0_gate_bias, l0_mlstm_norm, l0_w_out, l0_ffn2_norm, l0_ffn2_wg, l0_ffn2_wu, l0_ffn2_wd, l1_ffn1_norm, l1_ffn1_wg, l1_ffn1_wu, l1_ffn1_wd, l1_mix_norm, l1_w_in, l1_ssd_conv_w, l1_ssd_conv_b, l1_ssd_dt_bias, l1_ssd_A_log, l1_ssd_D, l1_ssd_norm, l1_sb_q_norm, l1_sb_k_norm, l1_w_out, l1_ffn2_norm, l1_ffn2_wg, l1_ffn2_wu, l1_ffn2_wd):
    bsz, s, d = x.shape
    t = bsz * s
    row = lambda a: a.reshape(1, -1).astype(F32)
    bf = lambda a: a.astype(BF16)

    def ffn(xf, mixes, norm_w, wg, wu, wd):
        return _ffn(xf, mixes, row(norm_w), bf(wg), bf(wu), bf(wd))

    xf = ffn(x.reshape(t, d), [], l0_ffn1_norm, l0_ffn1_wg, l0_ffn1_wu, l0_ffn1_wd)
    n_main = EVEN_GI
    w_in0 = jnp.concatenate([
        l0_w_in[:, :n_main],
        _pad_cols(l0_w_in[:, n_main:n_main + MLSTM_HEADS], LANES),
        _pad_cols(l0_w_in[:, n_main + MLSTM_HEADS:], LANES)], axis=1)
    gbias = jnp.stack([jnp.pad(l0_gate_bias[:MLSTM_HEADS], (0, LANES - MLSTM_HEADS)),
                       jnp.pad(l0_gate_bias[MLSTM_HEADS:], (0, LANES - MLSTM_HEADS))]).astype(F32)
    mix0 = _even_mixer(xf.reshape(bsz, s, d), row(l0_mix_norm), bf(w_in0), bf(l0_pool_w),
                       row(l0_pool_scale), l0_qk_conv_w.astype(F32), row(l0_qk_conv_b), gbias,
                       row(l0_mlstm_norm))
    xf = ffn(xf, [(mix0.reshape(t, -1), bf(l0_w_out))], l0_ffn2_norm, l0_ffn2_wg, l0_ffn2_wu, l0_ffn2_wd)

    xf = ffn(xf, [], l1_ffn1_norm, l1_ffn1_wg, l1_ffn1_wu, l1_ffn1_wd)
    c_dt = SSD_WIDTH + SSD_XBC
    c_q = c_dt + SSD_HEADS
    w_in1 = jnp.concatenate([l1_w_in[:, :c_dt], l1_w_in[:, c_q:],
                             _pad_cols(l1_w_in[:, c_dt:c_q], LANES)], axis=1)
    pad_heads = lambda a: jnp.pad(a.astype(F32), (0, LANES - SSD_HEADS)).reshape(1, LANES)
    seg_id = jnp.arange(SB_WIDTH) // SB_HEAD_DIM
    seg = (seg_id[:, None] == seg_id[None, :]).astype(BF16)
    c_out, qn, kn, vv = _odd_mixer(
        xf.reshape(bsz, s, d), row(l1_mix_norm), bf(w_in1), l1_ssd_conv_w.astype(F32), row(l1_ssd_conv_b),
        pad_heads(l1_ssd_dt_bias), pad_heads(l1_ssd_A_log), row(jnp.repeat(l1_ssd_D, SSD_HEAD_DIM)),
        row(l1_ssd_norm), row(jnp.tile(l1_sb_q_norm, SB_HEADS)), row(jnp.tile(l1_sb_k_norm, SB_HEADS)), seg)
    idx = jnp.arange(SB_BLOCK)
    tri = jnp.concatenate([(idx[:, None] >= idx[None, :]).astype(BF16),
                           jnp.ones((SB_BLOCK, SB_BLOCK), BF16)], axis=1)
    d_out = _stickbreak(qn, kn, vv, tri)
    w_out1 = bf(l1_w_out)
    xf = ffn(xf, [(c_out.reshape(t, -1), w_out1[:SSD_WIDTH]), (d_out.reshape(t, -1), w_out1[SSD_WIDTH:])],
             l1_ffn2_norm, l1_ffn2_wg, l1_ffn2_wu, l1_ffn2_wd)
    return xf.reshape(bsz, s, d)
```

```python
import functools
import math

import jax
import jax.numpy as jnp
from jax import lax
from jax.experimental import pallas as pl
from jax.experimental.pallas import tpu as pltpu

F32 = jnp.float32
BF16 = jnp.bfloat16

EPS = 1e-6
FFN_RES = 0.5
SHORT_CONV = 4
LANES = 128

POOL_WINDOWS = (2, 4, 8, 16)
POOL_WIDTH = 512
MLSTM_HEADS = 4
MLSTM_HEAD_DIM = 128
MLSTM_WIDTH = MLSTM_HEADS * MLSTM_HEAD_DIM
CHUNK = 128

SSD_HEADS = 16
SSD_HEAD_DIM = 64
SSD_WIDTH = SSD_HEADS * SSD_HEAD_DIM
SSD_GROUPS = 4
SSD_STATE = 128
SSD_GROUP_WIDTH = SSD_WIDTH // SSD_GROUPS
SB_HEADS = 8
SB_HEAD_DIM = 64
SB_WIDTH = SB_HEADS * SB_HEAD_DIM
SB_BLOCK = 128
SB_EXIT = 104.0
SB_NEAR = 3

VMEM_LIMIT_BYTES = 56 * 1024 * 1024


def _cparams(semantics):
    return pltpu.CompilerParams(dimension_semantics=semantics, vmem_limit_bytes=VMEM_LIMIT_BYTES)


def _rmsnorm(x, w):
    return x * lax.rsqrt(jnp.mean(x * x, axis=-1, keepdims=True) + EPS) * w


def _sigmoid(x):
    return 1.0 / (1.0 + jnp.exp(-x))


def _sigmoid_tanh(x):
    return 0.5 + 0.5 * jnp.tanh(0.5 * x)


def _silu_tanh(x):
    h = 0.5 * x
    return h + h * jnp.tanh(h)


def _softplus(x):
    return jnp.maximum(x, 0.0) + jnp.log(1.0 + jnp.exp(-jnp.abs(x)))


def _dot(a, b):
    return jnp.dot(a, b, preferred_element_type=F32)


def _dot_nt(a, b):
    return lax.dot_general(a, b, (((1,), (1,)), ((), ())), preferred_element_type=F32)


def _split_bf16(x):
    hi = x.astype(BF16)
    lo = (x - hi.astype(F32)).astype(BF16)
    return hi, lo


def _row_iota(shape):
    return lax.broadcasted_iota(jnp.int32, shape, 0)


def _lane_iota(shape):
    return lax.broadcasted_iota(jnp.int32, shape, 1)


def _chunk_scan(x, op, fill):
    rows = _row_iota(x.shape)
    sh = 1
    while sh < x.shape[0]:
        shifted = jnp.where(rows >= sh, pltpu.roll(x, sh, 0), fill)
        x = op(x, shifted)
        sh *= 2
    return x


def _causal_conv(buf_ref, raw, w_ref, b_ref):
    ts = raw.shape[0]
    buf_ref[CONV_HALO:, :] = raw
    acc = raw * w_ref[SHORT_CONV - 1:SHORT_CONV, :] + b_ref[...]
    for k in range(SHORT_CONV - 1):
        off = CONV_HALO - (SHORT_CONV - 1 - k)
        acc = acc + buf_ref[off:off + ts, :] * w_ref[k:k + 1, :]
    buf_ref[0:CONV_HALO, :] = raw[ts - CONV_HALO:, :]
    return acc


def _expand_heads(slab, n_heads, head_dim):
    per = LANES // head_dim
    m = slab.shape[0]
    lane = _lane_iota((m, LANES))
    pieces = []
    for p in range(n_heads // per):
        piece = jnp.broadcast_to(slab[:, p * per:p * per + 1], (m, LANES))
        for j in range(1, per):
            piece = jnp.where(lane >= j * head_dim, slab[:, p * per + j:p * per + j + 1], piece)
        pieces.append(piece)
    return jnp.concatenate(pieces, axis=1)


def _ffn_kernel(*refs, n_mix, ff_chunk):
    x_ref = refs[0]
    mix_refs = refs[1:1 + 2 * n_mix]
    nw_ref, wg_ref, wu_ref, wd_ref, o_ref = refs[1 + 2 * n_mix:]
    x = x_ref[...]
    for i in range(n_mix):
        x = x + _dot(mix_refs[2 * i][...], mix_refs[2 * i + 1][...])
    h = _rmsnorm(x, nw_ref[...]).astype(BF16)
    d_ff = wg_ref.shape[1]
    y = None
    for c0 in range(0, d_ff, ff_chunk):
        c1 = min(c0 + ff_chunk, d_ff)
        g = _dot(h, wg_ref[:, c0:c1])
        u = _dot(h, wu_ref[:, c0:c1])
        a = (g * _sigmoid(g) * u).astype(BF16)
        part = _dot(a, wd_ref[c0:c1, :])
        y = part if y is None else y + part
    o_ref[...] = x + FFN_RES * y


def _ffn(x, mixes, norm_w, wg, wu, wd, *, tm=512, ff_chunk=512):
    t, d = x.shape
    d_ff = wg.shape[1]
    const = lambda i: (0, 0)
    in_specs = [pl.BlockSpec((tm, d), lambda i: (i, 0))]
    args = [x]
    for mix, w_out in mixes:
        in_specs += [pl.BlockSpec((tm, mix.shape[1]), lambda i: (i, 0)),
                     pl.BlockSpec(w_out.shape, const)]
        args += [mix, w_out]
    in_specs += [pl.BlockSpec((1, d), const), pl.BlockSpec((d, d_ff), const),
                 pl.BlockSpec((d, d_ff), const), pl.BlockSpec((d_ff, d), const)]
    args += [norm_w, wg, wu, wd]
    return pl.pallas_call(
        functools.partial(_ffn_kernel, n_mix=len(mixes), ff_chunk=ff_chunk),
        grid=(t // tm,),
        in_specs=in_specs,
        out_specs=pl.BlockSpec((tm, d), lambda i: (i, 0)),
        out_shape=jax.ShapeDtypeStruct((t, d), F32),
        compiler_params=_cparams(("parallel",)),
        name="ffn",
    )(*args)


EVEN_U = 0
EVEN_QK = EVEN_U + POOL_WIDTH
EVEN_V = EVEN_QK + 2 * MLSTM_WIDTH
EVEN_O = EVEN_V + MLSTM_WIDTH
EVEN_GI = EVEN_O + MLSTM_WIDTH
EVEN_GF = EVEN_GI + LANES
EVEN_COLS = EVEN_GF + LANES
POOL_HALO = 16
CONV_HALO = 8


def _even_kernel(x_ref, nw_ref, win_ref, poolw_ref, pscale_ref, convw_ref, convb_ref,
                 gbias_ref, mnorm_ref, o_ref,
                 halo_u, halo_qk, state, m_state, *, ts):
    s_idx = pl.program_id(1)

    @pl.when(s_idx == 0)
    def _():
        halo_u[...] = jnp.zeros_like(halo_u)
        halo_qk[0:CONV_HALO, :] = jnp.zeros((CONV_HALO, halo_qk.shape[1]), F32)
        state[...] = jnp.zeros_like(state)
        m_state[...] = jnp.zeros_like(m_state)

    h = _rmsnorm(x_ref[...], nw_ref[...]).astype(BF16)
    proj = _dot(h, win_ref[...])

    u = proj[:, EVEN_U:EVEN_U + POOL_WIDTH]
    ue = jnp.concatenate([halo_u[...], u], axis=0)
    halo_u[...] = u[ts - POOL_HALO:, :]
    pos = (s_idx * ts + 1 + _row_iota((ts, 1))).astype(F32)
    for g, win in enumerate(POOL_WINDOWS):
        acc = ue[:, g * LANES:(g + 1) * LANES]
        sh = 1
        while sh < win:
            acc = acc + pltpu.roll(acc, sh, 0)
            sh *= 2
        win_sum = acc[POOL_HALO:, :]
        pooled = win_sum / jnp.minimum(pos, float(win)) - u[:, g * LANES:(g + 1) * LANES]
        mixed = _dot(pooled.astype(BF16), poolw_ref[g])
        o_ref[:, g * LANES:(g + 1) * LANES] = (
            mixed * pscale_ref[:, g * LANES:(g + 1) * LANES]).astype(o_ref.dtype)

    qk_raw = proj[:, EVEN_QK:EVEN_QK + 2 * MLSTM_WIDTH]
    conv = _causal_conv(halo_qk, qk_raw, convw_ref, convb_ref)
    qk = _silu_tanh(conv)
    q_all = qk[:, :MLSTM_WIDTH].astype(BF16)
    k_all = qk[:, MLSTM_WIDTH:] * (MLSTM_HEAD_DIM ** -0.5)
    v_all = proj[:, EVEN_V:EVEN_V + MLSTM_WIDTH].astype(BF16)
    o_gate = _sigmoid_tanh(proj[:, EVEN_O:EVEN_O + MLSTM_WIDTH])

    g_i = proj[:, EVEN_GI:EVEN_GI + LANES] + gbias_ref[0:1, :]
    g_f = -_softplus(-(proj[:, EVEN_GF:EVEN_GF + LANES] + gbias_ref[1:2, :]))

    ones_blk = jnp.ones((CHUNK, MLSTM_HEAD_DIM), BF16)
    causal = _row_iota((CHUNK, CHUNK)) >= _lane_iota((CHUNK, CHUNK))
    for c in range(ts // CHUNK):
        r0, r1 = c * CHUNK, (c + 1) * CHUNK
        li = g_i[r0:r1, :]
        b = _chunk_scan(g_f[r0:r1, :], jnp.add, 0.0)
        a = li - b
        m_prev = m_state[...]
        big_m = jnp.maximum(m_prev, _chunk_scan(a, jnp.maximum, -jnp.inf))
        b_last = b[CHUNK - 1:CHUNK, :]
        m_new = b_last + big_m[CHUNK - 1:CHUNK, :]
        w_inter = jnp.exp(m_prev - big_m)
        e_negm = jnp.exp(-(b + big_m))
        w_state = jnp.exp(b_last + a - m_new)
        decay = jnp.exp(b_last + m_prev - m_new)
        m_state[...] = m_new
        a_t = a.T
        for hd in range(MLSTM_HEADS):
            c0, c1 = hd * MLSTM_HEAD_DIM, (hd + 1) * MLSTM_HEAD_DIM
            qc = q_all[r0:r1, c0:c1]
            kc = k_all[r0:r1, c0:c1]
            v_ext = jnp.concatenate([v_all[r0:r1, c0:c1], ones_blk], axis=1)
            w_intra = jnp.exp(jnp.where(causal, a_t[hd:hd + 1, :] - big_m[:, hd:hd + 1], -jnp.inf))
            p = (_dot_nt(qc, kc.astype(BF16)) * w_intra).astype(BF16)
            st = state[hd]
            numden = w_inter[:, hd:hd + 1] * _dot(qc, st.astype(BF16)) + _dot(p, v_ext)
            den = jnp.maximum(jnp.abs(numden[:, MLSTM_HEAD_DIM:MLSTM_HEAD_DIM + 1]),
                              e_negm[:, hd:hd + 1])
            hh = numden[:, :MLSTM_HEAD_DIM] / den
            hn = _rmsnorm(hh, mnorm_ref[:, c0:c1])
            o_ref[r0:r1, POOL_WIDTH + c0:POOL_WIDTH + c1] = (
                o_gate[r0:r1, c0:c1] * hn).astype(o_ref.dtype)
            kw_t = (kc * w_state[:, hd:hd + 1]).T.astype(BF16)
            state[hd] = decay[:, hd:hd + 1] * st + _dot(kw_t, v_ext)


def _even_mixer(x, norm_w, w_in, pool_w, pool_scale, conv_w, conv_b, gate_bias, mnorm, *, ts=512):
    bsz, s, d = x.shape
    const2 = lambda b, i: (0, 0)
    return pl.pallas_call(
        functools.partial(_even_kernel, ts=ts),
        grid=(bsz, s // ts),
        in_specs=[
            pl.BlockSpec((None, ts, d), lambda b, i: (b, i, 0)),
            pl.BlockSpec((1, d), const2),
            pl.BlockSpec(w_in.shape, const2),
            pl.BlockSpec(pool_w.shape, lambda b, i: (0, 0, 0)),
            pl.BlockSpec(pool_scale.shape, const2),
            pl.BlockSpec(conv_w.shape, const2),
            pl.BlockSpec(conv_b.shape, const2),
            pl.BlockSpec(gate_bias.shape, const2),
            pl.BlockSpec(mnorm.shape, const2),
        ],
        out_specs=pl.BlockSpec((None, ts, POOL_WIDTH + MLSTM_WIDTH), lambda b, i: (b, i, 0)),
        out_shape=jax.ShapeDtypeStruct((bsz, s, POOL_WIDTH + MLSTM_WIDTH), BF16),
        scratch_shapes=[
            pltpu.VMEM((POOL_HALO, POOL_WIDTH), F32),
            pltpu.VMEM((CONV_HALO + ts, 2 * MLSTM_WIDTH), F32),
            pltpu.VMEM((MLSTM_HEADS, MLSTM_HEAD_DIM, 2 * MLSTM_HEAD_DIM), F32),
            pltpu.VMEM((1, LANES), F32),
        ],
        compiler_params=_cparams(("parallel", "arbitrary")),
        name="even_mixer",
    )(x, norm_w, w_in, pool_w, pool_scale, conv_w, conv_b, gate_bias, mnorm)


ODD_Z = 0
ODD_XBC = ODD_Z + SSD_WIDTH
ODD_XS = ODD_XBC
ODD_B = ODD_XS + SSD_WIDTH
ODD_C = ODD_B + SSD_GROUPS * SSD_STATE
ODD_Q = ODD_C + SSD_GROUPS * SSD_STATE
ODD_K = ODD_Q + SB_WIDTH
ODD_V = ODD_K + SB_WIDTH
ODD_DT = ODD_V + SB_WIDTH
ODD_COLS = ODD_DT + LANES
SSD_XBC = SSD_WIDTH + 2 * SSD_GROUPS * SSD_STATE


def _odd_kernel(x_ref, nw_ref, win_ref, convw_ref, convb_ref, dtb_ref, alog_ref, dskip_ref,
                snorm_ref, qn_ref, kn_ref, seg_ref,
                c_ref, q_ref, k_ref, v_ref,
                halo, hstate, *, ts):
    s_idx = pl.program_id(1)

    @pl.when(s_idx == 0)
    def _():
        halo[0:CONV_HALO, :] = jnp.zeros((CONV_HALO, halo.shape[1]), F32)
        hstate[...] = jnp.zeros_like(hstate)

    h = _rmsnorm(x_ref[...], nw_ref[...]).astype(BF16)
    proj = _dot(h, win_ref[...])

    seg = seg_ref[...]
    for src, nref, dst, scale in ((ODD_Q, qn_ref, q_ref, SB_HEAD_DIM ** -0.5), (ODD_K, kn_ref, k_ref, 1.0)):
        t = proj[:, src:src + SB_WIDTH]
        hi, lo = _split_bf16(t * t)
        ssq = _dot(hi, seg) + _dot(lo, seg)
        dst[...] = (t * lax.rsqrt(ssq * (1.0 / SB_HEAD_DIM) + EPS) * (nref[...] * scale)).astype(dst.dtype)
    v_ref[...] = proj[:, ODD_V:ODD_V + SB_WIDTH].astype(v_ref.dtype)

    raw = proj[:, ODD_XBC:ODD_XBC + SSD_XBC]
    conv = _causal_conv(halo, raw, convw_ref, convb_ref)
    xbc = _silu_tanh(conv)
    xs_all = xbc[:, :SSD_WIDTH]
    bm_all = xbc[:, SSD_WIDTH:SSD_WIDTH + SSD_GROUPS * SSD_STATE].astype(BF16)
    cm_all = xbc[:, SSD_WIDTH + SSD_GROUPS * SSD_STATE:].astype(BF16)

    dt_all = _softplus(proj[:, ODD_DT:ODD_DT + LANES] + dtb_ref[...])
    a_all = dt_all * (-jnp.exp(alog_ref[...]))
    zg = proj[:, ODD_Z:ODD_Z + SSD_WIDTH]
    zg = _silu_tanh(zg)

    causal = _row_iota((CHUNK, CHUNK)) >= _lane_iota((CHUNK, CHUNK))
    lane = _lane_iota((CHUNK, LANES))
    heads_per_group = SSD_HEADS // SSD_GROUPS
    for c in range(ts // CHUNK):
        r0, r1 = c * CHUNK, (c + 1) * CHUNK
        xs = xs_all[r0:r1, :]
        a_cum = _chunk_scan(a_all[r0:r1, :], jnp.add, 0.0)
        a_last = a_cum[CHUNK - 1:CHUNK, :]
        a_t = a_cum.T
        xdt = xs * _expand_heads(dt_all[r0:r1, :], SSD_HEADS, SSD_HEAD_DIM)
        a_wide = _expand_heads(a_cum, SSD_HEADS, SSD_HEAD_DIM)
        a_last_wide = a_wide[CHUNK - 1:CHUNK, :]
        xw = (xdt * jnp.exp(a_last_wide - a_wide)).astype(BF16)
        xdt = xdt.astype(BF16)
        from_start = jnp.exp(a_wide)
        chunk_decay = jnp.exp(a_last_wide)
        y_parts = []
        for g in range(SSD_GROUPS):
            bg = bm_all[r0:r1, g * SSD_STATE:(g + 1) * SSD_STATE]
            cg = cm_all[r0:r1, g * SSD_STATE:(g + 1) * SSD_STATE]
            cb = _dot_nt(cg, bg)
            gs = slice(g * SSD_GROUP_WIDTH, (g + 1) * SSD_GROUP_WIDTH)
            hprev = hstate[g]
            y_off = _dot(cg, hprev.astype(BF16))
            bg_t = bm_all[r0:r1, g * SSD_STATE:(g + 1) * SSD_STATE].astype(F32).T.astype(BF16)
            hstate[g] = chunk_decay[:, gs] * hprev + _dot(bg_t, xw[:, gs])
            diag = []
            for pair in range(heads_per_group // 2):
                h0 = g * heads_per_group + 2 * pair
                ps = slice(g * SSD_GROUP_WIDTH + pair * LANES, g * SSD_GROUP_WIDTH + (pair + 1) * LANES)
                outs = []
                for hh in (h0, h0 + 1):
                    dec = jnp.exp(jnp.where(causal, a_cum[:, hh:hh + 1] - a_t[hh:hh + 1, :], -jnp.inf))
                    outs.append(_dot((cb * dec).astype(BF16), xdt[:, ps]))
                diag.append(jnp.where(lane < SSD_HEAD_DIM, outs[0], outs[1]))
            y_parts.append(jnp.concatenate(diag, axis=1) + y_off * from_start[:, gs])
        y = jnp.concatenate(y_parts, axis=1) + dskip_ref[...] * xs
        y = y * zg[r0:r1, :]
        for g in range(SSD_GROUPS):
            gs = slice(g * SSD_GROUP_WIDTH, (g + 1) * SSD_GROUP_WIDTH)
            c_ref[r0:r1, gs] = _rmsnorm(y[:, gs], snorm_ref[:, gs]).astype(c_ref.dtype)


def _odd_mixer(x, norm_w, w_in, conv_w, conv_b, dt_bias, a_log, d_skip, snorm, qn, kn, seg, *, ts=512):
    bsz, s, d = x.shape
    const2 = lambda b, i: (0, 0)
    tile = lambda w: pl.BlockSpec((None, ts, w), lambda b, i: (b, i, 0))
    return pl.pallas_call(
        functools.partial(_odd_kernel, ts=ts),
        grid=(bsz, s // ts),
        in_specs=[tile(d)] + [pl.BlockSpec(a.shape, const2) for a in
                              (norm_w, w_in, conv_w, conv_b, dt_bias, a_log, d_skip, snorm, qn, kn, seg)],
        out_specs=[tile(SSD_WIDTH), tile(SB_WIDTH), tile(SB_WIDTH), tile(SB_WIDTH)],
        out_shape=[jax.ShapeDtypeStruct((bsz, s, SSD_WIDTH), BF16)]
        + [jax.ShapeDtypeStruct((bsz, s, SB_WIDTH), BF16)] * 3,
        scratch_shapes=[
            pltpu.VMEM((CONV_HALO + ts, SSD_XBC), F32),
            pltpu.VMEM((SSD_GROUPS, SSD_STATE, SSD_GROUP_WIDTH), F32),
        ],
        compiler_params=_cparams(("parallel", "arbitrary")),
        name="odd_mixer",
    )(x, norm_w, w_in, conv_w, conv_b, dt_bias, a_log, d_skip, snorm, qn, kn, seg)


def _sb_kernel(q_ref, k_ref, v_ref, tri_ref, o_ref, acc_ref, r_ref):
    qi = pl.program_id(1)
    blk = SB_BLOCK
    pairs = SB_WIDTH // LANES
    rows = 2 * pairs * blk
    first = _lane_iota((blk, LANES)) < SB_HEAD_DIM
    q = q_ref[...]
    zero = jnp.zeros((blk, LANES), q.dtype)
    q_stack = []
    for p in range(pairs):
        qp = q[:, p * LANES:(p + 1) * LANES]
        q_stack.append(jnp.concatenate([jnp.where(first, qp, zero), jnp.where(first, zero, qp)], axis=0))
    tri = tri_ref[...]
    strict = _row_iota((rows, blk)) % blk > _lane_iota((rows, blk))

    def process(j_lo, nblk, diag, r):
        k0 = pl.multiple_of(j_lo * blk, blk)
        kb = k_ref[pl.ds(k0, nblk * blk), :]
        vb = v_ref[pl.ds(k0, nblk * blk), :]
        z = jnp.concatenate([_dot_nt(q_stack[p], kb[:, p * LANES:(p + 1) * LANES]) for p in range(pairs)],
                            axis=0)
        sp = _softplus(z)
        sp_near_first = [sp[:, b * blk:(b + 1) * blk] for b in reversed(range(nblk))]
        if diag:
            sp_near_first[0] = jnp.where(strict, sp_near_first[0], 0.0)
        hi, lo = _split_bf16(jnp.concatenate(sp_near_first, axis=0))
        cs = _dot(jnp.concatenate([hi, lo], axis=1), tri)
        ws = []
        for i in range(nblk):
            b = nblk - 1 - i
            cs_b = cs[i * rows:(i + 1) * rows]
            arg = z[:, b * blk:(b + 1) * blk] - cs_b[:, :blk]
            w_b = jnp.exp(arg if r is None else arg - r)
            if diag and i == 0:
                w_b = jnp.where(strict, w_b, 0.0)
            ws.append(w_b.astype(BF16))
            r = cs_b[:, blk:] if r is None else r + cs_b[:, blk:]
        w = jnp.concatenate(ws[::-1], axis=1)
        outs = []
        for p in range(pairs):
            pv = _dot(w[2 * p * blk:2 * (p + 1) * blk], vb[:, p * LANES:(p + 1) * LANES])
            outs.append(jnp.where(first, pv[:blk], pv[blk:]))
        return jnp.concatenate(outs, axis=1), r

    @pl.when(qi >= SB_NEAR - 1)
    def _():
        acc_ref[...], r_ref[...] = process(qi - (SB_NEAR - 1), SB_NEAR, True, None)

    @pl.when(qi < SB_NEAR - 1)
    def _():
        acc_ref[...], r_ref[...] = process(qi, 1, True, None)

    def cond(c):
        return jnp.logical_and(c[0] >= 0, c[1] < SB_EXIT)

    def body(c):
        out, r = process(c[0], 1, False, r_ref[...])
        acc_ref[...] += out
        r_ref[...] = r
        return c[0] - 1, jnp.min(r)

    j_start = jnp.where(qi >= SB_NEAR - 1, qi - SB_NEAR, qi - 1)
    lax.while_loop(cond, body, (j_start, jnp.min(r_ref[...])))
    o_ref[...] = acc_ref[...].astype(o_ref.dtype)


def _stickbreak(q, k, v, tri):
    bsz, s, w = q.shape
    return pl.pallas_call(
        _sb_kernel,
        grid=(bsz, s // SB_BLOCK),
        in_specs=[
            pl.BlockSpec((None, SB_BLOCK, w), lambda b, i: (b, i, 0)),
            pl.BlockSpec((None, s, w), lambda b, i: (b, 0, 0)),
            pl.BlockSpec((None, s, w), lambda b, i: (b, 0, 0)),
            pl.BlockSpec(tri.shape, lambda b, i: (0, 0)),
        ],
        out_specs=pl.BlockSpec((None, SB_BLOCK, w), lambda b, i: (b, i, 0)),
        out_shape=jax.ShapeDtypeStruct((bsz, s, w), BF16),
        scratch_shapes=[pltpu.VMEM((SB_BLOCK, w), F32),
                        pltpu.VMEM((2 * (w // LANES) * SB_BLOCK, SB_BLOCK), F32)],
        compiler_params=_cparams(("parallel", "arbitrary")),
        name="stickbreak",
    )(q, k, v, tri)


def _pad_cols(w, n):
    return jnp.pad(w, ((0, 0), (0, n - w.shape[1])))


def kernel(x, l0_ffn1_norm, l0_ffn1_wg, l0_ffn1_wu, l0_ffn1_wd, l0_mix_norm, l0_w_in, l0_pool_w, l0_pool_scale, l0_qk_conv_w, l0_qk_conv_b, l0_gate_bias, l0_mlstm_norm, l0_w_out, l0_ffn2_norm, l0_ffn2_wg, l0_ffn2_wu, l0_ffn2_wd, l1_ffn1_norm, l1_ffn1_wg, l1_ffn1_wu, l1_ffn1_wd, l1_mix_norm, l1_w_in, l1_ssd_conv_w, l1_ssd_conv_b, l1_ssd_dt_bias, l1_ssd_A_log, l1_ssd_D, l1_ssd_norm, l1_sb_q_norm, l1_sb_k_norm, l1_w_out, l1_ffn2_norm, l1_ffn2_wg, l1_ffn2_wu, l1_ffn2_wd):
    bsz, s, d = x.shape
    t = bsz * s
    row = lambda a: a.reshape(1, -1).astype(F32)
    bf = lambda a: a.astype(BF16)

    def ffn(xf, mixes, norm_w, wg, wu, wd):
        return _ffn(xf, mixes, row(norm_w), bf(wg), bf(wu), bf(wd))

    xf = ffn(x.reshape(t, d), [], l0_ffn1_norm, l0_ffn1_wg, l0_ffn1_wu, l0_ffn1_wd)
    n_main = EVEN_GI
    w_in0 = jnp.concatenate([
        l0_w_in[:, :n_main],
        _pad_cols(l0_w_in[:, n_main:n_main + MLSTM_HEADS], LANES),
        _pad_cols(l0_w_in[:, n_main + MLSTM_HEADS:], LANES)], axis=1)
    gbias = jnp.stack([jnp.pad(l0_gate_bias[:MLSTM_HEADS], (0, LANES - MLSTM_HEADS)),
                       jnp.pad(l0_gate_bias[MLSTM_HEADS:], (0, LANES - MLSTM_HEADS))]).astype(F32)
    mix0 = _even_mixer(xf.reshape(bsz, s, d), row(l0_mix_norm), bf(w_in0), bf(l0_pool_w),
                       row(l0_pool_scale), l0_qk_conv_w.astype(F32), row(l0_qk_conv_b), gbias,
                       row(l0_mlstm_norm))
    xf = ffn(xf, [(mix0.reshape(t, -1), bf(l0_w_out))], l0_ffn2_norm, l0_ffn2_wg, l0_ffn2_wu, l0_ffn2_wd)

    xf = ffn(xf, [], l1_ffn1_norm, l1_ffn1_wg, l1_ffn1_wu, l1_ffn1_wd)
    c_dt = SSD_WIDTH + SSD_XBC
    c_q = c_dt + SSD_HEADS
    w_in1 = jnp.concatenate([l1_w_in[:, :c_dt], l1_w_in[:, c_q:],
                             _pad_cols(l1_w_in[:, c_dt:c_q], LANES)], axis=1)
    pad_heads = lambda a: jnp.pad(a.astype(F32), (0, LANES - SSD_HEADS)).reshape(1, LANES)
    seg_id = jnp.arange(SB_WIDTH) // SB_HEAD_DIM
    seg = (seg_id[:, None] == seg_id[None, :]).astype(BF16)
    c_out, qn, kn, vv = _odd_mixer(
        xf.reshape(bsz, s, d), row(l1_mix_norm), bf(w_in1), l1_ssd_conv_w.astype(F32), row(l1_ssd_conv_b),
        pad_heads(l1_ssd_dt_bias), pad_heads(l1_ssd_A_log), row(jnp.repeat(l1_ssd_D, SSD_HEAD_DIM)),
        row(l1_ssd_norm), row(jnp.tile(l1_sb_q_norm, SB_HEADS)), row(jnp.tile(l1_sb_k_norm, SB_HEADS)), seg)
    idx = jnp.arange(SB_BLOCK)
    tri = jnp.concatenate([(idx[:, None] >= idx[None, :]).astype(BF16),
                           jnp.ones((SB_BLOCK, SB_BLOCK), BF16)], axis=1)
    tri = jnp.concatenate([tri, tri], axis=0)
    d_out = _stickbreak(qn, kn, vv, tri)
    w_out1 = bf(l1_w_out)
    xf = ffn(xf, [(c_out.reshape(t, -1), w_out1[:SSD_WIDTH]), (d_out.reshape(t, -1), w_out1[SSD_WIDTH:])],
             l1_ffn2_norm, l1_ffn2_wg, l1_ffn2_wu, l1_ffn2_wd)
    return xf.reshape(bsz, s, d)
```

```python
import functools
import math

import jax
import jax.numpy as jnp
from jax import lax
from jax.experimental import pallas as pl
from jax.experimental.pallas import tpu as pltpu

F32 = jnp.float32
BF16 = jnp.bfloat16

EPS = 1e-6
FFN_RES = 0.5
SHORT_CONV = 4
LANES = 128

POOL_WINDOWS = (2, 4, 8, 16)
POOL_WIDTH = 512
MLSTM_HEADS = 4
MLSTM_HEAD_DIM = 128
MLSTM_WIDTH = MLSTM_HEADS * MLSTM_HEAD_DIM
CHUNK = 128

SSD_HEADS = 16
SSD_HEAD_DIM = 64
SSD_WIDTH = SSD_HEADS * SSD_HEAD_DIM
SSD_GROUPS = 4
SSD_STATE = 128
SSD_GROUP_WIDTH = SSD_WIDTH // SSD_GROUPS
SB_HEADS = 8
SB_HEAD_DIM = 64
SB_WIDTH = SB_HEADS * SB_HEAD_DIM
SB_BLOCK = 128
SB_EXIT = 104.0
SB_NEAR = 3

VMEM_LIMIT_BYTES = 56 * 1024 * 1024


def _cparams(semantics):
    return pltpu.CompilerParams(dimension_semantics=semantics, vmem_limit_bytes=VMEM_LIMIT_BYTES)


def _rmsnorm(x, w):
    return x * lax.rsqrt(jnp.mean(x * x, axis=-1, keepdims=True) + EPS) * w


def _sigmoid(x):
    return 1.0 / (1.0 + jnp.exp(-x))


def _sigmoid_tanh(x):
    return 0.5 + 0.5 * jnp.tanh(0.5 * x)


def _silu_tanh(x):
    h = 0.5 * x
    return h + h * jnp.tanh(h)


def _softplus(x):
    return jnp.maximum(x, 0.0) + jnp.log(1.0 + jnp.exp(-jnp.abs(x)))


def _dot(a, b):
    return jnp.dot(a, b, preferred_element_type=F32)


def _dot_nt(a, b):
    return lax.dot_general(a, b, (((1,), (1,)), ((), ())), preferred_element_type=F32)


def _split_bf16(x):
    hi = x.astype(BF16)
    lo = (x - hi.astype(F32)).astype(BF16)
    return hi, lo


def _row_iota(shape):
    return lax.broadcasted_iota(jnp.int32, shape, 0)


def _lane_iota(shape):
    return lax.broadcasted_iota(jnp.int32, shape, 1)


def _chunk_scan(x, op, fill):
    rows = _row_iota(x.shape)
    sh = 1
    while sh < x.shape[0]:
        shifted = jnp.where(rows >= sh, pltpu.roll(x, sh, 0), fill)
        x = op(x, shifted)
        sh *= 2
    return x


def _causal_conv(buf_ref, raw, w_ref, b_ref):
    ts = raw.shape[0]
    buf_ref[CONV_HALO:, :] = raw
    acc = raw * w_ref[SHORT_CONV - 1:SHORT_CONV, :] + b_ref[...]
    for k in range(SHORT_CONV - 1):
        off = CONV_HALO - (SHORT_CONV - 1 - k)
        acc = acc + buf_ref[off:off + ts, :] * w_ref[k:k + 1, :]
    buf_ref[0:CONV_HALO, :] = raw[ts - CONV_HALO:, :]
    return acc


def _staggered(gens):
    pending, live = list(gens), []
    while pending or live:
        if pending:
            live.append(pending.pop(0))
        for g in list(live):
            try:
                next(g)
            except StopIteration:
                live.remove(g)


def _expand_heads(slab, n_heads, head_dim):
    per = LANES // head_dim
    m = slab.shape[0]
    lane = _lane_iota((m, LANES))
    pieces = []
    for p in range(n_heads // per):
        piece = jnp.broadcast_to(slab[:, p * per:p * per + 1], (m, LANES))
        for j in range(1, per):
            piece = jnp.where(lane >= j * head_dim, slab[:, p * per + j:p * per + j + 1], piece)
        pieces.append(piece)
    return jnp.concatenate(pieces, axis=1)


def _ffn_kernel(*refs, n_mix, ff_chunk):
    x_ref = refs[0]
    mix_refs = refs[1:1 + 2 * n_mix]
    nw_ref, wg_ref, wu_ref, wd_ref, o_ref = refs[1 + 2 * n_mix:]
    x = x_ref[...]
    for i in range(n_mix):
        x = x + _dot(mix_refs[2 * i][...], mix_refs[2 * i + 1][...])
    h = _rmsnorm(x, nw_ref[...]).astype(BF16)
    d_ff = wg_ref.shape[1]
    y = None
    for c0 in range(0, d_ff, ff_chunk):
        c1 = min(c0 + ff_chunk, d_ff)
        g = _dot(h, wg_ref[:, c0:c1])
        u = _dot(h, wu_ref[:, c0:c1])
        a = (g * _sigmoid(g) * u).astype(BF16)
        part = _dot(a, wd_ref[c0:c1, :])
        y = part if y is None else y + part
    o_ref[...] = x + FFN_RES * y


def _ffn(x, mixes, norm_w, wg, wu, wd, *, tm=512, ff_chunk=512):
    t, d = x.shape
    d_ff = wg.shape[1]
    const = lambda i: (0, 0)
    in_specs = [pl.BlockSpec((tm, d), lambda i: (i, 0))]
    args = [x]
    for mix, w_out in mixes:
        in_specs += [pl.BlockSpec((tm, mix.shape[1]), lambda i: (i, 0)),
                     pl.BlockSpec(w_out.shape, const)]
        args += [mix, w_out]
    in_specs += [pl.BlockSpec((1, d), const), pl.BlockSpec((d, d_ff), const),
                 pl.BlockSpec((d, d_ff), const), pl.BlockSpec((d_ff, d), const)]
    args += [norm_w, wg, wu, wd]
    return pl.pallas_call(
        functools.partial(_ffn_kernel, n_mix=len(mixes), ff_chunk=ff_chunk),
        grid=(t // tm,),
        in_specs=in_specs,
        out_specs=pl.BlockSpec((tm, d), lambda i: (i, 0)),
        out_shape=jax.ShapeDtypeStruct((t, d), F32),
        compiler_params=_cparams(("parallel",)),
        name="ffn",
    )(*args)


EVEN_U = 0
EVEN_QK = EVEN_U + POOL_WIDTH
EVEN_V = EVEN_QK + 2 * MLSTM_WIDTH
EVEN_O = EVEN_V + MLSTM_WIDTH
EVEN_GI = EVEN_O + MLSTM_WIDTH
EVEN_GF = EVEN_GI + LANES
EVEN_COLS = EVEN_GF + LANES
POOL_HALO = 16
CONV_HALO = 8


def _even_kernel(x_ref, nw_ref, win_ref, poolw_ref, pscale_ref, convw_ref, convb_ref,
                 gbias_ref, mnorm_ref, o_ref,
                 halo_u, halo_qk, state, m_state, *, ts):
    s_idx = pl.program_id(1)

    @pl.when(s_idx == 0)
    def _():
        halo_u[...] = jnp.zeros_like(halo_u)
        halo_qk[0:CONV_HALO, :] = jnp.zeros((CONV_HALO, halo_qk.shape[1]), F32)
        state[...] = jnp.zeros_like(state)
        m_state[...] = jnp.zeros_like(m_state)

    h = _rmsnorm(x_ref[...], nw_ref[...]).astype(BF16)
    full_proj = _dot(h, win_ref[...])
    proj = lambda c0, n: full_proj[:, c0:c0 + n]

    u = proj(EVEN_U, POOL_WIDTH)
    ue = jnp.concatenate([halo_u[...], u], axis=0)
    halo_u[...] = u[ts - POOL_HALO:, :]
    pos = (s_idx * ts + 1 + _row_iota((ts, 1))).astype(F32)
    for g, win in enumerate(POOL_WINDOWS):
        acc = ue[:, g * LANES:(g + 1) * LANES]
        sh = 1
        while sh < win:
            acc = acc + pltpu.roll(acc, sh, 0)
            sh *= 2
        win_sum = acc[POOL_HALO:, :]
        pooled = win_sum / jnp.minimum(pos, float(win)) - u[:, g * LANES:(g + 1) * LANES]
        mixed = _dot(pooled.astype(BF16), poolw_ref[g])
        o_ref[:, g * LANES:(g + 1) * LANES] = (
            mixed * pscale_ref[:, g * LANES:(g + 1) * LANES]).astype(o_ref.dtype)

    qk_raw = proj(EVEN_QK, 2 * MLSTM_WIDTH)
    conv = _causal_conv(halo_qk, qk_raw, convw_ref, convb_ref)
    qk = _silu_tanh(conv)
    q_all = qk[:, :MLSTM_WIDTH].astype(BF16)
    k_all = qk[:, MLSTM_WIDTH:] * (MLSTM_HEAD_DIM ** -0.5)
    v_all = proj(EVEN_V, MLSTM_WIDTH).astype(BF16)
    o_gate = _sigmoid_tanh(proj(EVEN_O, MLSTM_WIDTH))

    g_i = proj(EVEN_GI, LANES) + gbias_ref[0:1, :]
    g_f = -_softplus(-(proj(EVEN_GF, LANES) + gbias_ref[1:2, :]))

    ones_blk = jnp.ones((CHUNK, MLSTM_HEAD_DIM), BF16)
    causal = _row_iota((CHUNK, CHUNK)) >= _lane_iota((CHUNK, CHUNK))
    for c in range(ts // CHUNK):
        r0, r1 = c * CHUNK, (c + 1) * CHUNK
        li = g_i[r0:r1, :]
        b = _chunk_scan(g_f[r0:r1, :], jnp.add, 0.0)
        a = li - b
        m_prev = m_state[...]
        big_m = jnp.maximum(m_prev, _chunk_scan(a, jnp.maximum, -jnp.inf))
        b_last = b[CHUNK - 1:CHUNK, :]
        m_new = b_last + big_m[CHUNK - 1:CHUNK, :]
        w_inter = jnp.exp(m_prev - big_m)
        e_negm = jnp.exp(-(b + big_m))
        w_state = jnp.exp(b_last + a - m_new)
        decay = jnp.exp(b_last + m_prev - m_new)
        m_state[...] = m_new
        a_t = a.T
        def head_pipeline(hd):
            c0, c1 = hd * MLSTM_HEAD_DIM, (hd + 1) * MLSTM_HEAD_DIM
            qc = q_all[r0:r1, c0:c1]
            kc = k_all[r0:r1, c0:c1]
            v_ext = jnp.concatenate([v_all[r0:r1, c0:c1], ones_blk], axis=1)
            st = state[hd]
            s_qk = _dot_nt(qc, kc.astype(BF16))
            inter = _dot(qc, st.astype(BF16))
            yield
            w_intra = jnp.exp(jnp.where(causal, a_t[hd:hd + 1, :] - big_m[:, hd:hd + 1], -jnp.inf))
            p = (s_qk * w_intra).astype(BF16)
            kw_t = (kc * w_state[:, hd:hd + 1]).T.astype(BF16)
            yield
            pv = _dot(p, v_ext)
            state[hd] = decay[:, hd:hd + 1] * st + _dot(kw_t, v_ext)
            yield
            numden = w_inter[:, hd:hd + 1] * inter + pv
            den = jnp.maximum(jnp.abs(numden[:, MLSTM_HEAD_DIM:MLSTM_HEAD_DIM + 1]),
                              e_negm[:, hd:hd + 1])
            hh = numden[:, :MLSTM_HEAD_DIM] / den
            hn = _rmsnorm(hh, mnorm_ref[:, c0:c1])
            o_ref[r0:r1, POOL_WIDTH + c0:POOL_WIDTH + c1] = (
                o_gate[r0:r1, c0:c1] * hn).astype(o_ref.dtype)

        _staggered([head_pipeline(hd) for hd in range(MLSTM_HEADS)])


def _even_mixer(x, norm_w, w_in, pool_w, pool_scale, conv_w, conv_b, gate_bias, mnorm, *, ts=512):
    bsz, s, d = x.shape
    const2 = lambda b, i: (0, 0)
    return pl.pallas_call(
        functools.partial(_even_kernel, ts=ts),
        grid=(bsz, s // ts),
        in_specs=[
            pl.BlockSpec((None, ts, d), lambda b, i: (b, i, 0)),
            pl.BlockSpec((1, d), const2),
            pl.BlockSpec(w_in.shape, const2),
            pl.BlockSpec(pool_w.shape, lambda b, i: (0, 0, 0)),
            pl.BlockSpec(pool_scale.shape, const2),
            pl.BlockSpec(conv_w.shape, const2),
            pl.BlockSpec(conv_b.shape, const2),
            pl.BlockSpec(gate_bias.shape, const2),
            pl.BlockSpec(mnorm.shape, const2),
        ],
        out_specs=pl.BlockSpec((None, ts, POOL_WIDTH + MLSTM_WIDTH), lambda b, i: (b, i, 0)),
        out_shape=jax.ShapeDtypeStruct((bsz, s, POOL_WIDTH + MLSTM_WIDTH), BF16),
        scratch_shapes=[
            pltpu.VMEM((POOL_HALO, POOL_WIDTH), F32),
            pltpu.VMEM((CONV_HALO + ts, 2 * MLSTM_WIDTH), F32),
            pltpu.VMEM((MLSTM_HEADS, MLSTM_HEAD_DIM, 2 * MLSTM_HEAD_DIM), F32),
            pltpu.VMEM((1, LANES), F32),
        ],
        compiler_params=_cparams(("parallel", "arbitrary")),
        name="even_mixer",
    )(x, norm_w, w_in, pool_w, pool_scale, conv_w, conv_b, gate_bias, mnorm)


ODD_Z = 0
ODD_XBC = ODD_Z + SSD_WIDTH
ODD_XS = ODD_XBC
ODD_B = ODD_XS + SSD_WIDTH
ODD_C = ODD_B + SSD_GROUPS * SSD_STATE
ODD_Q = ODD_C + SSD_GROUPS * SSD_STATE
ODD_K = ODD_Q + SB_WIDTH
ODD_V = ODD_K + SB_WIDTH
ODD_DT = ODD_V + SB_WIDTH
ODD_COLS = ODD_DT + LANES
SSD_XBC = SSD_WIDTH + 2 * SSD_GROUPS * SSD_STATE


def _odd_kernel(x_ref, nw_ref, win_ref, convw_ref, convb_ref, dtb_ref, alog_ref, dskip_ref,
                snorm_ref, qn_ref, kn_ref, seg_ref,
                c_ref, q_ref, k_ref, v_ref,
                halo, hstate, *, ts, nb):
    @pl.when(pl.program_id(1) == 0)
    def _():
        halo[:, 0:CONV_HALO, :] = jnp.zeros((nb, CONV_HALO, halo.shape[2]), F32)
        hstate[...] = jnp.zeros_like(hstate)

    for bb in range(nb):
        _odd_tile(x_ref.at[bb], nw_ref, win_ref, convw_ref, convb_ref, dtb_ref, alog_ref, dskip_ref,
                  snorm_ref, qn_ref, kn_ref, seg_ref,
                  c_ref.at[bb], q_ref.at[bb], k_ref.at[bb], v_ref.at[bb], halo.at[bb], hstate.at[bb], ts=ts)


def _odd_tile(x_ref, nw_ref, win_ref, convw_ref, convb_ref, dtb_ref, alog_ref, dskip_ref,
              snorm_ref, qn_ref, kn_ref, seg_ref,
              c_ref, q_ref, k_ref, v_ref,
              halo, hstate, *, ts):
    h = _rmsnorm(x_ref[...], nw_ref[...]).astype(BF16)
    proj = lambda c0, n: _dot(h, win_ref[:, c0:c0 + n])

    seg = seg_ref[...]
    for src, nref, dst, scale in ((ODD_Q, qn_ref, q_ref, SB_HEAD_DIM ** -0.5), (ODD_K, kn_ref, k_ref, 1.0)):
        t = proj(src, SB_WIDTH)
        hi, lo = _split_bf16(t * t)
        ssq = _dot(hi, seg) + _dot(lo, seg)
        dst[...] = (t * lax.rsqrt(ssq * (1.0 / SB_HEAD_DIM) + EPS) * (nref[...] * scale)).astype(dst.dtype)
    v_ref[...] = proj(ODD_V, SB_WIDTH).astype(v_ref.dtype)

    raw = proj(ODD_XBC, SSD_XBC)
    conv = _causal_conv(halo, raw, convw_ref, convb_ref)
    xbc = _silu_tanh(conv)
    xs_all = xbc[:, :SSD_WIDTH]
    bm_all = xbc[:, SSD_WIDTH:SSD_WIDTH + SSD_GROUPS * SSD_STATE].astype(BF16)
    cm_all = xbc[:, SSD_WIDTH + SSD_GROUPS * SSD_STATE:].astype(BF16)

    dt_all = _softplus(proj(ODD_DT, LANES) + dtb_ref[...])
    a_all = dt_all * (-jnp.exp(alog_ref[...]))
    zg = proj(ODD_Z, SSD_WIDTH)
    zg = _silu_tanh(zg)

    causal = _row_iota((CHUNK, CHUNK)) >= _lane_iota((CHUNK, CHUNK))
    lane = _lane_iota((CHUNK, LANES))
    heads_per_group = SSD_HEADS // SSD_GROUPS
    for c in range(ts // CHUNK):
        r0, r1 = c * CHUNK, (c + 1) * CHUNK
        xs = xs_all[r0:r1, :]
        a_cum = _chunk_scan(a_all[r0:r1, :], jnp.add, 0.0)
        a_last = a_cum[CHUNK - 1:CHUNK, :]
        a_t = a_cum.T
        xdt = xs * _expand_heads(dt_all[r0:r1, :], SSD_HEADS, SSD_HEAD_DIM)
        a_wide = _expand_heads(a_cum, SSD_HEADS, SSD_HEAD_DIM)
        a_last_wide = a_wide[CHUNK - 1:CHUNK, :]
        xw = (xdt * jnp.exp(a_last_wide - a_wide)).astype(BF16)
        xdt = xdt.astype(BF16)
        from_start = jnp.exp(a_wide)
        chunk_decay = jnp.exp(a_last_wide)

        for g in range(SSD_GROUPS):
            bg = bm_all[r0:r1, g * SSD_STATE:(g + 1) * SSD_STATE]
            cg = cm_all[r0:r1, g * SSD_STATE:(g + 1) * SSD_STATE]
            gs = slice(g * SSD_GROUP_WIDTH, (g + 1) * SSD_GROUP_WIDTH)
            cb = _dot_nt(cg, bg)
            hprev = hstate[g]
            y_off = _dot(cg, hprev.astype(BF16))
            bg_t = bg.astype(F32).T.astype(BF16)
            hstate[g] = chunk_decay[:, gs] * hprev + _dot(bg_t, xw[:, gs])
            masked = []
            for hh in range(g * heads_per_group, (g + 1) * heads_per_group):
                dec = jnp.exp(jnp.where(causal, a_cum[:, hh:hh + 1] - a_t[hh:hh + 1, :], -jnp.inf))
                masked.append((cb * dec).astype(BF16))
            diag = []
            for pair in range(heads_per_group // 2):
                ps = slice(g * SSD_GROUP_WIDTH + pair * LANES, g * SSD_GROUP_WIDTH + (pair + 1) * LANES)
                outs = [_dot(masked[2 * pair + j], xdt[:, ps]) for j in range(2)]
                diag.append(jnp.where(lane < SSD_HEAD_DIM, outs[0], outs[1]))
            y = jnp.concatenate(diag, axis=1) + y_off * from_start[:, gs] + dskip_ref[:, gs] * xs[:, gs]
            c_ref[r0:r1, gs] = _rmsnorm(y * zg[r0:r1, gs], snorm_ref[:, gs]).astype(c_ref.dtype)


def _odd_mixer(x, norm_w, w_in, conv_w, conv_b, dt_bias, a_log, d_skip, snorm, qn, kn, seg, *, ts=512, nb=1):
    bsz, s, d = x.shape
    const2 = lambda b, i: (0, 0)
    tile = lambda w: pl.BlockSpec((nb, ts, w), lambda b, i: (b, i, 0))
    return pl.pallas_call(
        functools.partial(_odd_kernel, ts=ts, nb=nb),
        grid=(bsz // nb, s // ts),
        in_specs=[tile(d)] + [pl.BlockSpec(a.shape, const2) for a in
                              (norm_w, w_in, conv_w, conv_b, dt_bias, a_log, d_skip, snorm, qn, kn, seg)],
        out_specs=[tile(SSD_WIDTH), tile(SB_WIDTH), tile(SB_WIDTH), tile(SB_WIDTH)],
        out_shape=[jax.ShapeDtypeStruct((bsz, s, SSD_WIDTH), BF16)]
        + [jax.ShapeDtypeStruct((bsz, s, SB_WIDTH), BF16)] * 3,
        scratch_shapes=[
            pltpu.VMEM((nb, CONV_HALO + ts, SSD_XBC), F32),
            pltpu.VMEM((nb, SSD_GROUPS, SSD_STATE, SSD_GROUP_WIDTH), F32),
        ],
        compiler_params=_cparams(("parallel", "arbitrary")),
        name="odd_mixer",
    )(x, norm_w, w_in, conv_w, conv_b, dt_bias, a_log, d_skip, snorm, qn, kn, seg)


def _sb_kernel(q_ref, k_ref, v_ref, tri_ref, o_ref, acc_ref, r_ref, rmin_ref):
    qi = pl.program_id(1)
    blk = SB_BLOCK
    pairs = SB_WIDTH // LANES
    first = _lane_iota((blk, LANES)) < SB_HEAD_DIM
    q = q_ref[...]
    zero = jnp.zeros((blk, LANES), q.dtype)
    q_stack = []
    for p in range(pairs):
        qp = q[:, p * LANES:(p + 1) * LANES]
        q_stack.append(jnp.concatenate([jnp.where(first, qp, zero), jnp.where(first, zero, qp)], axis=0))
    tri = tri_ref[...]
    strict = jnp.concatenate([_row_iota((blk, blk)) > _lane_iota((blk, blk))] * 2, axis=0)

    def pair_pipeline(p, j_lo, nblk, diag, out):
        k0 = pl.multiple_of(j_lo * blk, blk)
        cols = slice(p * LANES, (p + 1) * LANES)
        r = None if diag else r_ref[2 * p * blk:2 * (p + 1) * blk, :]
        z = _dot_nt(q_stack[p], k_ref[pl.ds(k0, nblk * blk), cols])
        yield
        sp = _softplus(z)
        sp_near_first = [sp[:, b * blk:(b + 1) * blk] for b in reversed(range(nblk))]
        if diag:
            sp_near_first[0] = jnp.where(strict, sp_near_first[0], 0.0)
        hi, lo = _split_bf16(jnp.concatenate(sp_near_first, axis=0))
        yield
        cs = _dot(jnp.concatenate([hi, lo], axis=1), tri)
        yield
        ws = []
        for i in range(nblk):
            b = nblk - 1 - i
            cs_b = cs[2 * i * blk:2 * (i + 1) * blk]
            arg = z[:, b * blk:(b + 1) * blk] - cs_b[:, :blk]
            w_b = jnp.exp(arg if r is None else arg - r)
            if diag and i == 0:
                w_b = jnp.where(strict, w_b, 0.0)
            ws.append(w_b.astype(BF16))
            r = cs_b[:, blk:] if r is None else r + cs_b[:, blk:]
        w = jnp.concatenate(ws[::-1], axis=1)
        yield
        pv = _dot(w, v_ref[pl.ds(k0, nblk * blk), cols])
        out(p, jnp.where(first, pv[:blk], pv[blk:]), r)

    def process(j_lo, nblk, diag):
        rmins = []

        def out(p, acc, r):
            cols = slice(p * LANES, (p + 1) * LANES)
            if diag:
                acc_ref[:, cols] = acc
            else:
                acc_ref[:, cols] += acc
            r_ref[2 * p * blk:2 * (p + 1) * blk, :] = r
            rmins.append(jnp.min(r))

        _staggered([pair_pipeline(p, j_lo, nblk, diag, out) for p in range(pairs)])
        return functools.reduce(jnp.minimum, rmins)

    def near(j_lo, nblk):
        rmin_ref[0] = process(j_lo, nblk, True)

    @pl.when(qi >= SB_NEAR - 1)
    def _():
        near(qi - (SB_NEAR - 1), SB_NEAR)

    @pl.when(qi < SB_NEAR - 1)
    def _():
        near(qi, 1)

    def cond(c):
        return jnp.logical_and(c[0] >= 0, c[1] < SB_EXIT)

    def body(c):
        return c[0] - 1, process(c[0], 1, False)

    j_start = jnp.where(qi >= SB_NEAR - 1, qi - SB_NEAR, qi - 1)
    lax.while_loop(cond, body, (j_start, rmin_ref[0]))
    o_ref[...] = acc_ref[...].astype(o_ref.dtype)


def _stickbreak(q, k, v, tri):
    bsz, s, w = q.shape
    return pl.pallas_call(
        _sb_kernel,
        grid=(bsz, s // SB_BLOCK),
        in_specs=[
            pl.BlockSpec((None, SB_BLOCK, w), lambda b, i: (b, i, 0)),
            pl.BlockSpec((None, s, w), lambda b, i: (b, 0, 0)),
            pl.BlockSpec((None, s, w), lambda b, i: (b, 0, 0)),
            pl.BlockSpec(tri.shape, lambda b, i: (0, 0)),
        ],
        out_specs=pl.BlockSpec((None, SB_BLOCK, w), lambda b, i: (b, i, 0)),
        out_shape=jax.ShapeDtypeStruct((bsz, s, w), BF16),
        scratch_shapes=[pltpu.VMEM((SB_BLOCK, w), F32),
                        pltpu.VMEM((2 * (w // LANES) * SB_BLOCK, SB_BLOCK), F32),
                        pltpu.SMEM((1,), F32)],
        compiler_params=_cparams(("parallel", "arbitrary")),
        name="stickbreak",
    )(q, k, v, tri)


def _pad_cols(w, n):
    return jnp.pad(w, ((0, 0), (0, n - w.shape[1])))


def kernel(x, l0_ffn1_norm, l0_ffn1_wg, l0_ffn1_wu, l0_ffn1_wd, l0_mix_norm, l0_w_in, l0_pool_w, l0_pool_scale, l0_qk_conv_w, l0_qk_conv_b, l0_gate_bias, l0_mlstm_norm, l0_w_out, l0_ffn2_norm, l0_ffn2_wg, l0_ffn2_wu, l0_ffn2_wd, l1_ffn1_norm, l1_ffn1_wg, l1_ffn1_wu, l1_ffn1_wd, l1_mix_norm, l1_w_in, l1_ssd_conv_w, l1_ssd_conv_b, l1_ssd_dt_bias, l1_ssd_A_log, l1_ssd_D, l1_ssd_norm, l1_sb_q_norm, l1_sb_k_norm, l1_w_out, l1_ffn2_norm, l1_ffn2_wg, l1_ffn2_wu, l1_ffn2_wd):
    bsz, s, d = x.shape
    t = bsz * s
    row = lambda a: a.reshape(1, -1).astype(F32)
    bf = lambda a: a.astype(BF16)

    def ffn(xf, mixes, norm_w, wg, wu, wd):
        return _ffn(xf, mixes, row(norm_w), bf(wg), bf(wu), bf(wd))

    xf = ffn(x.reshape(t, d), [], l0_ffn1_norm, l0_ffn1_wg, l0_ffn1_wu, l0_ffn1_wd)
    n_main = EVEN_GI
    w_in0 = jnp.concatenate([
        l0_w_in[:, :n_main],
        _pad_cols(l0_w_in[:, n_main:n_main + MLSTM_HEADS], LANES),
        _pad_cols(l0_w_in[:, n_main + MLSTM_HEADS:], LANES)], axis=1)
    gbias = jnp.stack([jnp.pad(l0_gate_bias[:MLSTM_HEADS], (0, LANES - MLSTM_HEADS)),
                       jnp.pad(l0_gate_bias[MLSTM_HEADS:], (0, LANES - MLSTM_HEADS))]).astype(F32)
    mix0 = _even_mixer(xf.reshape(bsz, s, d), row(l0_mix_norm), bf(w_in0), bf(l0_pool_w),
                       row(l0_pool_scale), l0_qk_conv_w.astype(F32), row(l0_qk_conv_b), gbias,
                       row(l0_mlstm_norm))
    xf = ffn(xf, [(mix0.reshape(t, -1), bf(l0_w_out))], l0_ffn2_norm, l0_ffn2_wg, l0_ffn2_wu, l0_ffn2_wd)

    xf = ffn(xf, [], l1_ffn1_norm, l1_ffn1_wg, l1_ffn1_wu, l1_ffn1_wd)
    c_dt = SSD_WIDTH + SSD_XBC
    c_q = c_dt + SSD_HEADS
    w_in1 = jnp.concatenate([l1_w_in[:, :c_dt], l1_w_in[:, c_q:],
                             _pad_cols(l1_w_in[:, c_dt:c_q], LANES)], axis=1)
    pad_heads = lambda a: jnp.pad(a.astype(F32), (0, LANES - SSD_HEADS)).reshape(1, LANES)
    seg_id = jnp.arange(SB_WIDTH) // SB_HEAD_DIM
    seg = (seg_id[:, None] == seg_id[None, :]).astype(BF16)
    c_out, qn, kn, vv = _odd_mixer(
        xf.reshape(bsz, s, d), row(l1_mix_norm), bf(w_in1), l1_ssd_conv_w.astype(F32), row(l1_ssd_conv_b),
        pad_heads(l1_ssd_dt_bias), pad_heads(l1_ssd_A_log), row(jnp.repeat(l1_ssd_D, SSD_HEAD_DIM)),
        row(l1_ssd_norm), row(jnp.tile(l1_sb_q_norm, SB_HEADS)), row(jnp.tile(l1_sb_k_norm, SB_HEADS)), seg)
    idx = jnp.arange(SB_BLOCK)
    tri = jnp.concatenate([(idx[:, None] >= idx[None, :]).astype(BF16),
                           jnp.ones((SB_BLOCK, SB_BLOCK), BF16)], axis=1)
    tri = jnp.concatenate([tri, tri], axis=0)
    d_out = _stickbreak(qn, kn, vv, tri)
    w_out1 = bf(l1_w_out)
    xf = ffn(xf, [(c_out.reshape(t, -1), w_out1[:SSD_WIDTH]), (d_out.reshape(t, -1), w_out1[SSD_WIDTH:])],
             l1_ffn2_norm, l1_ffn2_wg, l1_ffn2_wu, l1_ffn2_wd)
    return xf.reshape(bsz, s, d)
```

```python
import functools
import math

import jax
import jax.numpy as jnp
from jax import lax
from jax.experimental import pallas as pl
from jax.experimental.pallas import tpu as pltpu

F32 = jnp.float32
BF16 = jnp.bfloat16

EPS = 1e-6
FFN_RES = 0.5
SHORT_CONV = 4
LANES = 128

POOL_WINDOWS = (2, 4, 8, 16)
POOL_WIDTH = 512
MLSTM_HEADS = 4
MLSTM_HEAD_DIM = 128
MLSTM_WIDTH = MLSTM_HEADS * MLSTM_HEAD_DIM
CHUNK = 128

SSD_HEADS = 16
SSD_HEAD_DIM = 64
SSD_WIDTH = SSD_HEADS * SSD_HEAD_DIM
SSD_GROUPS = 4
SSD_STATE = 128
SSD_GROUP_WIDTH = SSD_WIDTH // SSD_GROUPS
SB_HEADS = 8
SB_HEAD_DIM = 64
SB_WIDTH = SB_HEADS * SB_HEAD_DIM
SB_BLOCK = 128
SB_EXIT = 104.0
SB_TILES = 4
SB_NEAR = 3

VMEM_LIMIT_BYTES = 56 * 1024 * 1024


def _cparams(semantics):
    return pltpu.CompilerParams(dimension_semantics=semantics, vmem_limit_bytes=VMEM_LIMIT_BYTES)


def _rmsnorm(x, w):
    return x * lax.rsqrt(jnp.mean(x * x, axis=-1, keepdims=True) + EPS) * w


def _sigmoid(x):
    return 1.0 / (1.0 + jnp.exp(-x))


def _sigmoid_tanh(x):
    return 0.5 + 0.5 * jnp.tanh(0.5 * x)


def _silu_tanh(x):
    h = 0.5 * x
    return h + h * jnp.tanh(h)


def _softplus(x):
    return jnp.maximum(x, 0.0) + jnp.log(1.0 + jnp.exp(-jnp.abs(x)))


def _dot(a, b):
    return jnp.dot(a, b, preferred_element_type=F32)


def _dot_nt(a, b):
    return lax.dot_general(a, b, (((1,), (1,)), ((), ())), preferred_element_type=F32)


def _split_bf16(x):
    hi = x.astype(BF16)
    lo = (x - hi.astype(F32)).astype(BF16)
    return hi, lo


def _row_iota(shape):
    return lax.broadcasted_iota(jnp.int32, shape, 0)


def _lane_iota(shape):
    return lax.broadcasted_iota(jnp.int32, shape, 1)


def _chunk_scan(x, op, fill):
    rows = _row_iota(x.shape)
    sh = 1
    while sh < x.shape[0]:
        shifted = jnp.where(rows >= sh, pltpu.roll(x, sh, 0), fill)
        x = op(x, shifted)
        sh *= 2
    return x


def _causal_conv(buf_ref, raw, w_ref, b_ref):
    ts = raw.shape[0]
    buf_ref[CONV_HALO:, :] = raw
    acc = raw * w_ref[SHORT_CONV - 1:SHORT_CONV, :] + b_ref[...]
    for k in range(SHORT_CONV - 1):
        off = CONV_HALO - (SHORT_CONV - 1 - k)
        acc = acc + buf_ref[off:off + ts, :] * w_ref[k:k + 1, :]
    buf_ref[0:CONV_HALO, :] = raw[ts - CONV_HALO:, :]
    return acc


def _staggered(gens):
    pending, live = list(gens), []
    while pending or live:
        if pending:
            live.append(pending.pop(0))
        for g in list(live):
            try:
                next(g)
            except StopIteration:
                live.remove(g)


def _expand_heads(slab, n_heads, head_dim):
    per = LANES // head_dim
    m = slab.shape[0]
    lane = _lane_iota((m, LANES))
    pieces = []
    for p in range(n_heads // per):
        piece = jnp.broadcast_to(slab[:, p * per:p * per + 1], (m, LANES))
        for j in range(1, per):
            piece = jnp.where(lane >= j * head_dim, slab[:, p * per + j:p * per + j + 1], piece)
        pieces.append(piece)
    return jnp.concatenate(pieces, axis=1)


def _ffn_kernel(*refs, n_mix, ff_chunk):
    x_ref = refs[0]
    mix_refs = refs[1:1 + 2 * n_mix]
    nw_ref, wg_ref, wu_ref, wd_ref, o_ref = refs[1 + 2 * n_mix:]
    x = x_ref[...]
    for i in range(n_mix):
        x = x + _dot(mix_refs[2 * i][...], mix_refs[2 * i + 1][...])
    h = _rmsnorm(x, nw_ref[...]).astype(BF16)
    d_ff = wg_ref.shape[1]
    y = None
    for c0 in range(0, d_ff, ff_chunk):
        c1 = min(c0 + ff_chunk, d_ff)
        g = _dot(h, wg_ref[:, c0:c1])
        u = _dot(h, wu_ref[:, c0:c1])
        a = (g * _sigmoid(g) * u).astype(BF16)
        part = _dot(a, wd_ref[c0:c1, :])
        y = part if y is None else y + part
    o_ref[...] = x + FFN_RES * y


def _ffn(x, mixes, norm_w, wg, wu, wd, *, tm=512, ff_chunk=512):
    t, d = x.shape
    d_ff = wg.shape[1]
    const = lambda i: (0, 0)
    in_specs = [pl.BlockSpec((tm, d), lambda i: (i, 0))]
    args = [x]
    for mix, w_out in mixes:
        in_specs += [pl.BlockSpec((tm, mix.shape[1]), lambda i: (i, 0)),
                     pl.BlockSpec(w_out.shape, const)]
        args += [mix, w_out]
    in_specs += [pl.BlockSpec((1, d), const), pl.BlockSpec((d, d_ff), const),
                 pl.BlockSpec((d, d_ff), const), pl.BlockSpec((d_ff, d), const)]
    args += [norm_w, wg, wu, wd]
    return pl.pallas_call(
        functools.partial(_ffn_kernel, n_mix=len(mixes), ff_chunk=ff_chunk),
        grid=(t // tm,),
        in_specs=in_specs,
        out_specs=pl.BlockSpec((tm, d), lambda i: (i, 0)),
        out_shape=jax.ShapeDtypeStruct((t, d), F32),
        compiler_params=_cparams(("parallel",)),
        name="ffn",
    )(*args)


EVEN_U = 0
EVEN_QK = EVEN_U + POOL_WIDTH
EVEN_V = EVEN_QK + 2 * MLSTM_WIDTH
EVEN_O = EVEN_V + MLSTM_WIDTH
EVEN_GI = EVEN_O + MLSTM_WIDTH
EVEN_GF = EVEN_GI + LANES
EVEN_COLS = EVEN_GF + LANES
POOL_HALO = 16
CONV_HALO = 8


def _even_kernel(x_ref, nw_ref, win_ref, poolw_ref, pscale_ref, convw_ref, convb_ref,
                 gbias_ref, mnorm_ref, o_ref,
                 halo_u, halo_qk, state, m_state, *, ts):
    s_idx = pl.program_id(1)

    @pl.when(s_idx == 0)
    def _():
        halo_u[...] = jnp.zeros_like(halo_u)
        halo_qk[0:CONV_HALO, :] = jnp.zeros((CONV_HALO, halo_qk.shape[1]), F32)
        state[...] = jnp.zeros_like(state)
        m_state[...] = jnp.zeros_like(m_state)

    h = _rmsnorm(x_ref[...], nw_ref[...]).astype(BF16)
    full_proj = _dot(h, win_ref[...])
    proj = lambda c0, n: full_proj[:, c0:c0 + n]

    u = proj(EVEN_U, POOL_WIDTH)
    ue = jnp.concatenate([halo_u[...], u], axis=0)
    halo_u[...] = u[ts - POOL_HALO:, :]
    pos = (s_idx * ts + 1 + _row_iota((ts, 1))).astype(F32)
    for g, win in enumerate(POOL_WINDOWS):
        acc = ue[:, g * LANES:(g + 1) * LANES]
        sh = 1
        while sh < win:
            acc = acc + pltpu.roll(acc, sh, 0)
            sh *= 2
        win_sum = acc[POOL_HALO:, :]
        pooled = win_sum / jnp.minimum(pos, float(win)) - u[:, g * LANES:(g + 1) * LANES]
        mixed = _dot(pooled.astype(BF16), poolw_ref[g])
        o_ref[:, g * LANES:(g + 1) * LANES] = (
            mixed * pscale_ref[:, g * LANES:(g + 1) * LANES]).astype(o_ref.dtype)

    qk_raw = proj(EVEN_QK, 2 * MLSTM_WIDTH)
    conv = _causal_conv(halo_qk, qk_raw, convw_ref, convb_ref)
    qk = _silu_tanh(conv)
    q_all = qk[:, :MLSTM_WIDTH].astype(BF16)
    k_all = qk[:, MLSTM_WIDTH:] * (MLSTM_HEAD_DIM ** -0.5)
    v_all = proj(EVEN_V, MLSTM_WIDTH).astype(BF16)
    o_gate = _sigmoid_tanh(proj(EVEN_O, MLSTM_WIDTH))

    g_i = proj(EVEN_GI, LANES) + gbias_ref[0:1, :]
    g_f = -_softplus(-(proj(EVEN_GF, LANES) + gbias_ref[1:2, :]))

    ones_blk = jnp.ones((CHUNK, MLSTM_HEAD_DIM), BF16)
    causal = _row_iota((CHUNK, CHUNK)) >= _lane_iota((CHUNK, CHUNK))
    for c in range(ts // CHUNK):
        r0, r1 = c * CHUNK, (c + 1) * CHUNK
        li = g_i[r0:r1, :]
        b = _chunk_scan(g_f[r0:r1, :], jnp.add, 0.0)
        a = li - b
        m_prev = m_state[...]
        big_m = jnp.maximum(m_prev, _chunk_scan(a, jnp.maximum, -jnp.inf))
        b_last = b[CHUNK - 1:CHUNK, :]
        m_new = b_last + big_m[CHUNK - 1:CHUNK, :]
        w_inter = jnp.exp(m_prev - big_m)
        e_negm = jnp.exp(-(b + big_m))
        w_state = jnp.exp(b_last + a - m_new)
        decay = jnp.exp(b_last + m_prev - m_new)
        m_state[...] = m_new
        a_t = a.T
        def head_pipeline(hd):
            c0, c1 = hd * MLSTM_HEAD_DIM, (hd + 1) * MLSTM_HEAD_DIM
            qc = q_all[r0:r1, c0:c1]
            kc = k_all[r0:r1, c0:c1]
            v_ext = jnp.concatenate([v_all[r0:r1, c0:c1], ones_blk], axis=1)
            st = state[hd]
            s_qk = _dot_nt(qc, kc.astype(BF16))
            inter = _dot(qc, st.astype(BF16))
            yield
            w_intra = jnp.exp(jnp.where(causal, a_t[hd:hd + 1, :] - big_m[:, hd:hd + 1], -jnp.inf))
            p = (s_qk * w_intra).astype(BF16)
            kw_t = (kc * w_state[:, hd:hd + 1]).T.astype(BF16)
            yield
            pv = _dot(p, v_ext)
            state[hd] = decay[:, hd:hd + 1] * st + _dot(kw_t, v_ext)
            yield
            numden = w_inter[:, hd:hd + 1] * inter + pv
            den = jnp.maximum(jnp.abs(numden[:, MLSTM_HEAD_DIM:MLSTM_HEAD_DIM + 1]),
                              e_negm[:, hd:hd + 1])
            hh = numden[:, :MLSTM_HEAD_DIM] / den
            hn = _rmsnorm(hh, mnorm_ref[:, c0:c1])
            o_ref[r0:r1, POOL_WIDTH + c0:POOL_WIDTH + c1] = (
                o_gate[r0:r1, c0:c1] * hn).astype(o_ref.dtype)

        _staggered([head_pipeline(hd) for hd in range(MLSTM_HEADS)])


def _even_mixer(x, norm_w, w_in, pool_w, pool_scale, conv_w, conv_b, gate_bias, mnorm, *, ts=512):
    bsz, s, d = x.shape
    const2 = lambda b, i: (0, 0)
    return pl.pallas_call(
        functools.partial(_even_kernel, ts=ts),
        grid=(bsz, s // ts),
        in_specs=[
            pl.BlockSpec((None, ts, d), lambda b, i: (b, i, 0)),
            pl.BlockSpec((1, d), const2),
            pl.BlockSpec(w_in.shape, const2),
            pl.BlockSpec(pool_w.shape, lambda b, i: (0, 0, 0)),
            pl.BlockSpec(pool_scale.shape, const2),
            pl.BlockSpec(conv_w.shape, const2),
            pl.BlockSpec(conv_b.shape, const2),
            pl.BlockSpec(gate_bias.shape, const2),
            pl.BlockSpec(mnorm.shape, const2),
        ],
        out_specs=pl.BlockSpec((None, ts, POOL_WIDTH + MLSTM_WIDTH), lambda b, i: (b, i, 0)),
        out_shape=jax.ShapeDtypeStruct((bsz, s, POOL_WIDTH + MLSTM_WIDTH), BF16),
        scratch_shapes=[
            pltpu.VMEM((POOL_HALO, POOL_WIDTH), F32),
            pltpu.VMEM((CONV_HALO + ts, 2 * MLSTM_WIDTH), F32),
            pltpu.VMEM((MLSTM_HEADS, MLSTM_HEAD_DIM, 2 * MLSTM_HEAD_DIM), F32),
            pltpu.VMEM((1, LANES), F32),
        ],
        compiler_params=_cparams(("parallel", "arbitrary")),
        name="even_mixer",
    )(x, norm_w, w_in, pool_w, pool_scale, conv_w, conv_b, gate_bias, mnorm)


ODD_Z = 0
ODD_XBC = ODD_Z + SSD_WIDTH
ODD_XS = ODD_XBC
ODD_B = ODD_XS + SSD_WIDTH
ODD_C = ODD_B + SSD_GROUPS * SSD_STATE
ODD_Q = ODD_C + SSD_GROUPS * SSD_STATE
ODD_K = ODD_Q + SB_WIDTH
ODD_V = ODD_K + SB_WIDTH
ODD_DT = ODD_V + SB_WIDTH
ODD_COLS = ODD_DT + LANES
SSD_XBC = SSD_WIDTH + 2 * SSD_GROUPS * SSD_STATE


def _odd_kernel(x_ref, nw_ref, win_ref, convw_ref, convb_ref, dtb_ref, alog_ref, dskip_ref,
                snorm_ref, qn_ref, kn_ref, seg_ref,
                c_ref, q_ref, k_ref, v_ref,
                halo, hstate, *, ts, nb):
    @pl.when(pl.program_id(1) == 0)
    def _():
        halo[:, 0:CONV_HALO, :] = jnp.zeros((nb, CONV_HALO, halo.shape[2]), F32)
        hstate[...] = jnp.zeros_like(hstate)

    for bb in range(nb):
        _odd_tile(x_ref.at[bb], nw_ref, win_ref, convw_ref, convb_ref, dtb_ref, alog_ref, dskip_ref,
                  snorm_ref, qn_ref, kn_ref, seg_ref,
                  c_ref.at[bb], q_ref.at[bb], k_ref.at[bb], v_ref.at[bb], halo.at[bb], hstate.at[bb], ts=ts)


def _odd_tile(x_ref, nw_ref, win_ref, convw_ref, convb_ref, dtb_ref, alog_ref, dskip_ref,
              snorm_ref, qn_ref, kn_ref, seg_ref,
              c_ref, q_ref, k_ref, v_ref,
              halo, hstate, *, ts):
    h = _rmsnorm(x_ref[...], nw_ref[...]).astype(BF16)
    proj = lambda c0, n: _dot(h, win_ref[:, c0:c0 + n])

    seg = seg_ref[...]
    for src, nref, dst, scale in ((ODD_Q, qn_ref, q_ref, SB_HEAD_DIM ** -0.5), (ODD_K, kn_ref, k_ref, 1.0)):
        t = proj(src, SB_WIDTH)
        hi, lo = _split_bf16(t * t)
        ssq = _dot(hi, seg) + _dot(lo, seg)
        dst[...] = (t * lax.rsqrt(ssq * (1.0 / SB_HEAD_DIM) + EPS) * (nref[...] * scale)).astype(dst.dtype)
    v_ref[...] = proj(ODD_V, SB_WIDTH).astype(v_ref.dtype)

    raw = proj(ODD_XBC, SSD_XBC)
    conv = _causal_conv(halo, raw, convw_ref, convb_ref)
    xbc = _silu_tanh(conv)
    xs_all = xbc[:, :SSD_WIDTH]
    bm_all = xbc[:, SSD_WIDTH:SSD_WIDTH + SSD_GROUPS * SSD_STATE].astype(BF16)
    cm_all = xbc[:, SSD_WIDTH + SSD_GROUPS * SSD_STATE:].astype(BF16)

    dt_all = _softplus(proj(ODD_DT, LANES) + dtb_ref[...])
    a_all = dt_all * (-jnp.exp(alog_ref[...]))
    zg = proj(ODD_Z, SSD_WIDTH)
    zg = _silu_tanh(zg)

    causal = _row_iota((CHUNK, CHUNK)) >= _lane_iota((CHUNK, CHUNK))
    lane = _lane_iota((CHUNK, LANES))
    heads_per_group = SSD_HEADS // SSD_GROUPS
    for c in range(ts // CHUNK):
        r0, r1 = c * CHUNK, (c + 1) * CHUNK
        xs = xs_all[r0:r1, :]
        a_cum = _chunk_scan(a_all[r0:r1, :], jnp.add, 0.0)
        a_last = a_cum[CHUNK - 1:CHUNK, :]
        a_t = a_cum.T
        xdt = xs * _expand_heads(dt_all[r0:r1, :], SSD_HEADS, SSD_HEAD_DIM)
        a_wide = _expand_heads(a_cum, SSD_HEADS, SSD_HEAD_DIM)
        a_last_wide = a_wide[CHUNK - 1:CHUNK, :]
        xw = (xdt * jnp.exp(a_last_wide - a_wide)).astype(BF16)
        xdt = xdt.astype(BF16)
        from_start = jnp.exp(a_wide)
        chunk_decay = jnp.exp(a_last_wide)

        for g in range(SSD_GROUPS):
            bg = bm_all[r0:r1, g * SSD_STATE:(g + 1) * SSD_STATE]
            cg = cm_all[r0:r1, g * SSD_STATE:(g + 1) * SSD_STATE]
            gs = slice(g * SSD_GROUP_WIDTH, (g + 1) * SSD_GROUP_WIDTH)
            cb = _dot_nt(cg, bg)
            hprev = hstate[g]
            y_off = _dot(cg, hprev.astype(BF16))
            bg_t = bg.astype(F32).T.astype(BF16)
            hstate[g] = chunk_decay[:, gs] * hprev + _dot(bg_t, xw[:, gs])
            masked = []
            for hh in range(g * heads_per_group, (g + 1) * heads_per_group):
                dec = jnp.exp(jnp.where(causal, a_cum[:, hh:hh + 1] - a_t[hh:hh + 1, :], -jnp.inf))
                masked.append((cb * dec).astype(BF16))
            diag = []
            for pair in range(heads_per_group // 2):
                ps = slice(g * SSD_GROUP_WIDTH + pair * LANES, g * SSD_GROUP_WIDTH + (pair + 1) * LANES)
                outs = [_dot(masked[2 * pair + j], xdt[:, ps]) for j in range(2)]
                diag.append(jnp.where(lane < SSD_HEAD_DIM, outs[0], outs[1]))
            y = jnp.concatenate(diag, axis=1) + y_off * from_start[:, gs] + dskip_ref[:, gs] * xs[:, gs]
            c_ref[r0:r1, gs] = _rmsnorm(y * zg[r0:r1, gs], snorm_ref[:, gs]).astype(c_ref.dtype)


def _odd_mixer(x, norm_w, w_in, conv_w, conv_b, dt_bias, a_log, d_skip, snorm, qn, kn, seg, *, ts=512, nb=1):
    bsz, s, d = x.shape
    const2 = lambda b, i: (0, 0)
    tile = lambda w: pl.BlockSpec((nb, ts, w), lambda b, i: (b, i, 0))
    return pl.pallas_call(
        functools.partial(_odd_kernel, ts=ts, nb=nb),
        grid=(bsz // nb, s // ts),
        in_specs=[tile(d)] + [pl.BlockSpec(a.shape, const2) for a in
                              (norm_w, w_in, conv_w, conv_b, dt_bias, a_log, d_skip, snorm, qn, kn, seg)],
        out_specs=[tile(SSD_WIDTH), tile(SB_WIDTH), tile(SB_WIDTH), tile(SB_WIDTH)],
        out_shape=[jax.ShapeDtypeStruct((bsz, s, SSD_WIDTH), BF16)]
        + [jax.ShapeDtypeStruct((bsz, s, SB_WIDTH), BF16)] * 3,
        scratch_shapes=[
            pltpu.VMEM((nb, CONV_HALO + ts, SSD_XBC), F32),
            pltpu.VMEM((nb, SSD_GROUPS, SSD_STATE, SSD_GROUP_WIDTH), F32),
        ],
        compiler_params=_cparams(("parallel", "arbitrary")),
        name="odd_mixer",
    )(x, norm_w, w_in, conv_w, conv_b, dt_bias, a_log, d_skip, snorm, qn, kn, seg)


def _sb_kernel(q_ref, k_ref, v_ref, tri_ref, o_ref, acc_ref, r_ref, rmin_ref):
    step = pl.program_id(1)
    blk = SB_BLOCK
    pairs = SB_WIDTH // LANES
    first = _lane_iota((blk, LANES)) < SB_HEAD_DIM
    zero = jnp.zeros((blk, LANES), q_ref.dtype)
    q_stack = []
    for t in range(SB_TILES):
        q = q_ref[t * blk:(t + 1) * blk, :]
        for p in range(pairs):
            qp = q[:, p * LANES:(p + 1) * LANES]
            q_stack.append(jnp.concatenate([jnp.where(first, qp, zero), jnp.where(first, zero, qp)], axis=0))
    tri = tri_ref[...]
    strict = jnp.concatenate([_row_iota((blk, blk)) > _lane_iota((blk, blk))] * 2, axis=0)

    def pair_pipeline(t, p, j_lo, nblk, diag, out):
        k0 = pl.multiple_of(j_lo * blk, blk)
        cols = slice(p * LANES, (p + 1) * LANES)
        r = None if diag else r_ref[t, 2 * p * blk:2 * (p + 1) * blk, :]
        z = _dot_nt(q_stack[t * pairs + p], k_ref[pl.ds(k0, nblk * blk), cols])
        yield
        sp = _softplus(z)
        sp_near_first = [sp[:, b * blk:(b + 1) * blk] for b in reversed(range(nblk))]
        if diag:
            sp_near_first[0] = jnp.where(strict, sp_near_first[0], 0.0)
        hi, lo = _split_bf16(jnp.concatenate(sp_near_first, axis=0))
        yield
        cs = _dot(jnp.concatenate([hi, lo], axis=1), tri)
        yield
        ws = []
        for i in range(nblk):
            b = nblk - 1 - i
            cs_b = cs[2 * i * blk:2 * (i + 1) * blk]
            arg = z[:, b * blk:(b + 1) * blk] - cs_b[:, :blk]
            w_b = jnp.exp(arg if r is None else arg - r)
            if diag and i == 0:
                w_b = jnp.where(strict, w_b, 0.0)
            ws.append(w_b.astype(BF16))
            r = cs_b[:, blk:] if r is None else r + cs_b[:, blk:]
        w = jnp.concatenate(ws[::-1], axis=1)
        yield
        pv = _dot(w, v_ref[pl.ds(k0, nblk * blk), cols])
        out(t, p, jnp.where(first, pv[:blk], pv[blk:]), r)

    def process(jobs, diag):
        rmins = {t: [] for t, _, _ in jobs}

        def out(t, p, acc, r):
            cols = slice(p * LANES, (p + 1) * LANES)
            if diag:
                acc_ref[t, :, cols] = acc
            else:
                acc_ref[t, :, cols] += acc
            r_ref[t, 2 * p * blk:2 * (p + 1) * blk, :] = r
            rmins[t].append(jnp.min(r))

        _staggered([pair_pipeline(t, p, j_lo, nblk, diag, out) for t, j_lo, nblk in jobs for p in range(pairs)])
        return {t: functools.reduce(jnp.minimum, v) for t, v in rmins.items()}

    def near(nblk_of):
        jobs = [(t, step * SB_TILES + t - (nblk_of(t) - 1), nblk_of(t)) for t in range(SB_TILES)]
        for t, m in process(jobs, True).items():
            rmin_ref[t] = m

    head_steps = -(-(SB_NEAR - 1) // SB_TILES)

    @pl.when(step >= head_steps)
    def _():
        near(lambda t: SB_NEAR)

    for s0 in range(head_steps):
        @pl.when(step == s0)
        def _():
            near(lambda t: min(SB_NEAR, s0 * SB_TILES + t + 1))

    for t in range(SB_TILES):
        qi = step * SB_TILES + t

        def cond(c):
            return jnp.logical_and(c[0] >= 0, c[1] < SB_EXIT)

        def body(c, t=t):
            return c[0] - 1, process([(t, c[0], 1)], False)[t]

        lax.while_loop(cond, body, (qi - jnp.minimum(qi + 1, SB_NEAR), rmin_ref[t]))
        o_ref[t * blk:(t + 1) * blk, :] = acc_ref[t].astype(o_ref.dtype)


def _stickbreak(q, k, v, tri):
    bsz, s, w = q.shape
    rows = SB_TILES * SB_BLOCK
    return pl.pallas_call(
        _sb_kernel,
        grid=(bsz, s // rows),
        in_specs=[
            pl.BlockSpec((None, rows, w), lambda b, i: (b, i, 0)),
            pl.BlockSpec((None, s, w), lambda b, i: (b, 0, 0)),
            pl.BlockSpec((None, s, w), lambda b, i: (b, 0, 0)),
            pl.BlockSpec(tri.shape, lambda b, i: (0, 0)),
        ],
        out_specs=pl.BlockSpec((None, rows, w), lambda b, i: (b, i, 0)),
        out_shape=jax.ShapeDtypeStruct((bsz, s, w), BF16),
        scratch_shapes=[pltpu.VMEM((SB_TILES, SB_BLOCK, w), F32),
                        pltpu.VMEM((SB_TILES, 2 * (w // LANES) * SB_BLOCK, SB_BLOCK), F32),
                        pltpu.SMEM((SB_TILES,), F32)],
        compiler_params=_cparams(("parallel", "arbitrary")),
        name="stickbreak",
    )(q, k, v, tri)


def _pad_cols(w, n):
    return jnp.pad(w, ((0, 0), (0, n - w.shape[1])))


def kernel(x, l0_ffn1_norm, l0_ffn1_wg, l0_ffn1_wu, l0_ffn1_wd, l0_mix_norm, l0_w_in, l0_pool_w, l0_pool_scale, l0_qk_conv_w, l0_qk_conv_b, l0_gate_bias, l0_mlstm_norm, l0_w_out, l0_ffn2_norm, l0_ffn2_wg, l0_ffn2_wu, l0_ffn2_wd, l1_ffn1_norm, l1_ffn1_wg, l1_ffn1_wu, l1_ffn1_wd, l1_mix_norm, l1_w_in, l1_ssd_conv_w, l1_ssd_conv_b, l1_ssd_dt_bias, l1_ssd_A_log, l1_ssd_D, l1_ssd_norm, l1_sb_q_norm, l1_sb_k_norm, l1_w_out, l1_ffn2_norm, l1_ffn2_wg, l1_ffn2_wu, l1_ffn2_wd):
    bsz, s, d = x.shape
    t = bsz * s
    row = lambda a: a.reshape(1, -1).astype(F32)
    bf = lambda a: a.astype(BF16)

    def ffn(xf, mixes, norm_w, wg, wu, wd):
        return _ffn(xf, mixes, row(norm_w), bf(wg), bf(wu), bf(wd))

    xf = ffn(x.reshape(t, d), [], l0_ffn1_norm, l0_ffn1_wg, l0_ffn1_wu, l0_ffn1_wd)
    n_main = EVEN_GI
    w_in0 = jnp.concatenate([
        l0_w_in[:, :n_main],
        _pad_cols(l0_w_in[:, n_main:n_main + MLSTM_HEADS], LANES),
        _pad_cols(l0_w_in[:, n_main + MLSTM_HEADS:], LANES)], axis=1)
    gbias = jnp.stack([jnp.pad(l0_gate_bias[:MLSTM_HEADS], (0, LANES - MLSTM_HEADS)),
                       jnp.pad(l0_gate_bias[MLSTM_HEADS:], (0, LANES - MLSTM_HEADS))]).astype(F32)
    mix0 = _even_mixer(xf.reshape(bsz, s, d), row(l0_mix_norm), bf(w_in0), bf(l0_pool_w),
                       row(l0_pool_scale), l0_qk_conv_w.astype(F32), row(l0_qk_conv_b), gbias,
                       row(l0_mlstm_norm))
    xf = ffn(xf, [(mix0.reshape(t, -1), bf(l0_w_out))], l0_ffn2_norm, l0_ffn2_wg, l0_ffn2_wu, l0_ffn2_wd)

    xf = ffn(xf, [], l1_ffn1_norm, l1_ffn1_wg, l1_ffn1_wu, l1_ffn1_wd)
    c_dt = SSD_WIDTH + SSD_XBC
    c_q = c_dt + SSD_HEADS
    w_in1 = jnp.concatenate([l1_w_in[:, :c_dt], l1_w_in[:, c_q:],
                             _pad_cols(l1_w_in[:, c_dt:c_q], LANES)], axis=1)
    pad_heads = lambda a: jnp.pad(a.astype(F32), (0, LANES - SSD_HEADS)).reshape(1, LANES)
    seg_id = jnp.arange(SB_WIDTH) // SB_HEAD_DIM
    seg = (seg_id[:, None] == seg_id[None, :]).astype(BF16)
    c_out, qn, kn, vv = _odd_mixer(
        xf.reshape(bsz, s, d), row(l1_mix_norm), bf(w_in1), l1_ssd_conv_w.astype(F32), row(l1_ssd_conv_b),
        pad_heads(l1_ssd_dt_bias), pad_heads(l1_ssd_A_log), row(jnp.repeat(l1_ssd_D, SSD_HEAD_DIM)),
        row(l1_ssd_norm), row(jnp.tile(l1_sb_q_norm, SB_HEADS)), row(jnp.tile(l1_sb_k_norm, SB_HEADS)), seg)
    idx = jnp.arange(SB_BLOCK)
    tri = jnp.concatenate([(idx[:, None] >= idx[None, :]).astype(BF16),
                           jnp.ones((SB_BLOCK, SB_BLOCK), BF16)], axis=1)
    tri = jnp.concatenate([tri, tri], axis=0)
    d_out = _stickbreak(qn, kn, vv, tri)
    w_out1 = bf(l1_w_out)
    xf = ffn(xf, [(c_out.reshape(t, -1), w_out1[:SSD_WIDTH]), (d_out.reshape(t, -1), w_out1[SSD_WIDTH:])],
             l1_ffn2_norm, l1_ffn2_wg, l1_ffn2_wu, l1_ffn2_wd)
    return xf.reshape(bsz, s, d)
```

```python
import functools
import math

import jax
import jax.numpy as jnp
from jax import lax
from jax.experimental import pallas as pl
from jax.experimental.pallas import tpu as pltpu

F32 = jnp.float32
BF16 = jnp.bfloat16

EPS = 1e-6
FFN_RES = 0.5
SHORT_CONV = 4
LANES = 128

POOL_WINDOWS = (2, 4, 8, 16)
POOL_WIDTH = 512
MLSTM_HEADS = 4
MLSTM_HEAD_DIM = 128
MLSTM_WIDTH = MLSTM_HEADS * MLSTM_HEAD_DIM
CHUNK = 128

SSD_HEADS = 16
SSD_HEAD_DIM = 64
SSD_WIDTH = SSD_HEADS * SSD_HEAD_DIM
SSD_GROUPS = 4
SSD_STATE = 128
SSD_GROUP_WIDTH = SSD_WIDTH // SSD_GROUPS
SB_HEADS = 8
SB_HEAD_DIM = 64
SB_WIDTH = SB_HEADS * SB_HEAD_DIM
SB_BLOCK = 128
SB_EXIT = 104.0
SB_TILES = 4
SB_NEAR = 3

VMEM_LIMIT_BYTES = 56 * 1024 * 1024

FFN_TM = 512
FFN_CHUNK = 512
MIXER_TS = 512


def _cparams(semantics):
    return pltpu.CompilerParams(dimension_semantics=semantics, vmem_limit_bytes=VMEM_LIMIT_BYTES)


def _rmsnorm(x, w):
    return x * lax.rsqrt(jnp.mean(x * x, axis=-1, keepdims=True) + EPS) * w


def _sigmoid(x):
    return 1.0 / (1.0 + jnp.exp(-x))


def _sigmoid_tanh(x):
    return 0.5 + 0.5 * jnp.tanh(0.5 * x)


def _silu_tanh(x):
    h = 0.5 * x
    return h + h * jnp.tanh(h)


def _softplus(x):
    return jnp.maximum(x, 0.0) + jnp.log(1.0 + jnp.exp(-jnp.abs(x)))


def _dot(a, b):
    return jnp.dot(a, b, preferred_element_type=F32)


def _dot_nt(a, b):
    return lax.dot_general(a, b, (((1,), (1,)), ((), ())), preferred_element_type=F32)


def _split_bf16(x):
    hi = x.astype(BF16)
    lo = (x - hi.astype(F32)).astype(BF16)
    return hi, lo


def _row_iota(shape):
    return lax.broadcasted_iota(jnp.int32, shape, 0)


def _lane_iota(shape):
    return lax.broadcasted_iota(jnp.int32, shape, 1)


def _chunk_scan(x, op, fill):
    rows = _row_iota(x.shape)
    sh = 1
    while sh < x.shape[0]:
        shifted = jnp.where(rows >= sh, pltpu.roll(x, sh, 0), fill)
        x = op(x, shifted)
        sh *= 2
    return x


def _causal_conv(buf_ref, raw, w_ref, b_ref):
    ts = raw.shape[0]
    buf_ref[CONV_HALO:, :] = raw
    acc = raw * w_ref[SHORT_CONV - 1:SHORT_CONV, :] + b_ref[...]
    for k in range(SHORT_CONV - 1):
        off = CONV_HALO - (SHORT_CONV - 1 - k)
        acc = acc + buf_ref[off:off + ts, :] * w_ref[k:k + 1, :]
    buf_ref[0:CONV_HALO, :] = raw[ts - CONV_HALO:, :]
    return acc


def _staggered(gens):
    pending, live = list(gens), []
    while pending or live:
        if pending:
            live.append(pending.pop(0))
        for g in list(live):
            try:
                next(g)
            except StopIteration:
                live.remove(g)


def _expand_heads(slab, n_heads, head_dim):
    per = LANES // head_dim
    m = slab.shape[0]
    lane = _lane_iota((m, LANES))
    pieces = []
    for p in range(n_heads // per):
        piece = jnp.broadcast_to(slab[:, p * per:p * per + 1], (m, LANES))
        for j in range(1, per):
            piece = jnp.where(lane >= j * head_dim, slab[:, p * per + j:p * per + j + 1], piece)
        pieces.append(piece)
    return jnp.concatenate(pieces, axis=1)


def _ffn_kernel(*refs, n_mix, ff_chunk):
    x_ref = refs[0]
    mix_refs = refs[1:1 + 2 * n_mix]
    nw_ref, wg_ref, wu_ref, wd_ref, o_ref = refs[1 + 2 * n_mix:]
    x = x_ref[...]
    for i in range(n_mix):
        x = x + _dot(mix_refs[2 * i][...], mix_refs[2 * i + 1][...])
    h = _rmsnorm(x, nw_ref[...]).astype(BF16)
    d_ff = wg_ref.shape[1]
    y = None
    for c0 in range(0, d_ff, ff_chunk):
        c1 = min(c0 + ff_chunk, d_ff)
        g = _dot(h, wg_ref[:, c0:c1])
        u = _dot(h, wu_ref[:, c0:c1])
        a = (g * _sigmoid(g) * u).astype(BF16)
        part = _dot(a, wd_ref[c0:c1, :])
        y = part if y is None else y + part
    o_ref[...] = x + FFN_RES * y


def _ffn(x, mixes, norm_w, wg, wu, wd, *, tm=FFN_TM, ff_chunk=FFN_CHUNK):
    t, d = x.shape
    d_ff = wg.shape[1]
    const = lambda i: (0, 0)
    in_specs = [pl.BlockSpec((tm, d), lambda i: (i, 0))]
    args = [x]
    for mix, w_out in mixes:
        in_specs += [pl.BlockSpec((tm, mix.shape[1]), lambda i: (i, 0)),
                     pl.BlockSpec(w_out.shape, const)]
        args += [mix, w_out]
    in_specs += [pl.BlockSpec((1, d), const), pl.BlockSpec((d, d_ff), const),
                 pl.BlockSpec((d, d_ff), const), pl.BlockSpec((d_ff, d), const)]
    args += [norm_w, wg, wu, wd]
    return pl.pallas_call(
        functools.partial(_ffn_kernel, n_mix=len(mixes), ff_chunk=ff_chunk),
        grid=(t // tm,),
        in_specs=in_specs,
        out_specs=pl.BlockSpec((tm, d), lambda i: (i, 0)),
        out_shape=jax.ShapeDtypeStruct((t, d), F32),
        compiler_params=_cparams(("parallel",)),
        name="ffn",
    )(*args)


EVEN_U = 0
EVEN_QK = EVEN_U + POOL_WIDTH
EVEN_V = EVEN_QK + 2 * MLSTM_WIDTH
EVEN_O = EVEN_V + MLSTM_WIDTH
EVEN_GI = EVEN_O + MLSTM_WIDTH
EVEN_GF = EVEN_GI + LANES
EVEN_COLS = EVEN_GF + LANES
POOL_HALO = 16
CONV_HALO = 8


def _even_kernel(x_ref, nw_ref, win_ref, poolw_ref, pscale_ref, convw_ref, convb_ref,
                 gbias_ref, mnorm_ref, o_ref,
                 halo_u, halo_qk, state, m_state, *, ts):
    s_idx = pl.program_id(1)

    @pl.when(s_idx == 0)
    def _():
        halo_u[...] = jnp.zeros_like(halo_u)
        halo_qk[0:CONV_HALO, :] = jnp.zeros((CONV_HALO, halo_qk.shape[1]), F32)
        state[...] = jnp.zeros_like(state)
        m_state[...] = jnp.zeros_like(m_state)

    h = _rmsnorm(x_ref[...], nw_ref[...]).astype(BF16)
    full_proj = _dot(h, win_ref[...])
    proj = lambda c0, n: full_proj[:, c0:c0 + n]

    u = proj(EVEN_U, POOL_WIDTH)
    ue = jnp.concatenate([halo_u[...], u], axis=0)
    halo_u[...] = u[ts - POOL_HALO:, :]
    pos = (s_idx * ts + 1 + _row_iota((ts, 1))).astype(F32)
    for g, win in enumerate(POOL_WINDOWS):
        acc = ue[:, g * LANES:(g + 1) * LANES]
        sh = 1
        while sh < win:
            acc = acc + pltpu.roll(acc, sh, 0)
            sh *= 2
        win_sum = acc[POOL_HALO:, :]
        pooled = win_sum / jnp.minimum(pos, float(win)) - u[:, g * LANES:(g + 1) * LANES]
        mixed = _dot(pooled.astype(BF16), poolw_ref[g])
        o_ref[:, g * LANES:(g + 1) * LANES] = (
            mixed * pscale_ref[:, g * LANES:(g + 1) * LANES]).astype(o_ref.dtype)

    qk_raw = proj(EVEN_QK, 2 * MLSTM_WIDTH)
    conv = _causal_conv(halo_qk, qk_raw, convw_ref, convb_ref)
    qk = _silu_tanh(conv)
    q_all = qk[:, :MLSTM_WIDTH].astype(BF16)
    k_all = qk[:, MLSTM_WIDTH:] * (MLSTM_HEAD_DIM ** -0.5)
    v_all = proj(EVEN_V, MLSTM_WIDTH).astype(BF16)
    o_gate = _sigmoid_tanh(proj(EVEN_O, MLSTM_WIDTH))

    g_i = proj(EVEN_GI, LANES) + gbias_ref[0:1, :]
    g_f = -_softplus(-(proj(EVEN_GF, LANES) + gbias_ref[1:2, :]))

    ones_blk = jnp.ones((CHUNK, MLSTM_HEAD_DIM), BF16)
    causal = _row_iota((CHUNK, CHUNK)) >= _lane_iota((CHUNK, CHUNK))
    for c in range(ts // CHUNK):
        r0, r1 = c * CHUNK, (c + 1) * CHUNK
        li = g_i[r0:r1, :]
        b = _chunk_scan(g_f[r0:r1, :], jnp.add, 0.0)
        a = li - b
        m_prev = m_state[...]
        big_m = jnp.maximum(m_prev, _chunk_scan(a, jnp.maximum, -jnp.inf))
        b_last = b[CHUNK - 1:CHUNK, :]
        m_new = b_last + big_m[CHUNK - 1:CHUNK, :]
        w_inter = jnp.exp(m_prev - big_m)
        e_negm = jnp.exp(-(b + big_m))
        w_state = jnp.exp(b_last + a - m_new)
        decay = jnp.exp(b_last + m_prev - m_new)
        m_state[...] = m_new
        a_t = a.T
        def head_pipeline(hd):
            c0, c1 = hd * MLSTM_HEAD_DIM, (hd + 1) * MLSTM_HEAD_DIM
            qc = q_all[r0:r1, c0:c1]
            kc = k_all[r0:r1, c0:c1]
            v_ext = jnp.concatenate([v_all[r0:r1, c0:c1], ones_blk], axis=1)
            st = state[hd]
            s_qk = _dot_nt(qc, kc.astype(BF16))
            inter = _dot(qc, st.astype(BF16))
            yield
            w_intra = jnp.exp(jnp.where(causal, a_t[hd:hd + 1, :] - big_m[:, hd:hd + 1], -jnp.inf))
            p = (s_qk * w_intra).astype(BF16)
            kw_t = (kc * w_state[:, hd:hd + 1]).T.astype(BF16)
            yield
            pv = _dot(p, v_ext)
            state[hd] = decay[:, hd:hd + 1] * st + _dot(kw_t, v_ext)
            yield
            numden = w_inter[:, hd:hd + 1] * inter + pv
            den = jnp.maximum(jnp.abs(numden[:, MLSTM_HEAD_DIM:MLSTM_HEAD_DIM + 1]),
                              e_negm[:, hd:hd + 1])
            hh = numden[:, :MLSTM_HEAD_DIM] / den
            hn = _rmsnorm(hh, mnorm_ref[:, c0:c1])
            o_ref[r0:r1, POOL_WIDTH + c0:POOL_WIDTH + c1] = (
                o_gate[r0:r1, c0:c1] * hn).astype(o_ref.dtype)

        _staggered([head_pipeline(hd) for hd in range(MLSTM_HEADS)])


def _even_mixer(x, norm_w, w_in, pool_w, pool_scale, conv_w, conv_b, gate_bias, mnorm, *, ts=MIXER_TS):
    bsz, s, d = x.shape
    const2 = lambda b, i: (0, 0)
    return pl.pallas_call(
        functools.partial(_even_kernel, ts=ts),
        grid=(bsz, s // ts),
        in_specs=[
            pl.BlockSpec((None, ts, d), lambda b, i: (b, i, 0)),
            pl.BlockSpec((1, d), const2),
            pl.BlockSpec(w_in.shape, const2),
            pl.BlockSpec(pool_w.shape, lambda b, i: (0, 0, 0)),
            pl.BlockSpec(pool_scale.shape, const2),
            pl.BlockSpec(conv_w.shape, const2),
            pl.BlockSpec(conv_b.shape, const2),
            pl.BlockSpec(gate_bias.shape, const2),
            pl.BlockSpec(mnorm.shape, const2),
        ],
        out_specs=pl.BlockSpec((None, ts, POOL_WIDTH + MLSTM_WIDTH), lambda b, i: (b, i, 0)),
        out_shape=jax.ShapeDtypeStruct((bsz, s, POOL_WIDTH + MLSTM_WIDTH), BF16),
        scratch_shapes=[
            pltpu.VMEM((POOL_HALO, POOL_WIDTH), F32),
            pltpu.VMEM((CONV_HALO + ts, 2 * MLSTM_WIDTH), F32),
            pltpu.VMEM((MLSTM_HEADS, MLSTM_HEAD_DIM, 2 * MLSTM_HEAD_DIM), F32),
            pltpu.VMEM((1, LANES), F32),
        ],
        compiler_params=_cparams(("parallel", "arbitrary")),
        name="even_mixer",
    )(x, norm_w, w_in, pool_w, pool_scale, conv_w, conv_b, gate_bias, mnorm)


ODD_Z = 0
ODD_XBC = ODD_Z + SSD_WIDTH
ODD_XS = ODD_XBC
ODD_B = ODD_XS + SSD_WIDTH
ODD_C = ODD_B + SSD_GROUPS * SSD_STATE
ODD_Q = ODD_C + SSD_GROUPS * SSD_STATE
ODD_K = ODD_Q + SB_WIDTH
ODD_V = ODD_K + SB_WIDTH
ODD_DT = ODD_V + SB_WIDTH
ODD_COLS = ODD_DT + LANES
SSD_XBC = SSD_WIDTH + 2 * SSD_GROUPS * SSD_STATE


def _odd_kernel(x_ref, nw_ref, win_ref, convw_ref, convb_ref, dtb_ref, alog_ref, dskip_ref,
                snorm_ref, qn_ref, kn_ref, seg_ref,
                c_ref, q_ref, k_ref, v_ref,
                halo, hstate, *, ts):
    @pl.when(pl.program_id(1) == 0)
    def _():
        halo[0:CONV_HALO, :] = jnp.zeros((CONV_HALO, halo.shape[1]), F32)
        hstate[...] = jnp.zeros_like(hstate)

    h = _rmsnorm(x_ref[...], nw_ref[...]).astype(BF16)
    proj = lambda c0, n: _dot(h, win_ref[:, c0:c0 + n])

    seg = seg_ref[...]
    for src, nref, dst, scale in ((ODD_Q, qn_ref, q_ref, SB_HEAD_DIM ** -0.5), (ODD_K, kn_ref, k_ref, 1.0)):
        t = proj(src, SB_WIDTH)
        hi, lo = _split_bf16(t * t)
        ssq = _dot(hi, seg) + _dot(lo, seg)
        dst[...] = (t * lax.rsqrt(ssq * (1.0 / SB_HEAD_DIM) + EPS) * (nref[...] * scale)).astype(dst.dtype)
    v_ref[...] = proj(ODD_V, SB_WIDTH).astype(v_ref.dtype)

    raw = proj(ODD_XBC, SSD_XBC)
    conv = _causal_conv(halo, raw, convw_ref, convb_ref)
    xbc = _silu_tanh(conv)
    xs_all = xbc[:, :SSD_WIDTH]
    bm_all = xbc[:, SSD_WIDTH:SSD_WIDTH + SSD_GROUPS * SSD_STATE].astype(BF16)
    cm_all = xbc[:, SSD_WIDTH + SSD_GROUPS * SSD_STATE:].astype(BF16)

    dt_all = _softplus(proj(ODD_DT, LANES) + dtb_ref[...])
    a_all = dt_all * (-jnp.exp(alog_ref[...]))
    zg = proj(ODD_Z, SSD_WIDTH)
    zg = _silu_tanh(zg)

    causal = _row_iota((CHUNK, CHUNK)) >= _lane_iota((CHUNK, CHUNK))
    lane = _lane_iota((CHUNK, LANES))
    heads_per_group = SSD_HEADS // SSD_GROUPS
    for c in range(ts // CHUNK):
        r0, r1 = c * CHUNK, (c + 1) * CHUNK
        xs = xs_all[r0:r1, :]
        a_cum = _chunk_scan(a_all[r0:r1, :], jnp.add, 0.0)
        a_last = a_cum[CHUNK - 1:CHUNK, :]
        a_t = a_cum.T
        xdt = xs * _expand_heads(dt_all[r0:r1, :], SSD_HEADS, SSD_HEAD_DIM)
        a_wide = _expand_heads(a_cum, SSD_HEADS, SSD_HEAD_DIM)
        a_last_wide = a_wide[CHUNK - 1:CHUNK, :]
        xw = (xdt * jnp.exp(a_last_wide - a_wide)).astype(BF16)
        xdt = xdt.astype(BF16)
        from_start = jnp.exp(a_wide)
        chunk_decay = jnp.exp(a_last_wide)

        for g in range(SSD_GROUPS):
            bg = bm_all[r0:r1, g * SSD_STATE:(g + 1) * SSD_STATE]
            cg = cm_all[r0:r1, g * SSD_STATE:(g + 1) * SSD_STATE]
            gs = slice(g * SSD_GROUP_WIDTH, (g + 1) * SSD_GROUP_WIDTH)
            cb = _dot_nt(cg, bg)
            hprev = hstate[g]
            y_off = _dot(cg, hprev.astype(BF16))
            bg_t = bg.astype(F32).T.astype(BF16)
            hstate[g] = chunk_decay[:, gs] * hprev + _dot(bg_t, xw[:, gs])
            masked = []
            for hh in range(g * heads_per_group, (g + 1) * heads_per_group):
                dec = jnp.exp(jnp.where(causal, a_cum[:, hh:hh + 1] - a_t[hh:hh + 1, :], -jnp.inf))
                masked.append((cb * dec).astype(BF16))
            diag = []
            for pair in range(heads_per_group // 2):
                ps = slice(g * SSD_GROUP_WIDTH + pair * LANES, g * SSD_GROUP_WIDTH + (pair + 1) * LANES)
                outs = [_dot(masked[2 * pair + j], xdt[:, ps]) for j in range(2)]
                diag.append(jnp.where(lane < SSD_HEAD_DIM, outs[0], outs[1]))
            y = jnp.concatenate(diag, axis=1) + y_off * from_start[:, gs] + dskip_ref[:, gs] * xs[:, gs]
            c_ref[r0:r1, gs] = _rmsnorm(y * zg[r0:r1, gs], snorm_ref[:, gs]).astype(c_ref.dtype)


def _odd_mixer(x, norm_w, w_in, conv_w, conv_b, dt_bias, a_log, d_skip, snorm, qn, kn, seg, *, ts=MIXER_TS):
    bsz, s, d = x.shape
    const2 = lambda b, i: (0, 0)
    tile = lambda w: pl.BlockSpec((None, ts, w), lambda b, i: (b, i, 0))
    return pl.pallas_call(
        functools.partial(_odd_kernel, ts=ts),
        grid=(bsz, s // ts),
        in_specs=[tile(d)] + [pl.BlockSpec(a.shape, const2) for a in
                              (norm_w, w_in, conv_w, conv_b, dt_bias, a_log, d_skip, snorm, qn, kn, seg)],
        out_specs=[tile(SSD_WIDTH), tile(SB_WIDTH), tile(SB_WIDTH), tile(SB_WIDTH)],
        out_shape=[jax.ShapeDtypeStruct((bsz, s, SSD_WIDTH), BF16)]
        + [jax.ShapeDtypeStruct((bsz, s, SB_WIDTH), BF16)] * 3,
        scratch_shapes=[
            pltpu.VMEM((CONV_HALO + ts, SSD_XBC), F32),
            pltpu.VMEM((SSD_GROUPS, SSD_STATE, SSD_GROUP_WIDTH), F32),
        ],
        compiler_params=_cparams(("parallel", "arbitrary")),
        name="odd_mixer",
    )(x, norm_w, w_in, conv_w, conv_b, dt_bias, a_log, d_skip, snorm, qn, kn, seg)


def _sb_kernel(q_ref, k_ref, v_ref, tri_ref, o_ref, acc_ref, r_ref, rmin_ref):
    step = pl.program_id(1)
    blk = SB_BLOCK
    pairs = SB_WIDTH // LANES
    first = _lane_iota((blk, LANES)) < SB_HEAD_DIM
    zero = jnp.zeros((blk, LANES), q_ref.dtype)
    q_stack = []
    for t in range(SB_TILES):
        q = q_ref[t * blk:(t + 1) * blk, :]
        for p in range(pairs):
            qp = q[:, p * LANES:(p + 1) * LANES]
            q_stack.append(jnp.concatenate([jnp.where(first, qp, zero), jnp.where(first, zero, qp)], axis=0))
    tri = tri_ref[...]
    strict = jnp.concatenate([_row_iota((blk, blk)) > _lane_iota((blk, blk))] * 2, axis=0)

    def pair_pipeline(t, p, j_lo, nblk, diag, out):
        k0 = pl.multiple_of(j_lo * blk, blk)
        cols = slice(p * LANES, (p + 1) * LANES)
        r = None if diag else r_ref[t, 2 * p * blk:2 * (p + 1) * blk, :]
        z = _dot_nt(q_stack[t * pairs + p], k_ref[pl.ds(k0, nblk * blk), cols])
        yield
        sp = _softplus(z)
        sp_near_first = [sp[:, b * blk:(b + 1) * blk] for b in reversed(range(nblk))]
        if diag:
            sp_near_first[0] = jnp.where(strict, sp_near_first[0], 0.0)
        hi, lo = _split_bf16(jnp.concatenate(sp_near_first, axis=0))
        yield
        cs = _dot(jnp.concatenate([hi, lo], axis=1), tri)
        yield
        ws = []
        for i in range(nblk):
            b = nblk - 1 - i
            cs_b = cs[2 * i * blk:2 * (i + 1) * blk]
            arg = z[:, b * blk:(b + 1) * blk] - cs_b[:, :blk]
            w_b = jnp.exp(arg if r is None else arg - r)
            if diag and i == 0:
                w_b = jnp.where(strict, w_b, 0.0)
            ws.append(w_b.astype(BF16))
            r = cs_b[:, blk:] if r is None else r + cs_b[:, blk:]
        w = jnp.concatenate(ws[::-1], axis=1)
        yield
        pv = _dot(w, v_ref[pl.ds(k0, nblk * blk), cols])
        out(t, p, jnp.where(first, pv[:blk], pv[blk:]), r)

    def process(jobs, diag):
        rmins = {t: [] for t, _, _ in jobs}

        def out(t, p, acc, r):
            cols = slice(p * LANES, (p + 1) * LANES)
            if diag:
                acc_ref[t, :, cols] = acc
            else:
                acc_ref[t, :, cols] += acc
            r_ref[t, 2 * p * blk:2 * (p + 1) * blk, :] = r
            rmins[t].append(jnp.min(r))

        _staggered([pair_pipeline(t, p, j_lo, nblk, diag, out) for t, j_lo, nblk in jobs for p in range(pairs)])
        return {t: functools.reduce(jnp.minimum, v) for t, v in rmins.items()}

    def near(nblk_of):
        jobs = [(t, step * SB_TILES + t - (nblk_of(t) - 1), nblk_of(t)) for t in range(SB_TILES)]
        for t, m in process(jobs, True).items():
            rmin_ref[t] = m

    head_steps = -(-(SB_NEAR - 1) // SB_TILES)

    @pl.when(step >= head_steps)
    def _():
        near(lambda t: SB_NEAR)

    for s0 in range(head_steps):
        @pl.when(step == s0)
        def _():
            near(lambda t: min(SB_NEAR, s0 * SB_TILES + t + 1))

    for t in range(SB_TILES):
        qi = step * SB_TILES + t

        def cond(c):
            return jnp.logical_and(c[0] >= 0, c[1] < SB_EXIT)

        def body(c, t=t):
            return c[0] - 1, process([(t, c[0], 1)], False)[t]

        lax.while_loop(cond, body, (qi - jnp.minimum(qi + 1, SB_NEAR), rmin_ref[t]))
        o_ref[t * blk:(t + 1) * blk, :] = acc_ref[t].astype(o_ref.dtype)


def _stickbreak(q, k, v, tri):
    bsz, s, w = q.shape
    rows = SB_TILES * SB_BLOCK
    return pl.pallas_call(
        _sb_kernel,
        grid=(bsz, s // rows),
        in_specs=[
            pl.BlockSpec((None, rows, w), lambda b, i: (b, i, 0)),
            pl.BlockSpec((None, s, w), lambda b, i: (b, 0, 0)),
            pl.BlockSpec((None, s, w), lambda b, i: (b, 0, 0)),
            pl.BlockSpec(tri.shape, lambda b, i: (0, 0)),
        ],
        out_specs=pl.BlockSpec((None, rows, w), lambda b, i: (b, i, 0)),
        out_shape=jax.ShapeDtypeStruct((bsz, s, w), BF16),
        scratch_shapes=[pltpu.VMEM((SB_TILES, SB_BLOCK, w), F32),
                        pltpu.VMEM((SB_TILES, 2 * (w // LANES) * SB_BLOCK, SB_BLOCK), F32),
                        pltpu.SMEM((SB_TILES,), F32)],
        compiler_params=_cparams(("parallel", "arbitrary")),
        name="stickbreak",
    )(q, k, v, tri)


def _pad_cols(w, n):
    return jnp.pad(w, ((0, 0), (0, n - w.shape[1])))


def kernel(x, l0_ffn1_norm, l0_ffn1_wg, l0_ffn1_wu, l0_ffn1_wd, l0_mix_norm, l0_w_in, l0_pool_w, l0_pool_scale, l0_qk_conv_w, l0_qk_conv_b, l0_gate_bias, l0_mlstm_norm, l0_w_out, l0_ffn2_norm, l0_ffn2_wg, l0_ffn2_wu, l0_ffn2_wd, l1_ffn1_norm, l1_ffn1_wg, l1_ffn1_wu, l1_ffn1_wd, l1_mix_norm, l1_w_in, l1_ssd_conv_w, l1_ssd_conv_b, l1_ssd_dt_bias, l1_ssd_A_log, l1_ssd_D, l1_ssd_norm, l1_sb_q_norm, l1_sb_k_norm, l1_w_out, l1_ffn2_norm, l1_ffn2_wg, l1_ffn2_wu, l1_ffn2_wd):
    bsz, s, d = x.shape
    t = bsz * s
    row = lambda a: a.reshape(1, -1).astype(F32)
    bf = lambda a: a.astype(BF16)

    def ffn(xf, mixes, norm_w, wg, wu, wd):
        return _ffn(xf, mixes, row(norm_w), bf(wg), bf(wu), bf(wd))

    xf = ffn(x.reshape(t, d), [], l0_ffn1_norm, l0_ffn1_wg, l0_ffn1_wu, l0_ffn1_wd)
    n_main = EVEN_GI
    w_in0 = jnp.concatenate([
        bf(l0_w_in[:, :n_main]),
        _pad_cols(bf(l0_w_in[:, n_main:n_main + MLSTM_HEADS]), LANES),
        _pad_cols(bf(l0_w_in[:, n_main + MLSTM_HEADS:]), LANES)], axis=1)
    gbias = jnp.stack([jnp.pad(l0_gate_bias[:MLSTM_HEADS], (0, LANES - MLSTM_HEADS)),
                       jnp.pad(l0_gate_bias[MLSTM_HEADS:], (0, LANES - MLSTM_HEADS))]).astype(F32)
    mix0 = _even_mixer(xf.reshape(bsz, s, d), row(l0_mix_norm), w_in0, bf(l0_pool_w),
                       row(l0_pool_scale), l0_qk_conv_w.astype(F32), row(l0_qk_conv_b), gbias,
                       row(l0_mlstm_norm))
    xf = ffn(xf, [(mix0.reshape(t, -1), bf(l0_w_out))], l0_ffn2_norm, l0_ffn2_wg, l0_ffn2_wu, l0_ffn2_wd)

    xf = ffn(xf, [], l1_ffn1_norm, l1_ffn1_wg, l1_ffn1_wu, l1_ffn1_wd)
    c_dt = SSD_WIDTH + SSD_XBC
    c_q = c_dt + SSD_HEADS
    w_in1 = jnp.concatenate([bf(l1_w_in[:, :c_dt]), bf(l1_w_in[:, c_q:]),
                             _pad_cols(bf(l1_w_in[:, c_dt:c_q]), LANES)], axis=1)
    pad_heads = lambda a: jnp.pad(a.astype(F32), (0, LANES - SSD_HEADS)).reshape(1, LANES)
    seg_id = jnp.arange(SB_WIDTH) // SB_HEAD_DIM
    seg = (seg_id[:, None] == seg_id[None, :]).astype(BF16)
    c_out, qn, kn, vv = _odd_mixer(
        xf.reshape(bsz, s, d), row(l1_mix_norm), w_in1, l1_ssd_conv_w.astype(F32), row(l1_ssd_conv_b),
        pad_heads(l1_ssd_dt_bias), pad_heads(l1_ssd_A_log), row(jnp.repeat(l1_ssd_D, SSD_HEAD_DIM)),
        row(l1_ssd_norm), row(jnp.tile(l1_sb_q_norm, SB_HEADS)), row(jnp.tile(l1_sb_k_norm, SB_HEADS)), seg)
    idx = jnp.arange(SB_BLOCK)
    tri = jnp.concatenate([(idx[:, None] >= idx[None, :]).astype(BF16),
                           jnp.ones((SB_BLOCK, SB_BLOCK), BF16)], axis=1)
    tri = jnp.concatenate([tri, tri], axis=0)
    d_out = _stickbreak(qn, kn, vv, tri)
    xf = ffn(xf, [(c_out.reshape(t, -1), bf(l1_w_out[:SSD_WIDTH])), (d_out.reshape(t, -1), bf(l1_w_out[SSD_WIDTH:]))],
             l1_ffn2_norm, l1_ffn2_wg, l1_ffn2_wu, l1_ffn2_wd)
    return xf.reshape(bsz, s, d)
```

```python
import functools
import math

import jax
import jax.numpy as jnp
from jax import lax
from jax.experimental import pallas as pl
from jax.experimental.pallas import tpu as pltpu

F32 = jnp.float32
BF16 = jnp.bfloat16

EPS = 1e-6
FFN_RES = 0.5
SHORT_CONV = 4
LANES = 128
BF16_SUBLANES = 16

POOL_WINDOWS = (2, 4, 8, 16)
POOL_WIDTH = 512
MLSTM_HEADS = 4
MLSTM_HEAD_DIM = 128
MLSTM_WIDTH = MLSTM_HEADS * MLSTM_HEAD_DIM
CHUNK = 128

SSD_HEADS = 16
SSD_HEAD_DIM = 64
SSD_WIDTH = SSD_HEADS * SSD_HEAD_DIM
SSD_GROUPS = 4
SSD_STATE = 128
SSD_GROUP_WIDTH = SSD_WIDTH // SSD_GROUPS
SB_HEADS = 8
SB_HEAD_DIM = 64
SB_WIDTH = SB_HEADS * SB_HEAD_DIM
SB_BLOCK = 128
SB_EXIT = 104.0
SB_TILES = 4
SB_NEAR = 3

VMEM_LIMIT_BYTES = 56 * 1024 * 1024

FFN_TM = 512
FFN_CHUNK = 512
MIXER_TS = 512


def _cparams(semantics):
    return pltpu.CompilerParams(dimension_semantics=semantics, vmem_limit_bytes=VMEM_LIMIT_BYTES)


def _rmsnorm(x, w):
    return x * lax.rsqrt(jnp.mean(x * x, axis=-1, keepdims=True) + EPS) * w


def _sigmoid(x):
    return 1.0 / (1.0 + jnp.exp(-x))


def _sigmoid_tanh(x):
    return 0.5 + 0.5 * jnp.tanh(0.5 * x)


def _silu_tanh(x):
    h = 0.5 * x
    return h + h * jnp.tanh(h)


def _softplus(x):
    return jnp.maximum(x, 0.0) + jnp.log(1.0 + jnp.exp(-jnp.abs(x)))


def _dot(a, b):
    return jnp.dot(a, b, preferred_element_type=F32)


def _dot_nt(a, b):
    return lax.dot_general(a, b, (((1,), (1,)), ((), ())), preferred_element_type=F32)


def _split_bf16(x):
    hi = x.astype(BF16)
    lo = (x - hi.astype(F32)).astype(BF16)
    return hi, lo


def _row_iota(shape):
    return lax.broadcasted_iota(jnp.int32, shape, 0)


def _lane_iota(shape):
    return lax.broadcasted_iota(jnp.int32, shape, 1)


def _chunk_scan(x, op, fill):
    rows = _row_iota(x.shape)
    sh = 1
    while sh < x.shape[0]:
        shifted = jnp.where(rows >= sh, pltpu.roll(x, sh, 0), fill)
        x = op(x, shifted)
        sh *= 2
    return x


def _causal_conv(buf_ref, raw, w_ref, b_ref):
    ts = raw.shape[0]
    buf_ref[CONV_HALO:, :] = raw
    acc = raw * w_ref[SHORT_CONV - 1:SHORT_CONV, :] + b_ref[...]
    for k in range(SHORT_CONV - 1):
        off = CONV_HALO - (SHORT_CONV - 1 - k)
        acc = acc + buf_ref[off:off + ts, :] * w_ref[k:k + 1, :]
    buf_ref[0:CONV_HALO, :] = raw[ts - CONV_HALO:, :]
    return acc


def _staggered(gens):
    pending, live = list(gens), []
    while pending or live:
        if pending:
            live.append(pending.pop(0))
        for g in list(live):
            try:
                next(g)
            except StopIteration:
                live.remove(g)


def _expand_heads(slab, n_heads, head_dim):
    per = LANES // head_dim
    m = slab.shape[0]
    lane = _lane_iota((m, LANES))
    pieces = []
    for p in range(n_heads // per):
        piece = jnp.broadcast_to(slab[:, p * per:p * per + 1], (m, LANES))
        for j in range(1, per):
            piece = jnp.where(lane >= j * head_dim, slab[:, p * per + j:p * per + j + 1], piece)
        pieces.append(piece)
    return jnp.concatenate(pieces, axis=1)


def _ffn_kernel(*refs, n_mix, n_cast, ff_chunk):
    x_ref = refs[0]
    mix_refs = refs[1:1 + 2 * n_mix]
    nw_ref, wg_ref, wu_ref, wd_ref = refs[1 + 2 * n_mix:5 + 2 * n_mix]
    cast_in = refs[5 + 2 * n_mix:5 + 2 * n_mix + n_cast]
    o_ref = refs[5 + 2 * n_mix + n_cast]
    cast_out = refs[6 + 2 * n_mix + n_cast:]
    for src, dst in zip(cast_in, cast_out):
        dst[...] = src[...].astype(dst.dtype)
    x = x_ref[...]
    for i in range(n_mix):
        x = x + _dot(mix_refs[2 * i][...], mix_refs[2 * i + 1][...])
    h = _rmsnorm(x, nw_ref[...]).astype(BF16)
    d_ff = wg_ref.shape[1]
    y = None
    for c0 in range(0, d_ff, ff_chunk):
        c1 = min(c0 + ff_chunk, d_ff)
        g = _dot(h, wg_ref[:, c0:c1])
        u = _dot(h, wu_ref[:, c0:c1])
        a = (g * _sigmoid(g) * u).astype(BF16)
        part = _dot(a, wd_ref[c0:c1, :])
        y = part if y is None else y + part
    o_ref[...] = x + FFN_RES * y


def _cast_block_rows(n_rows, n_steps):
    rb = BF16_SUBLANES * pl.cdiv(pl.cdiv(n_rows, n_steps), BF16_SUBLANES)
    while n_rows % rb:
        rb += BF16_SUBLANES
    return rb


def _ffn(x, mixes, norm_w, wg, wu, wd, casts=(), *, tm=FFN_TM, ff_chunk=FFN_CHUNK):
    t, d = x.shape
    d_ff = wg.shape[1]
    n_steps = t // tm
    const = lambda i: (0, 0)
    in_specs = [pl.BlockSpec((tm, d), lambda i: (i, 0))]
    args = [x]
    for mix, w_out in mixes:
        in_specs += [pl.BlockSpec((tm, mix.shape[1]), lambda i: (i, 0)),
                     pl.BlockSpec(w_out.shape, const)]
        args += [mix, w_out]
    in_specs += [pl.BlockSpec((1, d), const), pl.BlockSpec((d, d_ff), const),
                 pl.BlockSpec((d, d_ff), const), pl.BlockSpec((d_ff, d), const)]
    args += [norm_w, wg, wu, wd]
    out_specs = [pl.BlockSpec((tm, d), lambda i: (i, 0))]
    out_shape = [jax.ShapeDtypeStruct((t, d), F32)]
    for w in casts:
        rb = _cast_block_rows(w.shape[0], n_steps)
        spec = pl.BlockSpec((rb, w.shape[1]), lambda i, last=w.shape[0] // rb - 1: (jnp.minimum(i, last), 0))
        in_specs.append(spec)
        args.append(w)
        out_specs.append(spec)
        out_shape.append(jax.ShapeDtypeStruct(w.shape, BF16))
    outs = pl.pallas_call(
        functools.partial(_ffn_kernel, n_mix=len(mixes), n_cast=len(casts), ff_chunk=ff_chunk),
        grid=(n_steps,),
        in_specs=in_specs,
        out_specs=out_specs,
        out_shape=out_shape,
        compiler_params=_cparams(("arbitrary",)),
        name="ffn",
    )(*args)
    return outs[0], outs[1:]


EVEN_U = 0
EVEN_QK = EVEN_U + POOL_WIDTH
EVEN_V = EVEN_QK + 2 * MLSTM_WIDTH
EVEN_O = EVEN_V + MLSTM_WIDTH
EVEN_GI = EVEN_O + MLSTM_WIDTH
EVEN_GF = EVEN_GI + LANES
EVEN_COLS = EVEN_GF + LANES
POOL_HALO = 16
CONV_HALO = 8


def _even_kernel(x_ref, nw_ref, win_ref, poolw_ref, pscale_ref, convw_ref, convb_ref,
                 gbias_ref, mnorm_ref, o_ref,
                 halo_u, halo_qk, state, m_state, *, ts):
    s_idx = pl.program_id(1)

    @pl.when(s_idx == 0)
    def _():
        halo_u[...] = jnp.zeros_like(halo_u)
        halo_qk[0:CONV_HALO, :] = jnp.zeros((CONV_HALO, halo_qk.shape[1]), F32)
        state[...] = jnp.zeros_like(state)
        m_state[...] = jnp.zeros_like(m_state)

    h = _rmsnorm(x_ref[...], nw_ref[...]).astype(BF16)
    full_proj = _dot(h, win_ref[...])
    proj = lambda c0, n: full_proj[:, c0:c0 + n]

    u = proj(EVEN_U, POOL_WIDTH)
    ue = jnp.concatenate([halo_u[...], u], axis=0)
    halo_u[...] = u[ts - POOL_HALO:, :]
    pos = (s_idx * ts + 1 + _row_iota((ts, 1))).astype(F32)
    for g, win in enumerate(POOL_WINDOWS):
        acc = ue[:, g * LANES:(g + 1) * LANES]
        sh = 1
        while sh < win:
            acc = acc + pltpu.roll(acc, sh, 0)
            sh *= 2
        win_sum = acc[POOL_HALO:, :]
        pooled = win_sum / jnp.minimum(pos, float(win)) - u[:, g * LANES:(g + 1) * LANES]
        mixed = _dot(pooled.astype(BF16), poolw_ref[g])
        o_ref[:, g * LANES:(g + 1) * LANES] = (
            mixed * pscale_ref[:, g * LANES:(g + 1) * LANES]).astype(o_ref.dtype)

    qk_raw = proj(EVEN_QK, 2 * MLSTM_WIDTH)
    conv = _causal_conv(halo_qk, qk_raw, convw_ref, convb_ref)
    qk = _silu_tanh(conv)
    q_all = qk[:, :MLSTM_WIDTH].astype(BF16)
    k_all = qk[:, MLSTM_WIDTH:] * (MLSTM_HEAD_DIM ** -0.5)
    v_all = proj(EVEN_V, MLSTM_WIDTH).astype(BF16)
    o_gate = _sigmoid_tanh(proj(EVEN_O, MLSTM_WIDTH))

    g_i = proj(EVEN_GI, LANES) + gbias_ref[0:1, :]
    g_f = -_softplus(-(proj(EVEN_GF, LANES) + gbias_ref[1:2, :]))

    ones_blk = jnp.ones((CHUNK, MLSTM_HEAD_DIM), BF16)
    causal = _row_iota((CHUNK, CHUNK)) >= _lane_iota((CHUNK, CHUNK))
    for c in range(ts // CHUNK):
        r0, r1 = c * CHUNK, (c + 1) * CHUNK
        li = g_i[r0:r1, :]
        b = _chunk_scan(g_f[r0:r1, :], jnp.add, 0.0)
        a = li - b
        m_prev = m_state[...]
        big_m = jnp.maximum(m_prev, _chunk_scan(a, jnp.maximum, -jnp.inf))
        b_last = b[CHUNK - 1:CHUNK, :]
        m_new = b_last + big_m[CHUNK - 1:CHUNK, :]
        w_inter = jnp.exp(m_prev - big_m)
        e_negm = jnp.exp(-(b + big_m))
        w_state = jnp.exp(b_last + a - m_new)
        decay = jnp.exp(b_last + m_prev - m_new)
        m_state[...] = m_new
        a_t = a.T
        def head_pipeline(hd):
            c0, c1 = hd * MLSTM_HEAD_DIM, (hd + 1) * MLSTM_HEAD_DIM
            qc = q_all[r0:r1, c0:c1]
            kc = k_all[r0:r1, c0:c1]
            v_ext = jnp.concatenate([v_all[r0:r1, c0:c1], ones_blk], axis=1)
            st = state[hd]
            s_qk = _dot_nt(qc, kc.astype(BF16))
            inter = _dot(qc, st.astype(BF16))
            yield
            w_intra = jnp.exp(jnp.where(causal, a_t[hd:hd + 1, :] - big_m[:, hd:hd + 1], -jnp.inf))
            p = (s_qk * w_intra).astype(BF16)
            kw_t = (kc * w_state[:, hd:hd + 1]).T.astype(BF16)
            yield
            pv = _dot(p, v_ext)
            state[hd] = decay[:, hd:hd + 1] * st + _dot(kw_t, v_ext)
            yield
            numden = w_inter[:, hd:hd + 1] * inter + pv
            den = jnp.maximum(jnp.abs(numden[:, MLSTM_HEAD_DIM:MLSTM_HEAD_DIM + 1]),
                              e_negm[:, hd:hd + 1])
            hh = numden[:, :MLSTM_HEAD_DIM] / den
            hn = _rmsnorm(hh, mnorm_ref[:, c0:c1])
            o_ref[r0:r1, POOL_WIDTH + c0:POOL_WIDTH + c1] = (
                o_gate[r0:r1, c0:c1] * hn).astype(o_ref.dtype)

        _staggered([head_pipeline(hd) for hd in range(MLSTM_HEADS)])


def _even_mixer(x, norm_w, w_in, pool_w, pool_scale, conv_w, conv_b, gate_bias, mnorm, *, ts=MIXER_TS):
    bsz, s, d = x.shape
    const2 = lambda b, i: (0, 0)
    return pl.pallas_call(
        functools.partial(_even_kernel, ts=ts),
        grid=(bsz, s // ts),
        in_specs=[
            pl.BlockSpec((None, ts, d), lambda b, i: (b, i, 0)),
            pl.BlockSpec((1, d), const2),
            pl.BlockSpec(w_in.shape, const2),
            pl.BlockSpec(pool_w.shape, lambda b, i: (0, 0, 0)),
            pl.BlockSpec(pool_scale.shape, const2),
            pl.BlockSpec(conv_w.shape, const2),
            pl.BlockSpec(conv_b.shape, const2),
            pl.BlockSpec(gate_bias.shape, const2),
            pl.BlockSpec(mnorm.shape, const2),
        ],
        out_specs=pl.BlockSpec((None, ts, POOL_WIDTH + MLSTM_WIDTH), lambda b, i: (b, i, 0)),
        out_shape=jax.ShapeDtypeStruct((bsz, s, POOL_WIDTH + MLSTM_WIDTH), BF16),
        scratch_shapes=[
            pltpu.VMEM((POOL_HALO, POOL_WIDTH), F32),
            pltpu.VMEM((CONV_HALO + ts, 2 * MLSTM_WIDTH), F32),
            pltpu.VMEM((MLSTM_HEADS, MLSTM_HEAD_DIM, 2 * MLSTM_HEAD_DIM), F32),
            pltpu.VMEM((1, LANES), F32),
        ],
        compiler_params=_cparams(("parallel", "arbitrary")),
        name="even_mixer",
    )(x, norm_w, w_in, pool_w, pool_scale, conv_w, conv_b, gate_bias, mnorm)


ODD_Z = 0
ODD_XBC = ODD_Z + SSD_WIDTH
ODD_XS = ODD_XBC
ODD_B = ODD_XS + SSD_WIDTH
ODD_C = ODD_B + SSD_GROUPS * SSD_STATE
ODD_Q = ODD_C + SSD_GROUPS * SSD_STATE
ODD_K = ODD_Q + SB_WIDTH
ODD_V = ODD_K + SB_WIDTH
ODD_DT = ODD_V + SB_WIDTH
ODD_COLS = ODD_DT + LANES
SSD_XBC = SSD_WIDTH + 2 * SSD_GROUPS * SSD_STATE


def _odd_kernel(x_ref, nw_ref, win_ref, convw_ref, convb_ref, dtb_ref, alog_ref, dskip_ref,
                snorm_ref, qn_ref, kn_ref, seg_ref,
                c_ref, q_ref, k_ref, v_ref,
                halo, hstate, *, ts):
    @pl.when(pl.program_id(1) == 0)
    def _():
        halo[0:CONV_HALO, :] = jnp.zeros((CONV_HALO, halo.shape[1]), F32)
        hstate[...] = jnp.zeros_like(hstate)

    h = _rmsnorm(x_ref[...], nw_ref[...]).astype(BF16)
    proj = lambda c0, n: _dot(h, win_ref[:, c0:c0 + n])

    seg = seg_ref[...]
    for src, nref, dst, scale in ((ODD_Q, qn_ref, q_ref, SB_HEAD_DIM ** -0.5), (ODD_K, kn_ref, k_ref, 1.0)):
        t = proj(src, SB_WIDTH)
        hi, lo = _split_bf16(t * t)
        ssq = _dot(hi, seg) + _dot(lo, seg)
        dst[...] = (t * lax.rsqrt(ssq * (1.0 / SB_HEAD_DIM) + EPS) * (nref[...] * scale)).astype(dst.dtype)
    v_ref[...] = proj(ODD_V, SB_WIDTH).astype(v_ref.dtype)

    raw = proj(ODD_XBC, SSD_XBC)
    conv = _causal_conv(halo, raw, convw_ref, convb_ref)
    xbc = _silu_tanh(conv)
    xs_all = xbc[:, :SSD_WIDTH]
    bm_all = xbc[:, SSD_WIDTH:SSD_WIDTH + SSD_GROUPS * SSD_STATE].astype(BF16)
    cm_all = xbc[:, SSD_WIDTH + SSD_GROUPS * SSD_STATE:].astype(BF16)

    dt_all = _softplus(proj(ODD_DT, LANES) + dtb_ref[...])
    a_all = dt_all * (-jnp.exp(alog_ref[...]))
    zg = proj(ODD_Z, SSD_WIDTH)
    zg = _silu_tanh(zg)

    causal = _row_iota((CHUNK, CHUNK)) >= _lane_iota((CHUNK, CHUNK))
    lane = _lane_iota((CHUNK, LANES))
    heads_per_group = SSD_HEADS // SSD_GROUPS
    for c in range(ts // CHUNK):
        r0, r1 = c * CHUNK, (c + 1) * CHUNK
        xs = xs_all[r0:r1, :]
        a_cum = _chunk_scan(a_all[r0:r1, :], jnp.add, 0.0)
        a_last = a_cum[CHUNK - 1:CHUNK, :]
        a_t = a_cum.T
        xdt = xs * _expand_heads(dt_all[r0:r1, :], SSD_HEADS, SSD_HEAD_DIM)
        a_wide = _expand_heads(a_cum, SSD_HEADS, SSD_HEAD_DIM)
        a_last_wide = a_wide[CHUNK - 1:CHUNK, :]
        xw = (xdt * jnp.exp(a_last_wide - a_wide)).astype(BF16)
        xdt = xdt.astype(BF16)
        from_start = jnp.exp(a_wide)
        chunk_decay = jnp.exp(a_last_wide)

        for g in range(SSD_GROUPS):
            bg = bm_all[r0:r1, g * SSD_STATE:(g + 1) * SSD_STATE]
            cg = cm_all[r0:r1, g * SSD_STATE:(g + 1) * SSD_STATE]
            gs = slice(g * SSD_GROUP_WIDTH, (g + 1) * SSD_GROUP_WIDTH)
            cb = _dot_nt(cg, bg)
            hprev = hstate[g]
            y_off = _dot(cg, hprev.astype(BF16))
            bg_t = bg.astype(F32).T.astype(BF16)
            hstate[g] = chunk_decay[:, gs] * hprev + _dot(bg_t, xw[:, gs])
            masked = []
            for hh in range(g * heads_per_group, (g + 1) * heads_per_group):
                dec = jnp.exp(jnp.where(causal, a_cum[:, hh:hh + 1] - a_t[hh:hh + 1, :], -jnp.inf))
                masked.append((cb * dec).astype(BF16))
            diag = []
            for pair in range(heads_per_group // 2):
                ps = slice(g * SSD_GROUP_WIDTH + pair * LANES, g * SSD_GROUP_WIDTH + (pair + 1) * LANES)
                outs = [_dot(masked[2 * pair + j], xdt[:, ps]) for j in range(2)]
                diag.append(jnp.where(lane < SSD_HEAD_DIM, outs[0], outs[1]))
            y = jnp.concatenate(diag, axis=1) + y_off * from_start[:, gs] + dskip_ref[:, gs] * xs[:, gs]
            c_ref[r0:r1, gs] = _rmsnorm(y * zg[r0:r1, gs], snorm_ref[:, gs]).astype(c_ref.dtype)


def _odd_mixer(x, norm_w, w_in, conv_w, conv_b, dt_bias, a_log, d_skip, snorm, qn, kn, seg, *, ts=MIXER_TS):
    bsz, s, d = x.shape
    const2 = lambda b, i: (0, 0)
    tile = lambda w: pl.BlockSpec((None, ts, w), lambda b, i: (b, i, 0))
    return pl.pallas_call(
        functools.partial(_odd_kernel, ts=ts),
        grid=(bsz, s // ts),
        in_specs=[tile(d)] + [pl.BlockSpec(a.shape, const2) for a in
                              (norm_w, w_in, conv_w, conv_b, dt_bias, a_log, d_skip, snorm, qn, kn, seg)],
        out_specs=[tile(SSD_WIDTH), tile(SB_WIDTH), tile(SB_WIDTH), tile(SB_WIDTH)],
        out_shape=[jax.ShapeDtypeStruct((bsz, s, SSD_WIDTH), BF16)]
        + [jax.ShapeDtypeStruct((bsz, s, SB_WIDTH), BF16)] * 3,
        scratch_shapes=[
            pltpu.VMEM((CONV_HALO + ts, SSD_XBC), F32),
            pltpu.VMEM((SSD_GROUPS, SSD_STATE, SSD_GROUP_WIDTH), F32),
        ],
        compiler_params=_cparams(("parallel", "arbitrary")),
        name="odd_mixer",
    )(x, norm_w, w_in, conv_w, conv_b, dt_bias, a_log, d_skip, snorm, qn, kn, seg)


def _sb_kernel(q_ref, k_ref, v_ref, tri_ref, o_ref, acc_ref, r_ref, rmin_ref):
    step = pl.program_id(1)
    blk = SB_BLOCK
    pairs = SB_WIDTH // LANES
    first = _lane_iota((blk, LANES)) < SB_HEAD_DIM
    zero = jnp.zeros((blk, LANES), q_ref.dtype)
    q_stack = []
    for t in range(SB_TILES):
        q = q_ref[t * blk:(t + 1) * blk, :]
        for p in range(pairs):
            qp = q[:, p * LANES:(p + 1) * LANES]
            q_stack.append(jnp.concatenate([jnp.where(first, qp, zero), jnp.where(first, zero, qp)], axis=0))
    tri = tri_ref[...]
    strict = jnp.concatenate([_row_iota((blk, blk)) > _lane_iota((blk, blk))] * 2, axis=0)

    def pair_pipeline(t, p, j_lo, nblk, diag, out):
        k0 = pl.multiple_of(j_lo * blk, blk)
        cols = slice(p * LANES, (p + 1) * LANES)
        r = None if diag else r_ref[t, 2 * p * blk:2 * (p + 1) * blk, :]
        z = _dot_nt(q_stack[t * pairs + p], k_ref[pl.ds(k0, nblk * blk), cols])
        yield
        sp = _softplus(z)
        sp_near_first = [sp[:, b * blk:(b + 1) * blk] for b in reversed(range(nblk))]
        if diag:
            sp_near_first[0] = jnp.where(strict, sp_near_first[0], 0.0)
        hi, lo = _split_bf16(jnp.concatenate(sp_near_first, axis=0))
        yield
        cs = _dot(jnp.concatenate([hi, lo], axis=1), tri)
        yield
        ws = []
        for i in range(nblk):
            b = nblk - 1 - i
            cs_b = cs[2 * i * blk:2 * (i + 1) * blk]
            arg = z[:, b * blk:(b + 1) * blk] - cs_b[:, :blk]
            w_b = jnp.exp(arg if r is None else arg - r)
            if diag and i == 0:
                w_b = jnp.where(strict, w_b, 0.0)
            ws.append(w_b.astype(BF16))
            r = cs_b[:, blk:] if r is None else r + cs_b[:, blk:]
        w = jnp.concatenate(ws[::-1], axis=1)
        yield
        pv = _dot(w, v_ref[pl.ds(k0, nblk * blk), cols])
        out(t, p, jnp.where(first, pv[:blk], pv[blk:]), r)

    def process(jobs, diag):
        rmins = {t: [] for t, _, _ in jobs}

        def out(t, p, acc, r):
            cols = slice(p * LANES, (p + 1) * LANES)
            if diag:
                acc_ref[t, :, cols] = acc
            else:
                acc_ref[t, :, cols] += acc
            r_ref[t, 2 * p * blk:2 * (p + 1) * blk, :] = r
            rmins[t].append(jnp.min(r))

        _staggered([pair_pipeline(t, p, j_lo, nblk, diag, out) for t, j_lo, nblk in jobs for p in range(pairs)])
        return {t: functools.reduce(jnp.minimum, v) for t, v in rmins.items()}

    def near(nblk_of):
        jobs = [(t, step * SB_TILES + t - (nblk_of(t) - 1), nblk_of(t)) for t in range(SB_TILES)]
        for t, m in process(jobs, True).items():
            rmin_ref[t] = m

    head_steps = -(-(SB_NEAR - 1) // SB_TILES)

    @pl.when(step >= head_steps)
    def _():
        near(lambda t: SB_NEAR)

    for s0 in range(head_steps):
        @pl.when(step == s0)
        def _():
            near(lambda t: min(SB_NEAR, s0 * SB_TILES + t + 1))

    for t in range(SB_TILES):
        qi = step * SB_TILES + t

        def cond(c):
            return jnp.logical_and(c[0] >= 0, c[1] < SB_EXIT)

        def body(c, t=t):
            return c[0] - 1, process([(t, c[0], 1)], False)[t]

        lax.while_loop(cond, body, (qi - jnp.minimum(qi + 1, SB_NEAR), rmin_ref[t]))
        o_ref[t * blk:(t + 1) * blk, :] = acc_ref[t].astype(o_ref.dtype)


def _stickbreak(q, k, v, tri):
    bsz, s, w = q.shape
    rows = SB_TILES * SB_BLOCK
    return pl.pallas_call(
        _sb_kernel,
        grid=(bsz, s // rows),
        in_specs=[
            pl.BlockSpec((None, rows, w), lambda b, i: (b, i, 0)),
            pl.BlockSpec((None, s, w), lambda b, i: (b, 0, 0)),
            pl.BlockSpec((None, s, w), lambda b, i: (b, 0, 0)),
            pl.BlockSpec(tri.shape, lambda b, i: (0, 0)),
        ],
        out_specs=pl.BlockSpec((None, rows, w), lambda b, i: (b, i, 0)),
        out_shape=jax.ShapeDtypeStruct((bsz, s, w), BF16),
        scratch_shapes=[pltpu.VMEM((SB_TILES, SB_BLOCK, w), F32),
                        pltpu.VMEM((SB_TILES, 2 * (w // LANES) * SB_BLOCK, SB_BLOCK), F32),
                        pltpu.SMEM((SB_TILES,), F32)],
        compiler_params=_cparams(("parallel", "arbitrary")),
        name="stickbreak",
    )(q, k, v, tri)


def _pad_cols(w, n):
    return jnp.pad(w, ((0, 0), (0, n - w.shape[1])))


def kernel(x, l0_ffn1_norm, l0_ffn1_wg, l0_ffn1_wu, l0_ffn1_wd, l0_mix_norm, l0_w_in, l0_pool_w, l0_pool_scale, l0_qk_conv_w, l0_qk_conv_b, l0_gate_bias, l0_mlstm_norm, l0_w_out, l0_ffn2_norm, l0_ffn2_wg, l0_ffn2_wu, l0_ffn2_wd, l1_ffn1_norm, l1_ffn1_wg, l1_ffn1_wu, l1_ffn1_wd, l1_mix_norm, l1_w_in, l1_ssd_conv_w, l1_ssd_conv_b, l1_ssd_dt_bias, l1_ssd_A_log, l1_ssd_D, l1_ssd_norm, l1_sb_q_norm, l1_sb_k_norm, l1_w_out, l1_ffn2_norm, l1_ffn2_wg, l1_ffn2_wu, l1_ffn2_wd):
    bsz, s, d = x.shape
    t = bsz * s
    row = lambda a: a.reshape(1, -1).astype(F32)
    bf = lambda a: a.astype(BF16)

    def ffn(xf, mixes, norm_w, w3, next_w3=()):
        return _ffn(xf, mixes, row(norm_w), *[bf(w) for w in w3], casts=tuple(next_w3))

    xf, w_l0f2 = ffn(x.reshape(t, d), [], l0_ffn1_norm, (l0_ffn1_wg, l0_ffn1_wu, l0_ffn1_wd),
                     (l0_ffn2_wg, l0_ffn2_wu, l0_ffn2_wd))
    n_main = EVEN_GI
    w_in0 = jnp.concatenate([
        bf(l0_w_in[:, :n_main]),
        _pad_cols(bf(l0_w_in[:, n_main:n_main + MLSTM_HEADS]), LANES),
        _pad_cols(bf(l0_w_in[:, n_main + MLSTM_HEADS:]), LANES)], axis=1)
    gbias = jnp.stack([jnp.pad(l0_gate_bias[:MLSTM_HEADS], (0, LANES - MLSTM_HEADS)),
                       jnp.pad(l0_gate_bias[MLSTM_HEADS:], (0, LANES - MLSTM_HEADS))]).astype(F32)
    mix0 = _even_mixer(xf.reshape(bsz, s, d), row(l0_mix_norm), w_in0, bf(l0_pool_w),
                       row(l0_pool_scale), l0_qk_conv_w.astype(F32), row(l0_qk_conv_b), gbias,
                       row(l0_mlstm_norm))
    xf, w_l1f1 = ffn(xf, [(mix0.reshape(t, -1), bf(l0_w_out))], l0_ffn2_norm, w_l0f2,
                     (l1_ffn1_wg, l1_ffn1_wu, l1_ffn1_wd))

    xf, w_l1f2 = ffn(xf, [], l1_ffn1_norm, w_l1f1, (l1_ffn2_wg, l1_ffn2_wu, l1_ffn2_wd))
    c_dt = SSD_WIDTH + SSD_XBC
    c_q = c_dt + SSD_HEADS
    w_in1 = jnp.concatenate([bf(l1_w_in[:, :c_dt]), bf(l1_w_in[:, c_q:]),
                             _pad_cols(bf(l1_w_in[:, c_dt:c_q]), LANES)], axis=1)
    pad_heads = lambda a: jnp.pad(a.astype(F32), (0, LANES - SSD_HEADS)).reshape(1, LANES)
    seg_id = jnp.arange(SB_WIDTH) // SB_HEAD_DIM
    seg = (seg_id[:, None] == seg_id[None, :]).astype(BF16)
    c_out, qn, kn, vv = _odd_mixer(
        xf.reshape(bsz, s, d), row(l1_mix_norm), w_in1, l1_ssd_conv_w.astype(F32), row(l1_ssd_conv_b),
        pad_heads(l1_ssd_dt_bias), pad_heads(l1_ssd_A_log), row(jnp.repeat(l1_ssd_D, SSD_HEAD_DIM)),
        row(l1_ssd_norm), row(jnp.tile(l1_sb_q_norm, SB_HEADS)), row(jnp.tile(l1_sb_k_norm, SB_HEADS)), seg)
    idx = jnp.arange(SB_BLOCK)
    tri = jnp.concatenate([(idx[:, None] >= idx[None, :]).astype(BF16),
                           jnp.ones((SB_BLOCK, SB_BLOCK), BF16)], axis=1)
    tri = jnp.concatenate([tri, tri], axis=0)
    d_out = _stickbreak(qn, kn, vv, tri)
    xf, _ = ffn(xf, [(c_out.reshape(t, -1), bf(l1_w_out[:SSD_WIDTH])), (d_out.reshape(t, -1), bf(l1_w_out[SSD_WIDTH:]))],
                l1_ffn2_norm, w_l1f2)
    return xf.reshape(bsz, s, d)
```

```python
import functools
import math

import jax
import jax.numpy as jnp
from jax import lax
from jax.experimental import pallas as pl
from jax.experimental.pallas import tpu as pltpu

F32 = jnp.float32
BF16 = jnp.bfloat16

EPS = 1e-6
FFN_RES = 0.5
SHORT_CONV = 4
LANES = 128
BF16_SUBLANES = 16

POOL_WINDOWS = (2, 4, 8, 16)
POOL_WIDTH = 512
MLSTM_HEADS = 4
MLSTM_HEAD_DIM = 128
MLSTM_WIDTH = MLSTM_HEADS * MLSTM_HEAD_DIM
CHUNK = 128

SSD_HEADS = 16
SSD_HEAD_DIM = 64
SSD_WIDTH = SSD_HEADS * SSD_HEAD_DIM
SSD_GROUPS = 4
SSD_STATE = 128
SSD_GROUP_WIDTH = SSD_WIDTH // SSD_GROUPS
SB_HEADS = 8
SB_HEAD_DIM = 64
SB_WIDTH = SB_HEADS * SB_HEAD_DIM
SB_BLOCK = 128
SB_EXIT = 104.0
SB_TILES = 4
SB_NEAR = 3

VMEM_LIMIT_BYTES = 56 * 1024 * 1024

FFN_TM = 512
FFN_CHUNK = 512
MIXER_TS = 512


def _cparams(semantics):
    return pltpu.CompilerParams(dimension_semantics=semantics, vmem_limit_bytes=VMEM_LIMIT_BYTES)


def _rmsnorm(x, w):
    return x * lax.rsqrt(jnp.mean(x * x, axis=-1, keepdims=True) + EPS) * w


def _sigmoid(x):
    return 1.0 / (1.0 + jnp.exp(-x))


def _sigmoid_tanh(x):
    return 0.5 + 0.5 * jnp.tanh(0.5 * x)


def _silu_tanh(x):
    h = 0.5 * x
    return h + h * jnp.tanh(h)


def _softplus(x):
    return jnp.maximum(x, 0.0) + jnp.log(1.0 + jnp.exp(-jnp.abs(x)))


def _dot(a, b):
    return jnp.dot(a, b, preferred_element_type=F32)


def _dot_nt(a, b):
    return lax.dot_general(a, b, (((1,), (1,)), ((), ())), preferred_element_type=F32)


def _split_bf16(x):
    hi = x.astype(BF16)
    lo = (x - hi.astype(F32)).astype(BF16)
    return hi, lo


def _row_iota(shape):
    return lax.broadcasted_iota(jnp.int32, shape, 0)


def _lane_iota(shape):
    return lax.broadcasted_iota(jnp.int32, shape, 1)


def _chunk_scan(x, op, fill):
    rows = _row_iota(x.shape)
    sh = 1
    while sh < x.shape[0]:
        shifted = jnp.where(rows >= sh, pltpu.roll(x, sh, 0), fill)
        x = op(x, shifted)
        sh *= 2
    return x


def _causal_conv(halo_ref, raw, w_ref, b_ref):
    assert w_ref.shape[0] == 4
    ts = raw.shape[0]
    xe = jnp.concatenate([halo_ref[...], raw], axis=0)
    halo_ref[...] = raw[ts - CONV_HALO:, :]
    d1 = pltpu.roll(xe, 1, 0)
    pair = xe * w_ref[1:2, :] + d1 * w_ref[0:1, :]
    out = xe * w_ref[3:4, :] + d1 * w_ref[2:3, :] + pltpu.roll(pair, 2, 0)
    return out[CONV_HALO:, :] + b_ref[...]


def _staggered(gens):
    pending, live = list(gens), []
    while pending or live:
        if pending:
            live.append(pending.pop(0))
        for g in list(live):
            try:
                next(g)
            except StopIteration:
                live.remove(g)


def _expand_heads(slab, n_heads, head_dim):
    per = LANES // head_dim
    m = slab.shape[0]
    lane = _lane_iota((m, LANES))
    pieces = []
    for p in range(n_heads // per):
        piece = jnp.broadcast_to(slab[:, p * per:p * per + 1], (m, LANES))
        for j in range(1, per):
            piece = jnp.where(lane >= j * head_dim, slab[:, p * per + j:p * per + j + 1], piece)
        pieces.append(piece)
    return jnp.concatenate(pieces, axis=1)


def _ffn_kernel(*refs, n_mix, n_cast, ff_chunk):
    x_ref = refs[0]
    mix_refs = refs[1:1 + 2 * n_mix]
    nw_ref, wg_ref, wu_ref, wd_ref = refs[1 + 2 * n_mix:5 + 2 * n_mix]
    cast_in = refs[5 + 2 * n_mix:5 + 2 * n_mix + n_cast]
    o_ref = refs[5 + 2 * n_mix + n_cast]
    cast_out = refs[6 + 2 * n_mix + n_cast:]
    for src, dst in zip(cast_in, cast_out):
        dst[...] = src[...].astype(dst.dtype)
    x = x_ref[...]
    for i in range(n_mix):
        x = x + _dot(mix_refs[2 * i][...], mix_refs[2 * i + 1][...])
    h = _rmsnorm(x, nw_ref[...]).astype(BF16)
    d_ff = wg_ref.shape[1]
    y = None
    for c0 in range(0, d_ff, ff_chunk):
        c1 = min(c0 + ff_chunk, d_ff)
        g = _dot(h, wg_ref[:, c0:c1])
        u = _dot(h, wu_ref[:, c0:c1])
        a = (g * _sigmoid(g) * u).astype(BF16)
        part = _dot(a, wd_ref[c0:c1, :])
        y = part if y is None else y + part
    o_ref[...] = x + FFN_RES * y


def _cast_block_rows(n_rows, n_steps):
    rb = BF16_SUBLANES * pl.cdiv(pl.cdiv(n_rows, n_steps), BF16_SUBLANES)
    while n_rows % rb:
        rb += BF16_SUBLANES
    return rb


def _ffn(x, mixes, norm_w, wg, wu, wd, casts=(), *, tm=FFN_TM, ff_chunk=FFN_CHUNK):
    t, d = x.shape
    d_ff = wg.shape[1]
    n_steps = t // tm
    const = lambda i: (0, 0)
    in_specs = [pl.BlockSpec((tm, d), lambda i: (i, 0))]
    args = [x]
    for mix, w_out in mixes:
        in_specs += [pl.BlockSpec((tm, mix.shape[1]), lambda i: (i, 0)),
                     pl.BlockSpec(w_out.shape, const)]
        args += [mix, w_out]
    in_specs += [pl.BlockSpec((1, d), const), pl.BlockSpec((d, d_ff), const),
                 pl.BlockSpec((d, d_ff), const), pl.BlockSpec((d_ff, d), const)]
    args += [norm_w, wg, wu, wd]
    out_specs = [pl.BlockSpec((tm, d), lambda i: (i, 0))]
    out_shape = [jax.ShapeDtypeStruct((t, d), F32)]
    for w in casts:
        rb = _cast_block_rows(w.shape[0], n_steps)
        spec = pl.BlockSpec((rb, w.shape[1]), lambda i, last=w.shape[0] // rb - 1: (jnp.minimum(i, last), 0))
        in_specs.append(spec)
        args.append(w)
        out_specs.append(spec)
        out_shape.append(jax.ShapeDtypeStruct(w.shape, BF16))
    outs = pl.pallas_call(
        functools.partial(_ffn_kernel, n_mix=len(mixes), n_cast=len(casts), ff_chunk=ff_chunk),
        grid=(n_steps,),
        in_specs=in_specs,
        out_specs=out_specs,
        out_shape=out_shape,
        compiler_params=_cparams(("arbitrary",)),
        name="ffn",
    )(*args)
    return outs[0], outs[1:]


EVEN_U = 0
EVEN_QK = EVEN_U + POOL_WIDTH
EVEN_V = EVEN_QK + 2 * MLSTM_WIDTH
EVEN_O = EVEN_V + MLSTM_WIDTH
EVEN_GI = EVEN_O + MLSTM_WIDTH
EVEN_GF = EVEN_GI + LANES
EVEN_COLS = EVEN_GF + LANES
POOL_HALO = 16
CONV_HALO = 8


def _even_kernel(x_ref, nw_ref, win_ref, poolw_ref, pscale_ref, convw_ref, convb_ref,
                 gbias_ref, mnorm_ref, o_ref,
                 halo_u, halo_qk, state, m_state, *, ts):
    s_idx = pl.program_id(1)

    @pl.when(s_idx == 0)
    def _():
        halo_u[...] = jnp.zeros_like(halo_u)
        halo_qk[...] = jnp.zeros_like(halo_qk)
        state[...] = jnp.zeros_like(state)
        m_state[...] = jnp.zeros_like(m_state)

    h = _rmsnorm(x_ref[...], nw_ref[...]).astype(BF16)
    full_proj = _dot(h, win_ref[...])
    proj = lambda c0, n: full_proj[:, c0:c0 + n]

    u = proj(EVEN_U, POOL_WIDTH)
    ue = jnp.concatenate([halo_u[...], u], axis=0)
    halo_u[...] = u[ts - POOL_HALO:, :]
    pos = (s_idx * ts + 1 + _row_iota((ts, 1))).astype(F32)
    for g, win in enumerate(POOL_WINDOWS):
        acc = ue[:, g * LANES:(g + 1) * LANES]
        sh = 1
        while sh < win:
            acc = acc + pltpu.roll(acc, sh, 0)
            sh *= 2
        win_sum = acc[POOL_HALO:, :]
        pooled = win_sum / jnp.minimum(pos, float(win)) - u[:, g * LANES:(g + 1) * LANES]
        mixed = _dot(pooled.astype(BF16), poolw_ref[g])
        o_ref[:, g * LANES:(g + 1) * LANES] = (
            mixed * pscale_ref[:, g * LANES:(g + 1) * LANES]).astype(o_ref.dtype)

    qk_raw = proj(EVEN_QK, 2 * MLSTM_WIDTH)
    conv = _causal_conv(halo_qk, qk_raw, convw_ref, convb_ref)
    qk = _silu_tanh(conv)
    q_all = qk[:, :MLSTM_WIDTH].astype(BF16)
    k_all = qk[:, MLSTM_WIDTH:] * (MLSTM_HEAD_DIM ** -0.5)
    v_all = proj(EVEN_V, MLSTM_WIDTH).astype(BF16)
    o_gate = _sigmoid_tanh(proj(EVEN_O, MLSTM_WIDTH))

    g_i = proj(EVEN_GI, LANES) + gbias_ref[0:1, :]
    g_f = -_softplus(-(proj(EVEN_GF, LANES) + gbias_ref[1:2, :]))

    ones_blk = jnp.ones((CHUNK, MLSTM_HEAD_DIM), BF16)
    causal = _row_iota((CHUNK, CHUNK)) >= _lane_iota((CHUNK, CHUNK))
    for c in range(ts // CHUNK):
        r0, r1 = c * CHUNK, (c + 1) * CHUNK
        li = g_i[r0:r1, :]
        b = _chunk_scan(g_f[r0:r1, :], jnp.add, 0.0)
        a = li - b
        m_prev = m_state[...]
        big_m = jnp.maximum(m_prev, _chunk_scan(a, jnp.maximum, -jnp.inf))
        b_last = b[CHUNK - 1:CHUNK, :]
        m_new = b_last + big_m[CHUNK - 1:CHUNK, :]
        w_inter = jnp.exp(m_prev - big_m)
        e_negm = jnp.exp(-(b + big_m))
        w_state = jnp.exp(b_last + a - m_new)
        decay = jnp.exp(b_last + m_prev - m_new)
        m_state[...] = m_new
        a_t = a.T
        def head_pipeline(hd):
            c0, c1 = hd * MLSTM_HEAD_DIM, (hd + 1) * MLSTM_HEAD_DIM
            qc = q_all[r0:r1, c0:c1]
            kc = k_all[r0:r1, c0:c1]
            v_ext = jnp.concatenate([v_all[r0:r1, c0:c1], ones_blk], axis=1)
            st = state[hd]
            s_qk = _dot_nt(qc, kc.astype(BF16))
            inter = _dot(qc, st.astype(BF16))
            yield
            w_intra = jnp.exp(jnp.where(causal, a_t[hd:hd + 1, :] - big_m[:, hd:hd + 1], -jnp.inf))
            p = (s_qk * w_intra).astype(BF16)
            kw_t = (kc * w_state[:, hd:hd + 1]).T.astype(BF16)
            yield
            pv = _dot(p, v_ext)
            state[hd] = decay[:, hd:hd + 1] * st + _dot(kw_t, v_ext)
            yield
            numden = w_inter[:, hd:hd + 1] * inter + pv
            den = jnp.maximum(jnp.abs(numden[:, MLSTM_HEAD_DIM:MLSTM_HEAD_DIM + 1]),
                              e_negm[:, hd:hd + 1])
            hh = numden[:, :MLSTM_HEAD_DIM] / den
            hn = _rmsnorm(hh, mnorm_ref[:, c0:c1])
            o_ref[r0:r1, POOL_WIDTH + c0:POOL_WIDTH + c1] = (
                o_gate[r0:r1, c0:c1] * hn).astype(o_ref.dtype)

        _staggered([head_pipeline(hd) for hd in range(MLSTM_HEADS)])


def _even_mixer(x, norm_w, w_in, pool_w, pool_scale, conv_w, conv_b, gate_bias, mnorm, *, ts=MIXER_TS):
    bsz, s, d = x.shape
    const2 = lambda b, i: (0, 0)
    return pl.pallas_call(
        functools.partial(_even_kernel, ts=ts),
        grid=(bsz, s // ts),
        in_specs=[
            pl.BlockSpec((None, ts, d), lambda b, i: (b, i, 0)),
            pl.BlockSpec((1, d), const2),
            pl.BlockSpec(w_in.shape, const2),
            pl.BlockSpec(pool_w.shape, lambda b, i: (0, 0, 0)),
            pl.BlockSpec(pool_scale.shape, const2),
            pl.BlockSpec(conv_w.shape, const2),
            pl.BlockSpec(conv_b.shape, const2),
            pl.BlockSpec(gate_bias.shape, const2),
            pl.BlockSpec(mnorm.shape, const2),
        ],
        out_specs=pl.BlockSpec((None, ts, POOL_WIDTH + MLSTM_WIDTH), lambda b, i: (b, i, 0)),
        out_shape=jax.ShapeDtypeStruct((bsz, s, POOL_WIDTH + MLSTM_WIDTH), BF16),
        scratch_shapes=[
            pltpu.VMEM((POOL_HALO, POOL_WIDTH), F32),
            pltpu.VMEM((CONV_HALO, 2 * MLSTM_WIDTH), F32),
            pltpu.VMEM((MLSTM_HEADS, MLSTM_HEAD_DIM, 2 * MLSTM_HEAD_DIM), F32),
            pltpu.VMEM((1, LANES), F32),
        ],
        compiler_params=_cparams(("parallel", "arbitrary")),
        name="even_mixer",
    )(x, norm_w, w_in, pool_w, pool_scale, conv_w, conv_b, gate_bias, mnorm)


ODD_Z = 0
ODD_XBC = ODD_Z + SSD_WIDTH
ODD_XS = ODD_XBC
ODD_B = ODD_XS + SSD_WIDTH
ODD_C = ODD_B + SSD_GROUPS * SSD_STATE
ODD_Q = ODD_C + SSD_GROUPS * SSD_STATE
ODD_K = ODD_Q + SB_WIDTH
ODD_V = ODD_K + SB_WIDTH
ODD_DT = ODD_V + SB_WIDTH
ODD_COLS = ODD_DT + LANES
SSD_XBC = SSD_WIDTH + 2 * SSD_GROUPS * SSD_STATE


def _odd_kernel(x_ref, nw_ref, win_ref, convw_ref, convb_ref, dtb_ref, alog_ref, dskip_ref,
                snorm_ref, qn_ref, kn_ref, seg_ref,
                c_ref, q_ref, k_ref, v_ref,
                halo, hstate, *, ts):
    @pl.when(pl.program_id(1) == 0)
    def _():
        halo[...] = jnp.zeros_like(halo)
        hstate[...] = jnp.zeros_like(hstate)

    h = _rmsnorm(x_ref[...], nw_ref[...]).astype(BF16)
    proj = lambda c0, n: _dot(h, win_ref[:, c0:c0 + n])

    seg = seg_ref[...]
    for src, nref, dst, scale in ((ODD_Q, qn_ref, q_ref, SB_HEAD_DIM ** -0.5), (ODD_K, kn_ref, k_ref, 1.0)):
        t = proj(src, SB_WIDTH)
        hi, lo = _split_bf16(t * t)
        ssq = _dot(hi, seg) + _dot(lo, seg)
        dst[...] = (t * lax.rsqrt(ssq * (1.0 / SB_HEAD_DIM) + EPS) * (nref[...] * scale)).astype(dst.dtype)
    v_ref[...] = proj(ODD_V, SB_WIDTH).astype(v_ref.dtype)

    raw = proj(ODD_XBC, SSD_XBC)
    conv = _causal_conv(halo, raw, convw_ref, convb_ref)
    xbc = _silu_tanh(conv)
    xs_all = xbc[:, :SSD_WIDTH]
    bm_all = xbc[:, SSD_WIDTH:SSD_WIDTH + SSD_GROUPS * SSD_STATE].astype(BF16)
    cm_all = xbc[:, SSD_WIDTH + SSD_GROUPS * SSD_STATE:].astype(BF16)

    dt_all = _softplus(proj(ODD_DT, LANES) + dtb_ref[...])
    a_all = dt_all * (-jnp.exp(alog_ref[...]))
    zg = proj(ODD_Z, SSD_WIDTH)
    zg = _silu_tanh(zg)

    causal = _row_iota((CHUNK, CHUNK)) >= _lane_iota((CHUNK, CHUNK))
    lane = _lane_iota((CHUNK, LANES))
    heads_per_group = SSD_HEADS // SSD_GROUPS
    for c in range(ts // CHUNK):
        r0, r1 = c * CHUNK, (c + 1) * CHUNK
        xs = xs_all[r0:r1, :]
        a_cum = _chunk_scan(a_all[r0:r1, :], jnp.add, 0.0)
        a_last = a_cum[CHUNK - 1:CHUNK, :]
        a_t = a_cum.T
        xdt = xs * _expand_heads(dt_all[r0:r1, :], SSD_HEADS, SSD_HEAD_DIM)
        a_wide = _expand_heads(a_cum, SSD_HEADS, SSD_HEAD_DIM)
        a_last_wide = a_wide[CHUNK - 1:CHUNK, :]
        xw = (xdt * jnp.exp(a_last_wide - a_wide)).astype(BF16)
        xdt = xdt.astype(BF16)
        from_start = jnp.exp(a_wide)
        chunk_decay = jnp.exp(a_last_wide)

        for g in range(SSD_GROUPS):
            bg = bm_all[r0:r1, g * SSD_STATE:(g + 1) * SSD_STATE]
            cg = cm_all[r0:r1, g * SSD_STATE:(g + 1) * SSD_STATE]
            gs = slice(g * SSD_GROUP_WIDTH, (g + 1) * SSD_GROUP_WIDTH)
            cb = _dot_nt(cg, bg)
            hprev = hstate[g]
            y_off = _dot(cg, hprev.astype(BF16))
            bg_t = bg.astype(F32).T.astype(BF16)
            hstate[g] = chunk_decay[:, gs] * hprev + _dot(bg_t, xw[:, gs])
            masked = []
            for hh in range(g * heads_per_group, (g + 1) * heads_per_group):
                dec = jnp.exp(jnp.where(causal, a_cum[:, hh:hh + 1] - a_t[hh:hh + 1, :], -jnp.inf))
                masked.append((cb * dec).astype(BF16))
            diag = []
            for pair in range(heads_per_group // 2):
                ps = slice(g * SSD_GROUP_WIDTH + pair * LANES, g * SSD_GROUP_WIDTH + (pair + 1) * LANES)
                outs = [_dot(masked[2 * pair + j], xdt[:, ps]) for j in range(2)]
                diag.append(jnp.where(lane < SSD_HEAD_DIM, outs[0], outs[1]))
            y = jnp.concatenate(diag, axis=1) + y_off * from_start[:, gs] + dskip_ref[:, gs] * xs[:, gs]
            c_ref[r0:r1, gs] = _rmsnorm(y * zg[r0:r1, gs], snorm_ref[:, gs]).astype(c_ref.dtype)


def _odd_mixer(x, norm_w, w_in, conv_w, conv_b, dt_bias, a_log, d_skip, snorm, qn, kn, seg, *, ts=MIXER_TS):
    bsz, s, d = x.shape
    const2 = lambda b, i: (0, 0)
    tile = lambda w: pl.BlockSpec((None, ts, w), lambda b, i: (b, i, 0))
    return pl.pallas_call(
        functools.partial(_odd_kernel, ts=ts),
        grid=(bsz, s // ts),
        in_specs=[tile(d)] + [pl.BlockSpec(a.shape, const2) for a in
                              (norm_w, w_in, conv_w, conv_b, dt_bias, a_log, d_skip, snorm, qn, kn, seg)],
        out_specs=[tile(SSD_WIDTH), tile(SB_WIDTH), tile(SB_WIDTH), tile(SB_WIDTH)],
        out_shape=[jax.ShapeDtypeStruct((bsz, s, SSD_WIDTH), BF16)]
        + [jax.ShapeDtypeStruct((bsz, s, SB_WIDTH), BF16)] * 3,
        scratch_shapes=[
            pltpu.VMEM((CONV_HALO, SSD_XBC), F32),
            pltpu.VMEM((SSD_GROUPS, SSD_STATE, SSD_GROUP_WIDTH), F32),
        ],
        compiler_params=_cparams(("parallel", "arbitrary")),
        name="odd_mixer",
    )(x, norm_w, w_in, conv_w, conv_b, dt_bias, a_log, d_skip, snorm, qn, kn, seg)


def _sb_kernel(q_ref, k_ref, v_ref, tri_ref, o_ref, acc_ref, r_ref, rmin_ref):
    step = pl.program_id(1)
    blk = SB_BLOCK
    pairs = SB_WIDTH // LANES
    first = _lane_iota((blk, LANES)) < SB_HEAD_DIM
    zero = jnp.zeros((blk, LANES), q_ref.dtype)
    q_stack = []
    for t in range(SB_TILES):
        q = q_ref[t * blk:(t + 1) * blk, :]
        for p in range(pairs):
            qp = q[:, p * LANES:(p + 1) * LANES]
            q_stack.append(jnp.concatenate([jnp.where(first, qp, zero), jnp.where(first, zero, qp)], axis=0))
    tri = tri_ref[...]
    strict = jnp.concatenate([_row_iota((blk, blk)) > _lane_iota((blk, blk))] * 2, axis=0)

    def pair_pipeline(t, p, j_lo, nblk, diag, out):
        k0 = pl.multiple_of(j_lo * blk, blk)
        cols = slice(p * LANES, (p + 1) * LANES)
        r = None if diag else r_ref[t, 2 * p * blk:2 * (p + 1) * blk, :]
        z = _dot_nt(q_stack[t * pairs + p], k_ref[pl.ds(k0, nblk * blk), cols])
        yield
        sp = _softplus(z)
        sp_near_first = [sp[:, b * blk:(b + 1) * blk] for b in reversed(range(nblk))]
        if diag:
            sp_near_first[0] = jnp.where(strict, sp_near_first[0], 0.0)
        hi, lo = _split_bf16(jnp.concatenate(sp_near_first, axis=0))
        yield
        cs = _dot(jnp.concatenate([hi, lo], axis=1), tri)
        yield
        ws = []
        for i in range(nblk):
            b = nblk - 1 - i
            cs_b = cs[2 * i * blk:2 * (i + 1) * blk]
            arg = z[:, b * blk:(b + 1) * blk] - cs_b[:, :blk]
            w_b = jnp.exp(arg if r is None else arg - r)
            if diag and i == 0:
                w_b = jnp.where(strict, w_b, 0.0)
            ws.append(w_b.astype(BF16))
            r = cs_b[:, blk:] if r is None else r + cs_b[:, blk:]
        w = jnp.concatenate(ws[::-1], axis=1)
        yield
        pv = _dot(w, v_ref[pl.ds(k0, nblk * blk), cols])
        out(t, p, jnp.where(first, pv[:blk], pv[blk:]), r)

    def process(jobs, diag):
        rmins = {t: [] for t, _, _ in jobs}

        def out(t, p, acc, r):
            cols = slice(p * LANES, (p + 1) * LANES)
            if diag:
                acc_ref[t, :, cols] = acc
            else:
                acc_ref[t, :, cols] += acc
            r_ref[t, 2 * p * blk:2 * (p + 1) * blk, :] = r
            rmins[t].append(jnp.min(r))

        _staggered([pair_pipeline(t, p, j_lo, nblk, diag, out) for t, j_lo, nblk in jobs for p in range(pairs)])
        return {t: functools.reduce(jnp.minimum, v) for t, v in rmins.items()}

    def near(nblk_of):
        jobs = [(t, step * SB_TILES + t - (nblk_of(t) - 1), nblk_of(t)) for t in range(SB_TILES)]
        for t, m in process(jobs, True).items():
            rmin_ref[t] = m

    head_steps = -(-(SB_NEAR - 1) // SB_TILES)

    @pl.when(step >= head_steps)
    def _():
        near(lambda t: SB_NEAR)

    for s0 in range(head_steps):
        @pl.when(step == s0)
        def _():
            near(lambda t: min(SB_NEAR, s0 * SB_TILES + t + 1))

    for t in range(SB_TILES):
        qi = step * SB_TILES + t

        def cond(c):
            return jnp.logical_and(c[0] >= 0, c[1] < SB_EXIT)

        def body(c, t=t):
            return c[0] - 1, process([(t, c[0], 1)], False)[t]

        lax.while_loop(cond, body, (qi - jnp.minimum(qi + 1, SB_NEAR), rmin_ref[t]))
        o_ref[t * blk:(t + 1) * blk, :] = acc_ref[t].astype(o_ref.dtype)


def _stickbreak(q, k, v, tri):
    bsz, s, w = q.shape
    rows = SB_TILES * SB_BLOCK
    return pl.pallas_call(
        _sb_kernel,
        grid=(bsz, s // rows),
        in_specs=[
            pl.BlockSpec((None, rows, w), lambda b, i: (b, i, 0)),
            pl.BlockSpec((None, s, w), lambda b, i: (b, 0, 0)),
            pl.BlockSpec((None, s, w), lambda b, i: (b, 0, 0)),
            pl.BlockSpec(tri.shape, lambda b, i: (0, 0)),
        ],
        out_specs=pl.BlockSpec((None, rows, w), lambda b, i: (b, i, 0)),
        out_shape=jax.ShapeDtypeStruct((bsz, s, w), BF16),
        scratch_shapes=[pltpu.VMEM((SB_TILES, SB_BLOCK, w), F32),
                        pltpu.VMEM((SB_TILES, 2 * (w // LANES) * SB_BLOCK, SB_BLOCK), F32),
                        pltpu.SMEM((SB_TILES,), F32)],
        compiler_params=_cparams(("parallel", "arbitrary")),
        name="stickbreak",
    )(q, k, v, tri)


def _pad_cols(w, n):
    return jnp.pad(w, ((0, 0), (0, n - w.shape[1])))


def kernel(x, l0_ffn1_norm, l0_ffn1_wg, l0_ffn1_wu, l0_ffn1_wd, l0_mix_norm, l0_w_in, l0_pool_w, l0_pool_scale, l0_qk_conv_w, l0_qk_conv_b, l0_gate_bias, l0_mlstm_norm, l0_w_out, l0_ffn2_norm, l0_ffn2_wg, l0_ffn2_wu, l0_ffn2_wd, l1_ffn1_norm, l1_ffn1_wg, l1_ffn1_wu, l1_ffn1_wd, l1_mix_norm, l1_w_in, l1_ssd_conv_w, l1_ssd_conv_b, l1_ssd_dt_bias, l1_ssd_A_log, l1_ssd_D, l1_ssd_norm, l1_sb_q_norm, l1_sb_k_norm, l1_w_out, l1_ffn2_norm, l1_ffn2_wg, l1_ffn2_wu, l1_ffn2_wd):
    bsz, s, d = x.shape
    t = bsz * s
    row = lambda a: a.reshape(1, -1).astype(F32)
    bf = lambda a: a.astype(BF16)

    def ffn(xf, mixes, norm_w, w3, next_w3=()):
        return _ffn(xf, mixes, row(norm_w), *[bf(w) for w in w3], casts=tuple(next_w3))

    xf, w_l0f2 = ffn(x.reshape(t, d), [], l0_ffn1_norm, (l0_ffn1_wg, l0_ffn1_wu, l0_ffn1_wd),
                     (l0_ffn2_wg, l0_ffn2_wu, l0_ffn2_wd))
    n_main = EVEN_GI
    w_in0 = jnp.concatenate([
        bf(l0_w_in[:, :n_main]),
        _pad_cols(bf(l0_w_in[:, n_main:n_main + MLSTM_HEADS]), LANES),
        _pad_cols(bf(l0_w_in[:, n_main + MLSTM_HEADS:]), LANES)], axis=1)
    gbias = jnp.stack([jnp.pad(l0_gate_bias[:MLSTM_HEADS], (0, LANES - MLSTM_HEADS)),
                       jnp.pad(l0_gate_bias[MLSTM_HEADS:], (0, LANES - MLSTM_HEADS))]).astype(F32)
    mix0 = _even_mixer(xf.reshape(bsz, s, d), row(l0_mix_norm), w_in0, bf(l0_pool_w),
                       row(l0_pool_scale), l0_qk_conv_w.astype(F32), row(l0_qk_conv_b), gbias,
                       row(l0_mlstm_norm))
    xf, w_l1f1 = ffn(xf, [(mix0.reshape(t, -1), bf(l0_w_out))], l0_ffn2_norm, w_l0f2,
                     (l1_ffn1_wg, l1_ffn1_wu, l1_ffn1_wd))

    xf, w_l1f2 = ffn(xf, [], l1_ffn1_norm, w_l1f1, (l1_ffn2_wg, l1_ffn2_wu, l1_ffn2_wd))
    c_dt = SSD_WIDTH + SSD_XBC
    c_q = c_dt + SSD_HEADS
    w_in1 = jnp.concatenate([bf(l1_w_in[:, :c_dt]), bf(l1_w_in[:, c_q:]),
                             _pad_cols(bf(l1_w_in[:, c_dt:c_q]), LANES)], axis=1)
    pad_heads = lambda a: jnp.pad(a.astype(F32), (0, LANES - SSD_HEADS)).reshape(1, LANES)
    seg_id = jnp.arange(SB_WIDTH) // SB_HEAD_DIM
    seg = (seg_id[:, None] == seg_id[None, :]).astype(BF16)
    c_out, qn, kn, vv = _odd_mixer(
        xf.reshape(bsz, s, d), row(l1_mix_norm), w_in1, l1_ssd_conv_w.astype(F32), row(l1_ssd_conv_b),
        pad_heads(l1_ssd_dt_bias), pad_heads(l1_ssd_A_log), row(jnp.repeat(l1_ssd_D, SSD_HEAD_DIM)),
        row(l1_ssd_norm), row(jnp.tile(l1_sb_q_norm, SB_HEADS)), row(jnp.tile(l1_sb_k_norm, SB_HEADS)), seg)
    idx = jnp.arange(SB_BLOCK)
    tri = jnp.concatenate([(idx[:, None] >= idx[None, :]).astype(BF16),
                           jnp.ones((SB_BLOCK, SB_BLOCK), BF16)], axis=1)
    tri = jnp.concatenate([tri, tri], axis=0)
    d_out = _stickbreak(qn, kn, vv, tri)
    xf, _ = ffn(xf, [(c_out.reshape(t, -1), bf(l1_w_out[:SSD_WIDTH])), (d_out.reshape(t, -1), bf(l1_w_out[SSD_WIDTH:]))],
                l1_ffn2_norm, w_l1f2)
    return xf.reshape(bsz, s, d)
```

```python
import functools

import jax
import jax.numpy as jnp
from jax import lax
from jax.experimental import pallas as pl
from jax.experimental.pallas import tpu as pltpu

F32 = jnp.float32
BF16 = jnp.bfloat16

EPS = 1e-6
FFN_RES = 0.5
SHORT_CONV = 4
LANES = 128
BF16_SUBLANES = 16

POOL_WINDOWS = (2, 4, 8, 16)
POOL_WIDTH = 512
MLSTM_HEADS = 4
MLSTM_HEAD_DIM = 128
MLSTM_WIDTH = MLSTM_HEADS * MLSTM_HEAD_DIM
CHUNK = 128

SSD_HEADS = 16
SSD_HEAD_DIM = 64
SSD_WIDTH = SSD_HEADS * SSD_HEAD_DIM
SSD_GROUPS = 4
SSD_STATE = 128
SSD_GROUP_WIDTH = SSD_WIDTH // SSD_GROUPS
SB_HEADS = 8
SB_HEAD_DIM = 64
SB_WIDTH = SB_HEADS * SB_HEAD_DIM
SB_BLOCK = 128
SB_EXIT = 104.0
SB_TILES = 4
SB_NEAR = 3

VMEM_LIMIT_BYTES = 56 * 1024 * 1024

FFN_TM = 512
FFN_CHUNK = 512
MIXER_TS = 512


def _cparams(semantics):
    return pltpu.CompilerParams(dimension_semantics=semantics, vmem_limit_bytes=VMEM_LIMIT_BYTES)


def _rmsnorm(x, w):
    return x * lax.rsqrt(jnp.mean(x * x, axis=-1, keepdims=True) + EPS) * w


def _sigmoid(x):
    return 1.0 / (1.0 + jnp.exp(-x))


def _sigmoid_tanh(x):
    return 0.5 + 0.5 * jnp.tanh(0.5 * x)


def _silu_tanh(x):
    h = 0.5 * x
    return h + h * jnp.tanh(h)


def _softplus(x):
    return jnp.maximum(x, 0.0) + jnp.log(1.0 + jnp.exp(-jnp.abs(x)))


def _dot(a, b):
    return jnp.dot(a, b, preferred_element_type=F32)


def _dot_nt(a, b):
    return lax.dot_general(a, b, (((1,), (1,)), ((), ())), preferred_element_type=F32)


def _split_bf16(x):
    hi = x.astype(BF16)
    lo = (x - hi.astype(F32)).astype(BF16)
    return hi, lo


def _row_iota(shape):
    return lax.broadcasted_iota(jnp.int32, shape, 0)


def _lane_iota(shape):
    return lax.broadcasted_iota(jnp.int32, shape, 1)


def _chunk_scan(x, op, fill):
    rows = _row_iota(x.shape)
    sh = 1
    while sh < x.shape[0]:
        shifted = jnp.where(rows >= sh, pltpu.roll(x, sh, 0), fill)
        x = op(x, shifted)
        sh *= 2
    return x


def _causal_conv(halo_ref, raw, w_ref, b_ref):
    assert w_ref.shape[0] == SHORT_CONV == 4
    ts = raw.shape[0]
    xe = jnp.concatenate([halo_ref[...], raw], axis=0)
    halo_ref[...] = raw[ts - CONV_HALO:, :]
    d1 = pltpu.roll(xe, 1, 0)
    pair = xe * w_ref[1:2, :] + d1 * w_ref[0:1, :]
    out = xe * w_ref[3:4, :] + d1 * w_ref[2:3, :] + pltpu.roll(pair, 2, 0)
    return out[CONV_HALO:, :] + b_ref[...]


def _staggered(gens):
    pending, live = list(gens), []
    while pending or live:
        if pending:
            live.append(pending.pop(0))
        for g in list(live):
            try:
                next(g)
            except StopIteration:
                live.remove(g)


def _expand_heads(slab, n_heads, head_dim):
    per = LANES // head_dim
    m = slab.shape[0]
    lane = _lane_iota((m, LANES))
    pieces = []
    for p in range(n_heads // per):
        piece = jnp.broadcast_to(slab[:, p * per:p * per + 1], (m, LANES))
        for j in range(1, per):
            piece = jnp.where(lane >= j * head_dim, slab[:, p * per + j:p * per + j + 1], piece)
        pieces.append(piece)
    return jnp.concatenate(pieces, axis=1)


def _ffn_kernel(*refs, n_mix, n_cast, ff_chunk):
    x_ref = refs[0]
    mix_refs = refs[1:1 + 2 * n_mix]
    nw_ref, wg_ref, wu_ref, wd_ref = refs[1 + 2 * n_mix:5 + 2 * n_mix]
    cast_in = refs[5 + 2 * n_mix:5 + 2 * n_mix + n_cast]
    o_ref = refs[5 + 2 * n_mix + n_cast]
    cast_out = refs[6 + 2 * n_mix + n_cast:]
    for src, dst in zip(cast_in, cast_out):
        dst[...] = src[...].astype(dst.dtype)
    x = x_ref[...]
    for i in range(n_mix):
        x = x + _dot(mix_refs[2 * i][...], mix_refs[2 * i + 1][...])
    h = _rmsnorm(x, nw_ref[...]).astype(BF16)
    d_ff = wg_ref.shape[1]
    y = None
    for c0 in range(0, d_ff, ff_chunk):
        c1 = min(c0 + ff_chunk, d_ff)
        g = _dot(h, wg_ref[:, c0:c1])
        u = _dot(h, wu_ref[:, c0:c1])
        a = (g * _sigmoid(g) * u).astype(BF16)
        part = _dot(a, wd_ref[c0:c1, :])
        y = part if y is None else y + part
    o_ref[...] = x + FFN_RES * y


def _cast_block_rows(n_rows, n_steps):
    rb = BF16_SUBLANES * pl.cdiv(pl.cdiv(n_rows, n_steps), BF16_SUBLANES)
    while n_rows % rb:
        rb += BF16_SUBLANES
    return rb


def _ffn(x, mixes, norm_w, wg, wu, wd, casts=(), *, tm=FFN_TM, ff_chunk=FFN_CHUNK):
    t, d = x.shape
    d_ff = wg.shape[1]
    n_steps = t // tm
    const = lambda i: (0, 0)
    in_specs = [pl.BlockSpec((tm, d), lambda i: (i, 0))]
    args = [x]
    for mix, w_out in mixes:
        in_specs += [pl.BlockSpec((tm, mix.shape[1]), lambda i: (i, 0)),
                     pl.BlockSpec(w_out.shape, const)]
        args += [mix, w_out]
    in_specs += [pl.BlockSpec((1, d), const), pl.BlockSpec((d, d_ff), const),
                 pl.BlockSpec((d, d_ff), const), pl.BlockSpec((d_ff, d), const)]
    args += [norm_w, wg, wu, wd]
    out_specs = [pl.BlockSpec((tm, d), lambda i: (i, 0))]
    out_shape = [jax.ShapeDtypeStruct((t, d), F32)]
    for w in casts:
        rb = _cast_block_rows(w.shape[0], n_steps)
        spec = pl.BlockSpec((rb, w.shape[1]), lambda i, last=w.shape[0] // rb - 1: (jnp.minimum(i, last), 0))
        in_specs.append(spec)
        args.append(w)
        out_specs.append(spec)
        out_shape.append(jax.ShapeDtypeStruct(w.shape, BF16))
    outs = pl.pallas_call(
        functools.partial(_ffn_kernel, n_mix=len(mixes), n_cast=len(casts), ff_chunk=ff_chunk),
        grid=(n_steps,),
        in_specs=in_specs,
        out_specs=out_specs,
        out_shape=out_shape,
        compiler_params=_cparams(("arbitrary",)),
        name="ffn",
    )(*args)
    return outs[0], outs[1:]


EVEN_U = 0
EVEN_QK = EVEN_U + POOL_WIDTH
EVEN_V = EVEN_QK + 2 * MLSTM_WIDTH
EVEN_O = EVEN_V + MLSTM_WIDTH
EVEN_GI = EVEN_O + MLSTM_WIDTH
EVEN_GF = EVEN_GI + LANES
EVEN_COLS = EVEN_GF + LANES
POOL_HALO = 16
CONV_HALO = 8


def _even_kernel(x_ref, nw_ref, win_ref, poolw_ref, pscale_ref, convw_ref, convb_ref,
                 gbias_ref, mnorm_ref, o_ref,
                 halo_u, halo_qk, state, m_state, *, ts):
    s_idx = pl.program_id(1)

    @pl.when(s_idx == 0)
    def _():
        halo_u[...] = jnp.zeros_like(halo_u)
        halo_qk[...] = jnp.zeros_like(halo_qk)
        state[...] = jnp.zeros_like(state)
        m_state[...] = jnp.zeros_like(m_state)

    h = _rmsnorm(x_ref[...], nw_ref[...]).astype(BF16)
    full_proj = _dot(h, win_ref[...])
    proj = lambda c0, n: full_proj[:, c0:c0 + n]

    u = proj(EVEN_U, POOL_WIDTH)
    ue = jnp.concatenate([halo_u[...], u], axis=0)
    halo_u[...] = u[ts - POOL_HALO:, :]
    pos = (s_idx * ts + 1 + _row_iota((ts, 1))).astype(F32)
    for g, win in enumerate(POOL_WINDOWS):
        acc = ue[:, g * LANES:(g + 1) * LANES]
        sh = 1
        while sh < win:
            acc = acc + pltpu.roll(acc, sh, 0)
            sh *= 2
        win_sum = acc[POOL_HALO:, :]
        pooled = win_sum / jnp.minimum(pos, float(win)) - u[:, g * LANES:(g + 1) * LANES]
        mixed = _dot(pooled.astype(BF16), poolw_ref[g])
        o_ref[:, g * LANES:(g + 1) * LANES] = (
            mixed * pscale_ref[:, g * LANES:(g + 1) * LANES]).astype(o_ref.dtype)

    qk_raw = proj(EVEN_QK, 2 * MLSTM_WIDTH)
    conv = _causal_conv(halo_qk, qk_raw, convw_ref, convb_ref)
    qk = _silu_tanh(conv)
    q_all = qk[:, :MLSTM_WIDTH].astype(BF16)
    k_all = qk[:, MLSTM_WIDTH:] * (MLSTM_HEAD_DIM ** -0.5)
    v_all = proj(EVEN_V, MLSTM_WIDTH).astype(BF16)
    o_gate = _sigmoid_tanh(proj(EVEN_O, MLSTM_WIDTH))

    g_i = proj(EVEN_GI, LANES) + gbias_ref[0:1, :]
    g_f = -_softplus(-(proj(EVEN_GF, LANES) + gbias_ref[1:2, :]))

    ones_blk = jnp.ones((CHUNK, MLSTM_HEAD_DIM), BF16)
    causal = _row_iota((CHUNK, CHUNK)) >= _lane_iota((CHUNK, CHUNK))
    for c in range(ts // CHUNK):
        r0, r1 = c * CHUNK, (c + 1) * CHUNK
        li = g_i[r0:r1, :]
        b = _chunk_scan(g_f[r0:r1, :], jnp.add, 0.0)
        a = li - b
        m_prev = m_state[...]
        big_m = jnp.maximum(m_prev, _chunk_scan(a, jnp.maximum, -jnp.inf))
        b_last = b[CHUNK - 1:CHUNK, :]
        m_new = b_last + big_m[CHUNK - 1:CHUNK, :]
        w_inter = jnp.exp(m_prev - big_m)
        e_negm = jnp.exp(-(b + big_m))
        w_state = jnp.exp(b_last + a - m_new)
        decay = jnp.exp(b_last + m_prev - m_new)
        m_state[...] = m_new
        a_t = a.T
        def head_pipeline(hd):
            c0, c1 = hd * MLSTM_HEAD_DIM, (hd + 1) * MLSTM_HEAD_DIM
            qc = q_all[r0:r1, c0:c1]
            kc = k_all[r0:r1, c0:c1]
            v_ext = jnp.concatenate([v_all[r0:r1, c0:c1], ones_blk], axis=1)
            st = state[hd]
            s_qk = _dot_nt(qc, kc.astype(BF16))
            inter = _dot(qc, st.astype(BF16))
            yield
            w_intra = jnp.exp(jnp.where(causal, a_t[hd:hd + 1, :] - big_m[:, hd:hd + 1], -jnp.inf))
            p = (s_qk * w_intra).astype(BF16)
            kw_t = (kc * w_state[:, hd:hd + 1]).T.astype(BF16)
            yield
            pv = _dot(p, v_ext)
            state[hd] = decay[:, hd:hd + 1] * st + _dot(kw_t, v_ext)
            yield
            numden = w_inter[:, hd:hd + 1] * inter + pv
            den = jnp.maximum(jnp.abs(numden[:, MLSTM_HEAD_DIM:MLSTM_HEAD_DIM + 1]),
                              e_negm[:, hd:hd + 1])
            hh = numden[:, :MLSTM_HEAD_DIM] / den
            hn = _rmsnorm(hh, mnorm_ref[:, c0:c1])
            o_ref[r0:r1, POOL_WIDTH + c0:POOL_WIDTH + c1] = (
                o_gate[r0:r1, c0:c1] * hn).astype(o_ref.dtype)

        _staggered([head_pipeline(hd) for hd in range(MLSTM_HEADS)])


def _even_mixer(x, norm_w, w_in, pool_w, pool_scale, conv_w, conv_b, gate_bias, mnorm, *, ts=MIXER_TS):
    bsz, s, d = x.shape
    const2 = lambda b, i: (0, 0)
    return pl.pallas_call(
        functools.partial(_even_kernel, ts=ts),
        grid=(bsz, s // ts),
        in_specs=[
            pl.BlockSpec((None, ts, d), lambda b, i: (b, i, 0)),
            pl.BlockSpec((1, d), const2),
            pl.BlockSpec(w_in.shape, const2),
            pl.BlockSpec(pool_w.shape, lambda b, i: (0, 0, 0)),
            pl.BlockSpec(pool_scale.shape, const2),
            pl.BlockSpec(conv_w.shape, const2),
            pl.BlockSpec(conv_b.shape, const2),
            pl.BlockSpec(gate_bias.shape, const2),
            pl.BlockSpec(mnorm.shape, const2),
        ],
        out_specs=pl.BlockSpec((None, ts, POOL_WIDTH + MLSTM_WIDTH), lambda b, i: (b, i, 0)),
        out_shape=jax.ShapeDtypeStruct((bsz, s, POOL_WIDTH + MLSTM_WIDTH), BF16),
        scratch_shapes=[
            pltpu.VMEM((POOL_HALO, POOL_WIDTH), F32),
            pltpu.VMEM((CONV_HALO, 2 * MLSTM_WIDTH), F32),
            pltpu.VMEM((MLSTM_HEADS, MLSTM_HEAD_DIM, 2 * MLSTM_HEAD_DIM), F32),
            pltpu.VMEM((1, LANES), F32),
        ],
        compiler_params=_cparams(("parallel", "arbitrary")),
        name="even_mixer",
    )(x, norm_w, w_in, pool_w, pool_scale, conv_w, conv_b, gate_bias, mnorm)


ODD_Z = 0
ODD_XBC = ODD_Z + SSD_WIDTH
ODD_XS = ODD_XBC
ODD_B = ODD_XS + SSD_WIDTH
ODD_C = ODD_B + SSD_GROUPS * SSD_STATE
ODD_Q = ODD_C + SSD_GROUPS * SSD_STATE
ODD_K = ODD_Q + SB_WIDTH
ODD_V = ODD_K + SB_WIDTH
ODD_DT = ODD_V + SB_WIDTH
ODD_COLS = ODD_DT + LANES
SSD_XBC = SSD_WIDTH + 2 * SSD_GROUPS * SSD_STATE


def _odd_kernel(x_ref, nw_ref, win_ref, convw_ref, convb_ref, dtb_ref, alog_ref, dskip_ref,
                snorm_ref, qn_ref, kn_ref, seg_ref,
                c_ref, q_ref, k_ref, v_ref,
                halo, hstate, *, ts):
    @pl.when(pl.program_id(1) == 0)
    def _():
        halo[...] = jnp.zeros_like(halo)
        hstate[...] = jnp.zeros_like(hstate)

    h = _rmsnorm(x_ref[...], nw_ref[...]).astype(BF16)
    proj = lambda c0, n: _dot(h, win_ref[:, c0:c0 + n])

    seg = seg_ref[...]
    for src, nref, dst, scale in ((ODD_Q, qn_ref, q_ref, SB_HEAD_DIM ** -0.5), (ODD_K, kn_ref, k_ref, 1.0)):
        t = proj(src, SB_WIDTH)
        hi, lo = _split_bf16(t * t)
        ssq = _dot(hi, seg) + _dot(lo, seg)
        dst[...] = (t * lax.rsqrt(ssq * (1.0 / SB_HEAD_DIM) + EPS) * (nref[...] * scale)).astype(dst.dtype)
    v_ref[...] = proj(ODD_V, SB_WIDTH).astype(v_ref.dtype)

    raw = proj(ODD_XBC, SSD_XBC)
    conv = _causal_conv(halo, raw, convw_ref, convb_ref)
    xbc = _silu_tanh(conv)
    xs_all = xbc[:, :SSD_WIDTH]
    bm_all = xbc[:, SSD_WIDTH:SSD_WIDTH + SSD_GROUPS * SSD_STATE].astype(BF16)
    cm_all = xbc[:, SSD_WIDTH + SSD_GROUPS * SSD_STATE:].astype(BF16)

    dt_all = _softplus(proj(ODD_DT, LANES) + dtb_ref[...])
    a_all = dt_all * (-jnp.exp(alog_ref[...]))
    zg = proj(ODD_Z, SSD_WIDTH)
    zg = _silu_tanh(zg)

    causal = _row_iota((CHUNK, CHUNK)) >= _lane_iota((CHUNK, CHUNK))
    lane = _lane_iota((CHUNK, LANES))
    heads_per_group = SSD_HEADS // SSD_GROUPS
    for c in range(ts // CHUNK):
        r0, r1 = c * CHUNK, (c + 1) * CHUNK
        xs = xs_all[r0:r1, :]
        a_cum = _chunk_scan(a_all[r0:r1, :], jnp.add, 0.0)
        a_last = a_cum[CHUNK - 1:CHUNK, :]
        a_t = a_cum.T
        xdt = xs * _expand_heads(dt_all[r0:r1, :], SSD_HEADS, SSD_HEAD_DIM)
        a_wide = _expand_heads(a_cum, SSD_HEADS, SSD_HEAD_DIM)
        a_last_wide = a_wide[CHUNK - 1:CHUNK, :]
        xw = (xdt * jnp.exp(a_last_wide - a_wide)).astype(BF16)
        xdt = xdt.astype(BF16)
        from_start = jnp.exp(a_wide)
        chunk_decay = jnp.exp(a_last_wide)

        for g in range(SSD_GROUPS):
            bg = bm_all[r0:r1, g * SSD_STATE:(g + 1) * SSD_STATE]
            cg = cm_all[r0:r1, g * SSD_STATE:(g + 1) * SSD_STATE]
            gs = slice(g * SSD_GROUP_WIDTH, (g + 1) * SSD_GROUP_WIDTH)
            cb = _dot_nt(cg, bg)
            hprev = hstate[g]
            y_off = _dot(cg, hprev.astype(BF16))
            bg_t = bg.astype(F32).T.astype(BF16)
            hstate[g] = chunk_decay[:, gs] * hprev + _dot(bg_t, xw[:, gs])
            masked = []
            for hh in range(g * heads_per_group, (g + 1) * heads_per_group):
                dec = jnp.exp(jnp.where(causal, a_cum[:, hh:hh + 1] - a_t[hh:hh + 1, :], -jnp.inf))
                masked.append((cb * dec).astype(BF16))
            diag = []
            for pair in range(heads_per_group // 2):
                ps = slice(g * SSD_GROUP_WIDTH + pair * LANES, g * SSD_GROUP_WIDTH + (pair + 1) * LANES)
                outs = [_dot(masked[2 * pair + j], xdt[:, ps]) for j in range(2)]
                diag.append(jnp.where(lane < SSD_HEAD_DIM, outs[0], outs[1]))
            y = jnp.concatenate(diag, axis=1) + y_off * from_start[:, gs] + dskip_ref[:, gs] * xs[:, gs]
            c_ref[r0:r1, gs] = _rmsnorm(y * zg[r0:r1, gs], snorm_ref[:, gs]).astype(c_ref.dtype)


def _odd_mixer(x, norm_w, w_in, conv_w, conv_b, dt_bias, a_log, d_skip, snorm, qn, kn, seg, *, ts=MIXER_TS):
    bsz, s, d = x.shape
    const2 = lambda b, i: (0, 0)
    tile = lambda w: pl.BlockSpec((None, ts, w), lambda b, i: (b, i, 0))
    return pl.pallas_call(
        functools.partial(_odd_kernel, ts=ts),
        grid=(bsz, s // ts),
        in_specs=[tile(d)] + [pl.BlockSpec(a.shape, const2) for a in
                              (norm_w, w_in, conv_w, conv_b, dt_bias, a_log, d_skip, snorm, qn, kn, seg)],
        out_specs=[tile(SSD_WIDTH), tile(SB_WIDTH), tile(SB_WIDTH), tile(SB_WIDTH)],
        out_shape=[jax.ShapeDtypeStruct((bsz, s, SSD_WIDTH), BF16)]
        + [jax.ShapeDtypeStruct((bsz, s, SB_WIDTH), BF16)] * 3,
        scratch_shapes=[
            pltpu.VMEM((CONV_HALO, SSD_XBC), F32),
            pltpu.VMEM((SSD_GROUPS, SSD_STATE, SSD_GROUP_WIDTH), F32),
        ],
        compiler_params=_cparams(("parallel", "arbitrary")),
        name="odd_mixer",
    )(x, norm_w, w_in, conv_w, conv_b, dt_bias, a_log, d_skip, snorm, qn, kn, seg)


def _sb_kernel(q_ref, k_ref, v_ref, tri_ref, o_ref, acc_ref, r_ref, rmin_ref):
    step = pl.program_id(1)
    blk = SB_BLOCK
    pairs = SB_WIDTH // LANES
    first = _lane_iota((blk, LANES)) < SB_HEAD_DIM
    zero = jnp.zeros((blk, LANES), q_ref.dtype)
    q_stack = []
    for t in range(SB_TILES):
        q = q_ref[t * blk:(t + 1) * blk, :]
        for p in range(pairs):
            qp = q[:, p * LANES:(p + 1) * LANES]
            q_stack.append(jnp.concatenate([jnp.where(first, qp, zero), jnp.where(first, zero, qp)], axis=0))
    tri = tri_ref[...]
    strict = jnp.concatenate([_row_iota((blk, blk)) > _lane_iota((blk, blk))] * 2, axis=0)

    def pair_pipeline(t, p, j_lo, nblk, diag, out):
        k0 = pl.multiple_of(j_lo * blk, blk)
        cols = slice(p * LANES, (p + 1) * LANES)
        r = None if diag else r_ref[t, 2 * p * blk:2 * (p + 1) * blk, :]
        z = _dot_nt(q_stack[t * pairs + p], k_ref[pl.ds(k0, nblk * blk), cols])
        yield
        sp = _softplus(z)
        sp_near_first = [sp[:, b * blk:(b + 1) * blk] for b in reversed(range(nblk))]
        if diag:
            sp_near_first[0] = jnp.where(strict, sp_near_first[0], 0.0)
        hi, lo = _split_bf16(jnp.concatenate(sp_near_first, axis=0))
        yield
        cs = _dot(jnp.concatenate([hi, lo], axis=1), tri)
        yield
        ws = []
        for i in range(nblk):
            b = nblk - 1 - i
            cs_b = cs[2 * i * blk:2 * (i + 1) * blk]
            arg = z[:, b * blk:(b + 1) * blk] - cs_b[:, :blk]
            w_b = jnp.exp(arg if r is None else arg - r)
            if diag and i == 0:
                w_b = jnp.where(strict, w_b, 0.0)
            ws.append(w_b.astype(BF16))
            r = cs_b[:, blk:] if r is None else r + cs_b[:, blk:]
        w = jnp.concatenate(ws[::-1], axis=1)
        yield
        pv = _dot(w, v_ref[pl.ds(k0, nblk * blk), cols])
        out(t, p, jnp.where(first, pv[:blk], pv[blk:]), r)

    def process(jobs, diag):
        rmins = {t: [] for t, _, _ in jobs}

        def out(t, p, acc, r):
            cols = slice(p * LANES, (p + 1) * LANES)
            if diag:
                acc_ref[t, :, cols] = acc
            else:
                acc_ref[t, :, cols] += acc
            r_ref[t, 2 * p * blk:2 * (p + 1) * blk, :] = r
            rmins[t].append(jnp.min(r))

        _staggered([pair_pipeline(t, p, j_lo, nblk, diag, out) for t, j_lo, nblk in jobs for p in range(pairs)])
        return {t: functools.reduce(jnp.minimum, v) for t, v in rmins.items()}

    def near(nblk_of):
        jobs = [(t, step * SB_TILES + t - (nblk_of(t) - 1), nblk_of(t)) for t in range(SB_TILES)]
        for t, m in process(jobs, True).items():
            rmin_ref[t] = m

    head_steps = -(-(SB_NEAR - 1) // SB_TILES)

    @pl.when(step >= head_steps)
    def _():
        near(lambda t: SB_NEAR)

    for s0 in range(head_steps):
        @pl.when(step == s0)
        def _():
            near(lambda t: min(SB_NEAR, s0 * SB_TILES + t + 1))

    for t in range(SB_TILES):
        qi = step * SB_TILES + t

        def cond(c):
            return jnp.logical_and(c[0] >= 0, c[1] < SB_EXIT)

        def body(c, t=t):
            return c[0] - 1, process([(t, c[0], 1)], False)[t]

        lax.while_loop(cond, body, (qi - jnp.minimum(qi + 1, SB_NEAR), rmin_ref[t]))
        o_ref[t * blk:(t + 1) * blk, :] = acc_ref[t].astype(o_ref.dtype)


def _stickbreak(q, k, v, tri):
    bsz, s, w = q.shape
    rows = SB_TILES * SB_BLOCK
    return pl.pallas_call(
        _sb_kernel,
        grid=(bsz, s // rows),
        in_specs=[
            pl.BlockSpec((None, rows, w), lambda b, i: (b, i, 0)),
            pl.BlockSpec((None, s, w), lambda b, i: (b, 0, 0)),
            pl.BlockSpec((None, s, w), lambda b, i: (b, 0, 0)),
            pl.BlockSpec(tri.shape, lambda b, i: (0, 0)),
        ],
        out_specs=pl.BlockSpec((None, rows, w), lambda b, i: (b, i, 0)),
        out_shape=jax.ShapeDtypeStruct((bsz, s, w), BF16),
        scratch_shapes=[pltpu.VMEM((SB_TILES, SB_BLOCK, w), F32),
                        pltpu.VMEM((SB_TILES, 2 * (w // LANES) * SB_BLOCK, SB_BLOCK), F32),
                        pltpu.SMEM((SB_TILES,), F32)],
        compiler_params=_cparams(("parallel", "arbitrary")),
        name="stickbreak",
    )(q, k, v, tri)


def _pad_cols(w, n):
    return jnp.pad(w, ((0, 0), (0, n - w.shape[1])))


def kernel(x, l0_ffn1_norm, l0_ffn1_wg, l0_ffn1_wu, l0_ffn1_wd, l0_mix_norm, l0_w_in, l0_pool_w, l0_pool_scale, l0_qk_conv_w, l0_qk_conv_b, l0_gate_bias, l0_mlstm_norm, l0_w_out, l0_ffn2_norm, l0_ffn2_wg, l0_ffn2_wu, l0_ffn2_wd, l1_ffn1_norm, l1_ffn1_wg, l1_ffn1_wu, l1_ffn1_wd, l1_mix_norm, l1_w_in, l1_ssd_conv_w, l1_ssd_conv_b, l1_ssd_dt_bias, l1_ssd_A_log, l1_ssd_D, l1_ssd_norm, l1_sb_q_norm, l1_sb_k_norm, l1_w_out, l1_ffn2_norm, l1_ffn2_wg, l1_ffn2_wu, l1_ffn2_wd):
    bsz, s, d = x.shape
    t = bsz * s
    row = lambda a: a.reshape(1, -1).astype(F32)
    bf = lambda a: a.astype(BF16)

    def ffn(xf, mixes, norm_w, w3, next_w3=()):
        return _ffn(xf, mixes, row(norm_w), *[bf(w) for w in w3], casts=tuple(next_w3))

    xf, w_l0f2 = ffn(x.reshape(t, d), [], l0_ffn1_norm, (l0_ffn1_wg, l0_ffn1_wu, l0_ffn1_wd),
                     (l0_ffn2_wg, l0_ffn2_wu, l0_ffn2_wd))
    n_main = EVEN_GI
    w_in0 = jnp.concatenate([
        bf(l0_w_in[:, :n_main]),
        _pad_cols(bf(l0_w_in[:, n_main:n_main + MLSTM_HEADS]), LANES),
        _pad_cols(bf(l0_w_in[:, n_main + MLSTM_HEADS:]), LANES)], axis=1)
    gbias = jnp.stack([jnp.pad(l0_gate_bias[:MLSTM_HEADS], (0, LANES - MLSTM_HEADS)),
                       jnp.pad(l0_gate_bias[MLSTM_HEADS:], (0, LANES - MLSTM_HEADS))]).astype(F32)
    mix0 = _even_mixer(xf.reshape(bsz, s, d), row(l0_mix_norm), w_in0, bf(l0_pool_w),
                       row(l0_pool_scale), l0_qk_conv_w.astype(F32), row(l0_qk_conv_b), gbias,
                       row(l0_mlstm_norm))
    xf, w_l1f1 = ffn(xf, [(mix0.reshape(t, -1), bf(l0_w_out))], l0_ffn2_norm, w_l0f2,
                     (l1_ffn1_wg, l1_ffn1_wu, l1_ffn1_wd))

    xf, w_l1f2 = ffn(xf, [], l1_ffn1_norm, w_l1f1, (l1_ffn2_wg, l1_ffn2_wu, l1_ffn2_wd))
    c_dt = SSD_WIDTH + SSD_XBC
    c_q = c_dt + SSD_HEADS
    w_in1 = jnp.concatenate([bf(l1_w_in[:, :c_dt]), bf(l1_w_in[:, c_q:]),
                             _pad_cols(bf(l1_w_in[:, c_dt:c_q]), LANES)], axis=1)
    pad_heads = lambda a: jnp.pad(a.astype(F32), (0, LANES - SSD_HEADS)).reshape(1, LANES)
    seg_id = jnp.arange(SB_WIDTH) // SB_HEAD_DIM
    seg = (seg_id[:, None] == seg_id[None, :]).astype(BF16)
    c_out, qn, kn, vv = _odd_mixer(
        xf.reshape(bsz, s, d), row(l1_mix_norm), w_in1, l1_ssd_conv_w.astype(F32), row(l1_ssd_conv_b),
        pad_heads(l1_ssd_dt_bias), pad_heads(l1_ssd_A_log), row(jnp.repeat(l1_ssd_D, SSD_HEAD_DIM)),
        row(l1_ssd_norm), row(jnp.tile(l1_sb_q_norm, SB_HEADS)), row(jnp.tile(l1_sb_k_norm, SB_HEADS)), seg)
    idx = jnp.arange(SB_BLOCK)
    tri = jnp.concatenate([(idx[:, None] >= idx[None, :]).astype(BF16),
                           jnp.ones((SB_BLOCK, SB_BLOCK), BF16)], axis=1)
    tri = jnp.concatenate([tri, tri], axis=0)
    d_out = _stickbreak(qn, kn, vv, tri)
    xf, _ = ffn(xf, [(c_out.reshape(t, -1), bf(l1_w_out[:SSD_WIDTH])), (d_out.reshape(t, -1), bf(l1_w_out[SSD_WIDTH:]))],
                l1_ffn2_norm, w_l1f2)
    return xf.reshape(bsz, s, d)
```

```python
import functools

import jax
import jax.numpy as jnp
from jax import lax
from jax.experimental import pallas as pl
from jax.experimental.pallas import tpu as pltpu

F32 = jnp.float32
BF16 = jnp.bfloat16

EPS = 1e-6
FFN_RES = 0.5
SHORT_CONV = 4
LANES = 128
BF16_SUBLANES = 16

POOL_WINDOWS = (2, 4, 8, 16)
POOL_WIDTH = 512
MLSTM_HEADS = 4
MLSTM_HEAD_DIM = 128
MLSTM_WIDTH = MLSTM_HEADS * MLSTM_HEAD_DIM
CHUNK = 128

SSD_HEADS = 16
SSD_HEAD_DIM = 64
SSD_WIDTH = SSD_HEADS * SSD_HEAD_DIM
SSD_GROUPS = 4
SSD_STATE = 128
SSD_GROUP_WIDTH = SSD_WIDTH // SSD_GROUPS
SB_HEADS = 8
SB_HEAD_DIM = 64
SB_WIDTH = SB_HEADS * SB_HEAD_DIM
SB_BLOCK = 128
SB_EXIT = 104.0
SB_QROWS = 64
SB_UNITS = 8
SB_NEAR = 3

VMEM_LIMIT_BYTES = 56 * 1024 * 1024

FFN_TM = 512
FFN_CHUNK = 512
MIXER_TS = 512


def _cparams(semantics):
    return pltpu.CompilerParams(dimension_semantics=semantics, vmem_limit_bytes=VMEM_LIMIT_BYTES)


def _rmsnorm(x, w):
    return x * lax.rsqrt(jnp.mean(x * x, axis=-1, keepdims=True) + EPS) * w


def _sigmoid(x):
    return 1.0 / (1.0 + jnp.exp(-x))


def _sigmoid_tanh(x):
    return 0.5 + 0.5 * jnp.tanh(0.5 * x)


def _silu_tanh(x):
    h = 0.5 * x
    return h + h * jnp.tanh(h)


def _softplus(x):
    return jnp.maximum(x, 0.0) + jnp.log(1.0 + jnp.exp(-jnp.abs(x)))


def _dot(a, b):
    return jnp.dot(a, b, preferred_element_type=F32)


def _dot_nt(a, b):
    return lax.dot_general(a, b, (((1,), (1,)), ((), ())), preferred_element_type=F32)


def _split_bf16(x):
    hi = x.astype(BF16)
    lo = (x - hi.astype(F32)).astype(BF16)
    return hi, lo


def _row_iota(shape):
    return lax.broadcasted_iota(jnp.int32, shape, 0)


def _lane_iota(shape):
    return lax.broadcasted_iota(jnp.int32, shape, 1)


def _chunk_scan(x, op, fill):
    rows = _row_iota(x.shape)
    sh = 1
    while sh < x.shape[0]:
        shifted = jnp.where(rows >= sh, pltpu.roll(x, sh, 0), fill)
        x = op(x, shifted)
        sh *= 2
    return x


def _causal_conv(halo_ref, raw, w_ref, b_ref):
    assert w_ref.shape[0] == SHORT_CONV == 4
    ts = raw.shape[0]
    xe = jnp.concatenate([halo_ref[...], raw], axis=0)
    halo_ref[...] = raw[ts - CONV_HALO:, :]
    d1 = pltpu.roll(xe, 1, 0)
    pair = xe * w_ref[1:2, :] + d1 * w_ref[0:1, :]
    out = xe * w_ref[3:4, :] + d1 * w_ref[2:3, :] + pltpu.roll(pair, 2, 0)
    return out[CONV_HALO:, :] + b_ref[...]


def _staggered(gens):
    pending, live = list(gens), []
    while pending or live:
        if pending:
            live.append(pending.pop(0))
        for g in list(live):
            try:
                next(g)
            except StopIteration:
                live.remove(g)


def _expand_heads(slab, n_heads, head_dim):
    per = LANES // head_dim
    m = slab.shape[0]
    lane = _lane_iota((m, LANES))
    pieces = []
    for p in range(n_heads // per):
        piece = jnp.broadcast_to(slab[:, p * per:p * per + 1], (m, LANES))
        for j in range(1, per):
            piece = jnp.where(lane >= j * head_dim, slab[:, p * per + j:p * per + j + 1], piece)
        pieces.append(piece)
    return jnp.concatenate(pieces, axis=1)


def _ffn_kernel(*refs, n_mix, n_cast, ff_chunk):
    x_ref = refs[0]
    mix_refs = refs[1:1 + 2 * n_mix]
    nw_ref, wg_ref, wu_ref, wd_ref = refs[1 + 2 * n_mix:5 + 2 * n_mix]
    cast_in = refs[5 + 2 * n_mix:5 + 2 * n_mix + n_cast]
    o_ref = refs[5 + 2 * n_mix + n_cast]
    cast_out = refs[6 + 2 * n_mix + n_cast:]
    for src, dst in zip(cast_in, cast_out):
        dst[...] = src[...].astype(dst.dtype)
    x = x_ref[...]
    for i in range(n_mix):
        x = x + _dot(mix_refs[2 * i][...], mix_refs[2 * i + 1][...])
    h = _rmsnorm(x, nw_ref[...]).astype(BF16)
    d_ff = wg_ref.shape[1]
    y = None
    for c0 in range(0, d_ff, ff_chunk):
        c1 = min(c0 + ff_chunk, d_ff)
        g = _dot(h, wg_ref[:, c0:c1])
        u = _dot(h, wu_ref[:, c0:c1])
        a = (g * _sigmoid(g) * u).astype(BF16)
        part = _dot(a, wd_ref[c0:c1, :])
        y = part if y is None else y + part
    o_ref[...] = x + FFN_RES * y


def _cast_block_rows(n_rows, n_steps):
    rb = BF16_SUBLANES * pl.cdiv(pl.cdiv(n_rows, n_steps), BF16_SUBLANES)
    while n_rows % rb:
        rb += BF16_SUBLANES
    return rb


def _ffn(x, mixes, norm_w, wg, wu, wd, casts=(), *, tm=FFN_TM, ff_chunk=FFN_CHUNK):
    t, d = x.shape
    d_ff = wg.shape[1]
    n_steps = t // tm
    const = lambda i: (0, 0)
    in_specs = [pl.BlockSpec((tm, d), lambda i: (i, 0))]
    args = [x]
    for mix, w_out in mixes:
        in_specs += [pl.BlockSpec((tm, mix.shape[1]), lambda i: (i, 0)),
                     pl.BlockSpec(w_out.shape, const)]
        args += [mix, w_out]
    in_specs += [pl.BlockSpec((1, d), const), pl.BlockSpec((d, d_ff), const),
                 pl.BlockSpec((d, d_ff), const), pl.BlockSpec((d_ff, d), const)]
    args += [norm_w, wg, wu, wd]
    out_specs = [pl.BlockSpec((tm, d), lambda i: (i, 0))]
    out_shape = [jax.ShapeDtypeStruct((t, d), F32)]
    for w in casts:
        rb = _cast_block_rows(w.shape[0], n_steps)
        spec = pl.BlockSpec((rb, w.shape[1]), lambda i, last=w.shape[0] // rb - 1: (jnp.minimum(i, last), 0))
        in_specs.append(spec)
        args.append(w)
        out_specs.append(spec)
        out_shape.append(jax.ShapeDtypeStruct(w.shape, BF16))
    outs = pl.pallas_call(
        functools.partial(_ffn_kernel, n_mix=len(mixes), n_cast=len(casts), ff_chunk=ff_chunk),
        grid=(n_steps,),
        in_specs=in_specs,
        out_specs=out_specs,
        out_shape=out_shape,
        compiler_params=_cparams(("arbitrary",)),
        name="ffn",
    )(*args)
    return outs[0], outs[1:]


EVEN_U = 0
EVEN_QK = EVEN_U + POOL_WIDTH
EVEN_V = EVEN_QK + 2 * MLSTM_WIDTH
EVEN_O = EVEN_V + MLSTM_WIDTH
EVEN_GI = EVEN_O + MLSTM_WIDTH
EVEN_GF = EVEN_GI + LANES
EVEN_COLS = EVEN_GF + LANES
POOL_HALO = 16
CONV_HALO = 8


def _even_kernel(x_ref, nw_ref, win_ref, poolw_ref, pscale_ref, convw_ref, convb_ref,
                 gbias_ref, mnorm_ref, o_ref,
                 halo_u, halo_qk, state, m_state, *, ts):
    s_idx = pl.program_id(1)

    @pl.when(s_idx == 0)
    def _():
        halo_u[...] = jnp.zeros_like(halo_u)
        halo_qk[...] = jnp.zeros_like(halo_qk)
        state[...] = jnp.zeros_like(state)
        m_state[...] = jnp.zeros_like(m_state)

    h = _rmsnorm(x_ref[...], nw_ref[...]).astype(BF16)
    full_proj = _dot(h, win_ref[...])
    proj = lambda c0, n: full_proj[:, c0:c0 + n]

    u = proj(EVEN_U, POOL_WIDTH)
    ue = jnp.concatenate([halo_u[...], u], axis=0)
    halo_u[...] = u[ts - POOL_HALO:, :]
    pos = (s_idx * ts + 1 + _row_iota((ts, 1))).astype(F32)
    for g, win in enumerate(POOL_WINDOWS):
        acc = ue[:, g * LANES:(g + 1) * LANES]
        sh = 1
        while sh < win:
            acc = acc + pltpu.roll(acc, sh, 0)
            sh *= 2
        win_sum = acc[POOL_HALO:, :]
        pooled = win_sum / jnp.minimum(pos, float(win)) - u[:, g * LANES:(g + 1) * LANES]
        mixed = _dot(pooled.astype(BF16), poolw_ref[g])
        o_ref[:, g * LANES:(g + 1) * LANES] = (
            mixed * pscale_ref[:, g * LANES:(g + 1) * LANES]).astype(o_ref.dtype)

    qk_raw = proj(EVEN_QK, 2 * MLSTM_WIDTH)
    conv = _causal_conv(halo_qk, qk_raw, convw_ref, convb_ref)
    qk = _silu_tanh(conv)
    q_all = qk[:, :MLSTM_WIDTH].astype(BF16)
    k_all = qk[:, MLSTM_WIDTH:] * (MLSTM_HEAD_DIM ** -0.5)
    v_all = proj(EVEN_V, MLSTM_WIDTH).astype(BF16)
    o_gate = _sigmoid_tanh(proj(EVEN_O, MLSTM_WIDTH))

    g_i = proj(EVEN_GI, LANES) + gbias_ref[0:1, :]
    g_f = -_softplus(-(proj(EVEN_GF, LANES) + gbias_ref[1:2, :]))

    ones_blk = jnp.ones((CHUNK, MLSTM_HEAD_DIM), BF16)
    causal = _row_iota((CHUNK, CHUNK)) >= _lane_iota((CHUNK, CHUNK))
    for c in range(ts // CHUNK):
        r0, r1 = c * CHUNK, (c + 1) * CHUNK
        li = g_i[r0:r1, :]
        b = _chunk_scan(g_f[r0:r1, :], jnp.add, 0.0)
        a = li - b
        m_prev = m_state[...]
        big_m = jnp.maximum(m_prev, _chunk_scan(a, jnp.maximum, -jnp.inf))
        b_last = b[CHUNK - 1:CHUNK, :]
        m_new = b_last + big_m[CHUNK - 1:CHUNK, :]
        w_inter = jnp.exp(m_prev - big_m)
        e_negm = jnp.exp(-(b + big_m))
        w_state = jnp.exp(b_last + a - m_new)
        decay = jnp.exp(b_last + m_prev - m_new)
        m_state[...] = m_new
        a_t = a.T
        def head_pipeline(hd):
            c0, c1 = hd * MLSTM_HEAD_DIM, (hd + 1) * MLSTM_HEAD_DIM
            qc = q_all[r0:r1, c0:c1]
            kc = k_all[r0:r1, c0:c1]
            v_ext = jnp.concatenate([v_all[r0:r1, c0:c1], ones_blk], axis=1)
            st = state[hd]
            s_qk = _dot_nt(qc, kc.astype(BF16))
            inter = _dot(qc, st.astype(BF16))
            yield
            w_intra = jnp.exp(jnp.where(causal, a_t[hd:hd + 1, :] - big_m[:, hd:hd + 1], -jnp.inf))
            p = (s_qk * w_intra).astype(BF16)
            kw_t = (kc * w_state[:, hd:hd + 1]).T.astype(BF16)
            yield
            pv = _dot(p, v_ext)
            state[hd] = decay[:, hd:hd + 1] * st + _dot(kw_t, v_ext)
            yield
            numden = w_inter[:, hd:hd + 1] * inter + pv
            den = jnp.maximum(jnp.abs(numden[:, MLSTM_HEAD_DIM:MLSTM_HEAD_DIM + 1]),
                              e_negm[:, hd:hd + 1])
            hh = numden[:, :MLSTM_HEAD_DIM] / den
            hn = _rmsnorm(hh, mnorm_ref[:, c0:c1])
            o_ref[r0:r1, POOL_WIDTH + c0:POOL_WIDTH + c1] = (
                o_gate[r0:r1, c0:c1] * hn).astype(o_ref.dtype)

        _staggered([head_pipeline(hd) for hd in range(MLSTM_HEADS)])


def _even_mixer(x, norm_w, w_in, pool_w, pool_scale, conv_w, conv_b, gate_bias, mnorm, *, ts=MIXER_TS):
    bsz, s, d = x.shape
    const2 = lambda b, i: (0, 0)
    return pl.pallas_call(
        functools.partial(_even_kernel, ts=ts),
        grid=(bsz, s // ts),
        in_specs=[
            pl.BlockSpec((None, ts, d), lambda b, i: (b, i, 0)),
            pl.BlockSpec((1, d), const2),
            pl.BlockSpec(w_in.shape, const2),
            pl.BlockSpec(pool_w.shape, lambda b, i: (0, 0, 0)),
            pl.BlockSpec(pool_scale.shape, const2),
            pl.BlockSpec(conv_w.shape, const2),
            pl.BlockSpec(conv_b.shape, const2),
            pl.BlockSpec(gate_bias.shape, const2),
            pl.BlockSpec(mnorm.shape, const2),
        ],
        out_specs=pl.BlockSpec((None, ts, POOL_WIDTH + MLSTM_WIDTH), lambda b, i: (b, i, 0)),
        out_shape=jax.ShapeDtypeStruct((bsz, s, POOL_WIDTH + MLSTM_WIDTH), BF16),
        scratch_shapes=[
            pltpu.VMEM((POOL_HALO, POOL_WIDTH), F32),
            pltpu.VMEM((CONV_HALO, 2 * MLSTM_WIDTH), F32),
            pltpu.VMEM((MLSTM_HEADS, MLSTM_HEAD_DIM, 2 * MLSTM_HEAD_DIM), F32),
            pltpu.VMEM((1, LANES), F32),
        ],
        compiler_params=_cparams(("parallel", "arbitrary")),
        name="even_mixer",
    )(x, norm_w, w_in, pool_w, pool_scale, conv_w, conv_b, gate_bias, mnorm)


ODD_Z = 0
ODD_XBC = ODD_Z + SSD_WIDTH
ODD_XS = ODD_XBC
ODD_B = ODD_XS + SSD_WIDTH
ODD_C = ODD_B + SSD_GROUPS * SSD_STATE
ODD_Q = ODD_C + SSD_GROUPS * SSD_STATE
ODD_K = ODD_Q + SB_WIDTH
ODD_V = ODD_K + SB_WIDTH
ODD_DT = ODD_V + SB_WIDTH
ODD_COLS = ODD_DT + LANES
SSD_XBC = SSD_WIDTH + 2 * SSD_GROUPS * SSD_STATE


def _odd_kernel(x_ref, nw_ref, win_ref, convw_ref, convb_ref, dtb_ref, alog_ref, dskip_ref,
                snorm_ref, qn_ref, kn_ref, seg_ref,
                c_ref, q_ref, k_ref, v_ref,
                halo, hstate, *, ts):
    @pl.when(pl.program_id(1) == 0)
    def _():
        halo[...] = jnp.zeros_like(halo)
        hstate[...] = jnp.zeros_like(hstate)

    h = _rmsnorm(x_ref[...], nw_ref[...]).astype(BF16)
    proj = lambda c0, n: _dot(h, win_ref[:, c0:c0 + n])

    seg = seg_ref[...]
    for src, nref, dst, scale in ((ODD_Q, qn_ref, q_ref, SB_HEAD_DIM ** -0.5), (ODD_K, kn_ref, k_ref, 1.0)):
        t = proj(src, SB_WIDTH)
        hi, lo = _split_bf16(t * t)
        ssq = _dot(hi, seg) + _dot(lo, seg)
        dst[...] = (t * lax.rsqrt(ssq * (1.0 / SB_HEAD_DIM) + EPS) * (nref[...] * scale)).astype(dst.dtype)
    v_ref[...] = proj(ODD_V, SB_WIDTH).astype(v_ref.dtype)

    raw = proj(ODD_XBC, SSD_XBC)
    conv = _causal_conv(halo, raw, convw_ref, convb_ref)
    xbc = _silu_tanh(conv)
    xs_all = xbc[:, :SSD_WIDTH]
    bm_all = xbc[:, SSD_WIDTH:SSD_WIDTH + SSD_GROUPS * SSD_STATE].astype(BF16)
    cm_all = xbc[:, SSD_WIDTH + SSD_GROUPS * SSD_STATE:].astype(BF16)

    dt_all = _softplus(proj(ODD_DT, LANES) + dtb_ref[...])
    a_all = dt_all * (-jnp.exp(alog_ref[...]))
    zg = proj(ODD_Z, SSD_WIDTH)
    zg = _silu_tanh(zg)

    causal = _row_iota((CHUNK, CHUNK)) >= _lane_iota((CHUNK, CHUNK))
    lane = _lane_iota((CHUNK, LANES))
    heads_per_group = SSD_HEADS // SSD_GROUPS
    for c in range(ts // CHUNK):
        r0, r1 = c * CHUNK, (c + 1) * CHUNK
        xs = xs_all[r0:r1, :]
        a_cum = _chunk_scan(a_all[r0:r1, :], jnp.add, 0.0)
        a_last = a_cum[CHUNK - 1:CHUNK, :]
        a_t = a_cum.T
        xdt = xs * _expand_heads(dt_all[r0:r1, :], SSD_HEADS, SSD_HEAD_DIM)
        a_wide = _expand_heads(a_cum, SSD_HEADS, SSD_HEAD_DIM)
        a_last_wide = a_wide[CHUNK - 1:CHUNK, :]
        xw = (xdt * jnp.exp(a_last_wide - a_wide)).astype(BF16)
        xdt = xdt.astype(BF16)
        from_start = jnp.exp(a_wide)
        chunk_decay = jnp.exp(a_last_wide)

        for g in range(SSD_GROUPS):
            bg = bm_all[r0:r1, g * SSD_STATE:(g + 1) * SSD_STATE]
            cg = cm_all[r0:r1, g * SSD_STATE:(g + 1) * SSD_STATE]
            gs = slice(g * SSD_GROUP_WIDTH, (g + 1) * SSD_GROUP_WIDTH)
            cb = _dot_nt(cg, bg)
            hprev = hstate[g]
            y_off = _dot(cg, hprev.astype(BF16))
            bg_t = bg.astype(F32).T.astype(BF16)
            hstate[g] = chunk_decay[:, gs] * hprev + _dot(bg_t, xw[:, gs])
            masked = []
            for hh in range(g * heads_per_group, (g + 1) * heads_per_group):
                dec = jnp.exp(jnp.where(causal, a_cum[:, hh:hh + 1] - a_t[hh:hh + 1, :], -jnp.inf))
                masked.append((cb * dec).astype(BF16))
            diag = []
            for pair in range(heads_per_group // 2):
                ps = slice(g * SSD_GROUP_WIDTH + pair * LANES, g * SSD_GROUP_WIDTH + (pair + 1) * LANES)
                outs = [_dot(masked[2 * pair + j], xdt[:, ps]) for j in range(2)]
                diag.append(jnp.where(lane < SSD_HEAD_DIM, outs[0], outs[1]))
            y = jnp.concatenate(diag, axis=1) + y_off * from_start[:, gs] + dskip_ref[:, gs] * xs[:, gs]
            c_ref[r0:r1, gs] = _rmsnorm(y * zg[r0:r1, gs], snorm_ref[:, gs]).astype(c_ref.dtype)


def _odd_mixer(x, norm_w, w_in, conv_w, conv_b, dt_bias, a_log, d_skip, snorm, qn, kn, seg, *, ts=MIXER_TS):
    bsz, s, d = x.shape
    const2 = lambda b, i: (0, 0)
    tile = lambda w: pl.BlockSpec((None, ts, w), lambda b, i: (b, i, 0))
    return pl.pallas_call(
        functools.partial(_odd_kernel, ts=ts),
        grid=(bsz, s // ts),
        in_specs=[tile(d)] + [pl.BlockSpec(a.shape, const2) for a in
                              (norm_w, w_in, conv_w, conv_b, dt_bias, a_log, d_skip, snorm, qn, kn, seg)],
        out_specs=[tile(SSD_WIDTH), tile(SB_WIDTH), tile(SB_WIDTH), tile(SB_WIDTH)],
        out_shape=[jax.ShapeDtypeStruct((bsz, s, SSD_WIDTH), BF16)]
        + [jax.ShapeDtypeStruct((bsz, s, SB_WIDTH), BF16)] * 3,
        scratch_shapes=[
            pltpu.VMEM((CONV_HALO, SSD_XBC), F32),
            pltpu.VMEM((SSD_GROUPS, SSD_STATE, SSD_GROUP_WIDTH), F32),
        ],
        compiler_params=_cparams(("parallel", "arbitrary")),
        name="odd_mixer",
    )(x, norm_w, w_in, conv_w, conv_b, dt_bias, a_log, d_skip, snorm, qn, kn, seg)


def _sb_kernel(q_ref, k_ref, v_ref, tri_ref, o_ref, acc_ref, r_ref, rmin_ref):
    step = pl.program_id(1)
    blk = SB_BLOCK
    qr = SB_QROWS
    per_blk = blk // qr
    pairs = SB_WIDTH // LANES
    first = _lane_iota((qr, LANES)) < SB_HEAD_DIM
    zero = jnp.zeros((qr, LANES), q_ref.dtype)
    q_stack = []
    for u in range(SB_UNITS):
        q = q_ref[u * qr:(u + 1) * qr, :]
        for p in range(pairs):
            qp = q[:, p * LANES:(p + 1) * LANES]
            q_stack.append(jnp.concatenate([jnp.where(first, qp, zero), jnp.where(first, zero, qp)], axis=0))
    tri = tri_ref[...]
    strict = [jnp.concatenate([_row_iota((qr, blk)) + o * qr > _lane_iota((qr, blk))] * 2, axis=0)
              for o in range(per_blk)]

    def pair_pipeline(u, p, j_lo, nblk, diag, out):
        k0 = pl.multiple_of(j_lo * blk, blk)
        cols = slice(p * LANES, (p + 1) * LANES)
        mask = strict[u % per_blk]
        r = None if diag else r_ref[u, 2 * p * qr:2 * (p + 1) * qr, :]
        z = _dot_nt(q_stack[u * pairs + p], k_ref[pl.ds(k0, nblk * blk), cols])
        yield
        sp = _softplus(z)
        sp_near_first = [sp[:, b * blk:(b + 1) * blk] for b in reversed(range(nblk))]
        if diag:
            sp_near_first[0] = jnp.where(mask, sp_near_first[0], 0.0)
        hi, lo = _split_bf16(jnp.concatenate(sp_near_first, axis=0))
        yield
        cs = _dot(jnp.concatenate([hi, lo], axis=1), tri)
        yield
        ws = []
        for i in range(nblk):
            b = nblk - 1 - i
            cs_b = cs[2 * i * qr:2 * (i + 1) * qr]
            arg = z[:, b * blk:(b + 1) * blk] - cs_b[:, :blk]
            w_b = jnp.exp(arg if r is None else arg - r)
            if diag and i == 0:
                w_b = jnp.where(mask, w_b, 0.0)
            ws.append(w_b.astype(BF16))
            r = cs_b[:, blk:] if r is None else r + cs_b[:, blk:]
        w = jnp.concatenate(ws[::-1], axis=1)
        yield
        pv = _dot(w, v_ref[pl.ds(k0, nblk * blk), cols])
        out(u, p, jnp.where(first, pv[:qr], pv[qr:]), r)

    def process(jobs, diag):
        rmins = {u: [] for u, _, _ in jobs}

        def out(u, p, acc, r):
            cols = slice(p * LANES, (p + 1) * LANES)
            if diag:
                acc_ref[u, :, cols] = acc
            else:
                acc_ref[u, :, cols] += acc
            r_ref[u, 2 * p * qr:2 * (p + 1) * qr, :] = r
            rmins[u].append(jnp.min(r))

        _staggered([pair_pipeline(u, p, j_lo, nblk, diag, out) for u, j_lo, nblk in jobs for p in range(pairs)])
        return {u: functools.reduce(jnp.minimum, v) for u, v in rmins.items()}

    def own_block(u):
        return step * (SB_UNITS // per_blk) + u // per_blk

    def near_blocks(u):
        return SB_NEAR - (u % per_blk)

    def near(nblk_of):
        jobs = [(u, own_block(u) - (nblk_of(u) - 1), nblk_of(u)) for u in range(SB_UNITS)]
        for u, m in process(jobs, True).items():
            rmin_ref[u] = m

    blocks_per_step = SB_UNITS // per_blk
    head_steps = -(-(SB_NEAR - 1) // blocks_per_step)

    @pl.when(step >= head_steps)
    def _():
        near(near_blocks)

    for s0 in range(head_steps):
        @pl.when(step == s0)
        def _():
            near(lambda u: min(near_blocks(u), s0 * blocks_per_step + u // per_blk + 1))

    for u in range(SB_UNITS):
        qi = own_block(u)

        def cond(c):
            return jnp.logical_and(c[0] >= 0, c[1] < SB_EXIT)

        def body(c, u=u):
            return c[0] - 1, process([(u, c[0], 1)], False)[u]

        lax.while_loop(cond, body, (qi - jnp.minimum(qi + 1, near_blocks(u)), rmin_ref[u]))
        o_ref[u * qr:(u + 1) * qr, :] = acc_ref[u].astype(o_ref.dtype)


def _stickbreak(q, k, v, tri):
    bsz, s, w = q.shape
    rows = SB_UNITS * SB_QROWS
    return pl.pallas_call(
        _sb_kernel,
        grid=(bsz, s // rows),
        in_specs=[
            pl.BlockSpec((None, rows, w), lambda b, i: (b, i, 0)),
            pl.BlockSpec((None, s, w), lambda b, i: (b, 0, 0)),
            pl.BlockSpec((None, s, w), lambda b, i: (b, 0, 0)),
            pl.BlockSpec(tri.shape, lambda b, i: (0, 0)),
        ],
        out_specs=pl.BlockSpec((None, rows, w), lambda b, i: (b, i, 0)),
        out_shape=jax.ShapeDtypeStruct((bsz, s, w), BF16),
        scratch_shapes=[pltpu.VMEM((SB_UNITS, SB_QROWS, w), F32),
                        pltpu.VMEM((SB_UNITS, 2 * (w // LANES) * SB_QROWS, SB_BLOCK), F32),
                        pltpu.SMEM((SB_UNITS,), F32)],
        compiler_params=_cparams(("parallel", "arbitrary")),
        name="stickbreak",
    )(q, k, v, tri)


def _pad_cols(w, n):
    return jnp.pad(w, ((0, 0), (0, n - w.shape[1])))


def kernel(x, l0_ffn1_norm, l0_ffn1_wg, l0_ffn1_wu, l0_ffn1_wd, l0_mix_norm, l0_w_in, l0_pool_w, l0_pool_scale, l0_qk_conv_w, l0_qk_conv_b, l0_gate_bias, l0_mlstm_norm, l0_w_out, l0_ffn2_norm, l0_ffn2_wg, l0_ffn2_wu, l0_ffn2_wd, l1_ffn1_norm, l1_ffn1_wg, l1_ffn1_wu, l1_ffn1_wd, l1_mix_norm, l1_w_in, l1_ssd_conv_w, l1_ssd_conv_b, l1_ssd_dt_bias, l1_ssd_A_log, l1_ssd_D, l1_ssd_norm, l1_sb_q_norm, l1_sb_k_norm, l1_w_out, l1_ffn2_norm, l1_ffn2_wg, l1_ffn2_wu, l1_ffn2_wd):
    bsz, s, d = x.shape
    t = bsz * s
    row = lambda a: a.reshape(1, -1).astype(F32)
    bf = lambda a: a.astype(BF16)

    def ffn(xf, mixes, norm_w, w3, next_w3=()):
        return _ffn(xf, mixes, row(norm_w), *[bf(w) for w in w3], casts=tuple(next_w3))

    xf, w_l0f2 = ffn(x.reshape(t, d), [], l0_ffn1_norm, (l0_ffn1_wg, l0_ffn1_wu, l0_ffn1_wd),
                     (l0_ffn2_wg, l0_ffn2_wu, l0_ffn2_wd))
    n_main = EVEN_GI
    w_in0 = jnp.concatenate([
        bf(l0_w_in[:, :n_main]),
        _pad_cols(bf(l0_w_in[:, n_main:n_main + MLSTM_HEADS]), LANES),
        _pad_cols(bf(l0_w_in[:, n_main + MLSTM_HEADS:]), LANES)], axis=1)
    gbias = jnp.stack([jnp.pad(l0_gate_bias[:MLSTM_HEADS], (0, LANES - MLSTM_HEADS)),
                       jnp.pad(l0_gate_bias[MLSTM_HEADS:], (0, LANES - MLSTM_HEADS))]).astype(F32)
    mix0 = _even_mixer(xf.reshape(bsz, s, d), row(l0_mix_norm), w_in0, bf(l0_pool_w),
                       row(l0_pool_scale), l0_qk_conv_w.astype(F32), row(l0_qk_conv_b), gbias,
                       row(l0_mlstm_norm))
    xf, w_l1f1 = ffn(xf, [(mix0.reshape(t, -1), bf(l0_w_out))], l0_ffn2_norm, w_l0f2,
                     (l1_ffn1_wg, l1_ffn1_wu, l1_ffn1_wd))

    xf, w_l1f2 = ffn(xf, [], l1_ffn1_norm, w_l1f1, (l1_ffn2_wg, l1_ffn2_wu, l1_ffn2_wd))
    c_dt = SSD_WIDTH + SSD_XBC
    c_q = c_dt + SSD_HEADS
    w_in1 = jnp.concatenate([bf(l1_w_in[:, :c_dt]), bf(l1_w_in[:, c_q:]),
                             _pad_cols(bf(l1_w_in[:, c_dt:c_q]), LANES)], axis=1)
    pad_heads = lambda a: jnp.pad(a.astype(F32), (0, LANES - SSD_HEADS)).reshape(1, LANES)
    seg_id = jnp.arange(SB_WIDTH) // SB_HEAD_DIM
    seg = (seg_id[:, None] == seg_id[None, :]).astype(BF16)
    c_out, qn, kn, vv = _odd_mixer(
        xf.reshape(bsz, s, d), row(l1_mix_norm), w_in1, l1_ssd_conv_w.astype(F32), row(l1_ssd_conv_b),
        pad_heads(l1_ssd_dt_bias), pad_heads(l1_ssd_A_log), row(jnp.repeat(l1_ssd_D, SSD_HEAD_DIM)),
        row(l1_ssd_norm), row(jnp.tile(l1_sb_q_norm, SB_HEADS)), row(jnp.tile(l1_sb_k_norm, SB_HEADS)), seg)
    idx = jnp.arange(SB_BLOCK)
    tri = jnp.concatenate([(idx[:, None] >= idx[None, :]).astype(BF16),
                           jnp.ones((SB_BLOCK, SB_BLOCK), BF16)], axis=1)
    tri = jnp.concatenate([tri, tri], axis=0)
    d_out = _stickbreak(qn, kn, vv, tri)
    xf, _ = ffn(xf, [(c_out.reshape(t, -1), bf(l1_w_out[:SSD_WIDTH])), (d_out.reshape(t, -1), bf(l1_w_out[SSD_WIDTH:]))],
                l1_ffn2_norm, w_l1f2)
    return xf.reshape(bsz, s, d)
```

```python
import functools

import jax
import jax.numpy as jnp
from jax import lax
from jax.experimental import pallas as pl
from jax.experimental.pallas import tpu as pltpu

F32 = jnp.float32
BF16 = jnp.bfloat16

EPS = 1e-6
FFN_RES = 0.5
SHORT_CONV = 4
LANES = 128
BF16_SUBLANES = 16

POOL_WINDOWS = (2, 4, 8, 16)
POOL_WIDTH = 512
MLSTM_HEADS = 4
MLSTM_HEAD_DIM = 128
MLSTM_WIDTH = MLSTM_HEADS * MLSTM_HEAD_DIM
CHUNK = 128

SSD_HEADS = 16
SSD_HEAD_DIM = 64
SSD_WIDTH = SSD_HEADS * SSD_HEAD_DIM
SSD_GROUPS = 4
SSD_STATE = 128
SSD_GROUP_WIDTH = SSD_WIDTH // SSD_GROUPS
SB_HEADS = 8
SB_HEAD_DIM = 64
SB_WIDTH = SB_HEADS * SB_HEAD_DIM
SB_BLOCK = 128
SB_EXIT = 104.0
SB_QROWS = 64
SB_UNITS = 8
SB_NEAR = 3

VMEM_LIMIT_BYTES = 56 * 1024 * 1024

FFN_TM = 512
FFN_CHUNK = 512
MIXER_TS = 512
EVEN_SEQS = 2
EVEN_TS = 256


def _cparams(semantics):
    return pltpu.CompilerParams(dimension_semantics=semantics, vmem_limit_bytes=VMEM_LIMIT_BYTES)


def _rmsnorm(x, w):
    return x * lax.rsqrt(jnp.mean(x * x, axis=-1, keepdims=True) + EPS) * w


def _sigmoid(x):
    return 1.0 / (1.0 + jnp.exp(-x))


def _sigmoid_tanh(x):
    return 0.5 + 0.5 * jnp.tanh(0.5 * x)


def _silu_tanh(x):
    h = 0.5 * x
    return h + h * jnp.tanh(h)


def _softplus(x):
    return jnp.maximum(x, 0.0) + jnp.log(1.0 + jnp.exp(-jnp.abs(x)))


def _dot(a, b):
    return jnp.dot(a, b, preferred_element_type=F32)


def _dot_nt(a, b):
    return lax.dot_general(a, b, (((1,), (1,)), ((), ())), preferred_element_type=F32)


def _split_bf16(x):
    hi = x.astype(BF16)
    lo = (x - hi.astype(F32)).astype(BF16)
    return hi, lo


def _row_iota(shape):
    return lax.broadcasted_iota(jnp.int32, shape, 0)


def _lane_iota(shape):
    return lax.broadcasted_iota(jnp.int32, shape, 1)


def _chunk_scan(x, op, fill):
    rows = _row_iota(x.shape)
    sh = 1
    while sh < x.shape[0]:
        shifted = jnp.where(rows >= sh, pltpu.roll(x, sh, 0), fill)
        x = op(x, shifted)
        sh *= 2
    return x


def _causal_conv(halo_ref, raw, w_ref, b_ref):
    assert w_ref.shape[0] == SHORT_CONV == 4
    ts = raw.shape[0]
    xe = jnp.concatenate([halo_ref[...], raw], axis=0)
    halo_ref[...] = raw[ts - CONV_HALO:, :]
    d1 = pltpu.roll(xe, 1, 0)
    pair = xe * w_ref[1:2, :] + d1 * w_ref[0:1, :]
    out = xe * w_ref[3:4, :] + d1 * w_ref[2:3, :] + pltpu.roll(pair, 2, 0)
    return out[CONV_HALO:, :] + b_ref[...]


def _staggered(gens):
    pending, live = list(gens), []
    while pending or live:
        if pending:
            live.append(pending.pop(0))
        for g in list(live):
            try:
                next(g)
            except StopIteration:
                live.remove(g)


def _expand_heads(slab, n_heads, head_dim):
    per = LANES // head_dim
    m = slab.shape[0]
    lane = _lane_iota((m, LANES))
    pieces = []
    for p in range(n_heads // per):
        piece = jnp.broadcast_to(slab[:, p * per:p * per + 1], (m, LANES))
        for j in range(1, per):
            piece = jnp.where(lane >= j * head_dim, slab[:, p * per + j:p * per + j + 1], piece)
        pieces.append(piece)
    return jnp.concatenate(pieces, axis=1)


def _ffn_kernel(*refs, n_mix, n_cast, ff_chunk):
    x_ref = refs[0]
    mix_refs = refs[1:1 + 2 * n_mix]
    nw_ref, wg_ref, wu_ref, wd_ref = refs[1 + 2 * n_mix:5 + 2 * n_mix]
    cast_in = refs[5 + 2 * n_mix:5 + 2 * n_mix + n_cast]
    o_ref = refs[5 + 2 * n_mix + n_cast]
    cast_out = refs[6 + 2 * n_mix + n_cast:]
    for src, dst in zip(cast_in, cast_out):
        dst[...] = src[...].astype(dst.dtype)
    x = x_ref[...]
    for i in range(n_mix):
        x = x + _dot(mix_refs[2 * i][...], mix_refs[2 * i + 1][...])
    h = _rmsnorm(x, nw_ref[...]).astype(BF16)
    d_ff = wg_ref.shape[1]
    y = None
    for c0 in range(0, d_ff, ff_chunk):
        c1 = min(c0 + ff_chunk, d_ff)
        g = _dot(h, wg_ref[:, c0:c1])
        u = _dot(h, wu_ref[:, c0:c1])
        a = (g * _sigmoid(g) * u).astype(BF16)
        part = _dot(a, wd_ref[c0:c1, :])
        y = part if y is None else y + part
    o_ref[...] = x + FFN_RES * y


def _cast_block_rows(n_rows, n_steps):
    rb = BF16_SUBLANES * pl.cdiv(pl.cdiv(n_rows, n_steps), BF16_SUBLANES)
    while n_rows % rb:
        rb += BF16_SUBLANES
    return rb


def _ffn(x, mixes, norm_w, wg, wu, wd, casts=(), *, tm=FFN_TM, ff_chunk=FFN_CHUNK):
    t, d = x.shape
    d_ff = wg.shape[1]
    n_steps = t // tm
    const = lambda i: (0, 0)
    in_specs = [pl.BlockSpec((tm, d), lambda i: (i, 0))]
    args = [x]
    for mix, w_out in mixes:
        in_specs += [pl.BlockSpec((tm, mix.shape[1]), lambda i: (i, 0)),
                     pl.BlockSpec(w_out.shape, const)]
        args += [mix, w_out]
    in_specs += [pl.BlockSpec((1, d), const), pl.BlockSpec((d, d_ff), const),
                 pl.BlockSpec((d, d_ff), const), pl.BlockSpec((d_ff, d), const)]
    args += [norm_w, wg, wu, wd]
    out_specs = [pl.BlockSpec((tm, d), lambda i: (i, 0))]
    out_shape = [jax.ShapeDtypeStruct((t, d), F32)]
    for w in casts:
        rb = _cast_block_rows(w.shape[0], n_steps)
        spec = pl.BlockSpec((rb, w.shape[1]), lambda i, last=w.shape[0] // rb - 1: (jnp.minimum(i, last), 0))
        in_specs.append(spec)
        args.append(w)
        out_specs.append(spec)
        out_shape.append(jax.ShapeDtypeStruct(w.shape, BF16))
    outs = pl.pallas_call(
        functools.partial(_ffn_kernel, n_mix=len(mixes), n_cast=len(casts), ff_chunk=ff_chunk),
        grid=(n_steps,),
        in_specs=in_specs,
        out_specs=out_specs,
        out_shape=out_shape,
        compiler_params=_cparams(("arbitrary",)),
        name="ffn",
    )(*args)
    return outs[0], outs[1:]


EVEN_U = 0
EVEN_QK = EVEN_U + POOL_WIDTH
EVEN_V = EVEN_QK + 2 * MLSTM_WIDTH
EVEN_O = EVEN_V + MLSTM_WIDTH
EVEN_GI = EVEN_O + MLSTM_WIDTH
EVEN_GF = EVEN_GI + LANES
EVEN_COLS = EVEN_GF + LANES
POOL_HALO = 16
CONV_HALO = 8


def _even_kernel(x_ref, nw_ref, win_ref, poolw_ref, pscale_ref, convw_ref, convb_ref,
                 gbias_ref, mnorm_ref, o_ref,
                 halo_u, halo_qk, state, m_state, *, ts):
    s_idx = pl.program_id(1)

    @pl.when(s_idx == 0)
    def _():
        halo_u[...] = jnp.zeros_like(halo_u)
        halo_qk[...] = jnp.zeros_like(halo_qk)
        state[...] = jnp.zeros_like(state)
        m_state[...] = jnp.zeros_like(m_state)

    pos = (s_idx * ts + 1 + _row_iota((ts, 1))).astype(F32)

    def tile_inputs(bb):
        h = _rmsnorm(x_ref[bb], nw_ref[...]).astype(BF16)
        full_proj = _dot(h, win_ref[...])
        proj = lambda c0, n: full_proj[:, c0:c0 + n]

        u = proj(EVEN_U, POOL_WIDTH)
        ue = jnp.concatenate([halo_u[bb], u], axis=0)
        halo_u[bb] = u[ts - POOL_HALO:, :]
        for g, win in enumerate(POOL_WINDOWS):
            acc = ue[:, g * LANES:(g + 1) * LANES]
            sh = 1
            while sh < win:
                acc = acc + pltpu.roll(acc, sh, 0)
                sh *= 2
            win_sum = acc[POOL_HALO:, :]
            pooled = win_sum / jnp.minimum(pos, float(win)) - u[:, g * LANES:(g + 1) * LANES]
            mixed = _dot(pooled.astype(BF16), poolw_ref[g])
            o_ref[bb, :, g * LANES:(g + 1) * LANES] = (
                mixed * pscale_ref[:, g * LANES:(g + 1) * LANES]).astype(o_ref.dtype)

        qk = _silu_tanh(_causal_conv(halo_qk.at[bb], proj(EVEN_QK, 2 * MLSTM_WIDTH), convw_ref, convb_ref))
        return dict(
            q=qk[:, :MLSTM_WIDTH].astype(BF16), k=qk[:, MLSTM_WIDTH:] * (MLSTM_HEAD_DIM ** -0.5),
            v=proj(EVEN_V, MLSTM_WIDTH).astype(BF16), o=_sigmoid_tanh(proj(EVEN_O, MLSTM_WIDTH)),
            g_i=proj(EVEN_GI, LANES) + gbias_ref[0:1, :],
            g_f=-_softplus(-(proj(EVEN_GF, LANES) + gbias_ref[1:2, :])))

    tiles = [tile_inputs(bb) for bb in range(EVEN_SEQS)]
    ones_blk = jnp.ones((CHUNK, MLSTM_HEAD_DIM), BF16)
    causal = _row_iota((CHUNK, CHUNK)) >= _lane_iota((CHUNK, CHUNK))

    def chunk_gates(bb, r0, r1):
        t = tiles[bb]
        li = t["g_i"][r0:r1, :]
        b = _chunk_scan(t["g_f"][r0:r1, :], jnp.add, 0.0)
        a = li - b
        m_prev = m_state[bb]
        big_m = jnp.maximum(m_prev, _chunk_scan(a, jnp.maximum, -jnp.inf))
        b_last = b[CHUNK - 1:CHUNK, :]
        m_new = b_last + big_m[CHUNK - 1:CHUNK, :]
        m_state[bb] = m_new
        return dict(big_m=big_m, w_inter=jnp.exp(m_prev - big_m), e_negm=jnp.exp(-(b + big_m)),
                    w_state=jnp.exp(b_last + a - m_new), decay=jnp.exp(b_last + m_prev - m_new),
                    a_t=a.T)

    def head_pipeline(bb, hd, r0, r1, gt):
        t = tiles[bb]
        c0, c1 = hd * MLSTM_HEAD_DIM, (hd + 1) * MLSTM_HEAD_DIM
        qc = t["q"][r0:r1, c0:c1]
        kc = t["k"][r0:r1, c0:c1]
        v_ext = jnp.concatenate([t["v"][r0:r1, c0:c1], ones_blk], axis=1)
        st = state[bb, hd]
        s_qk = _dot_nt(qc, kc.astype(BF16))
        inter = _dot(qc, st.astype(BF16))
        yield
        w_intra = jnp.exp(jnp.where(causal, gt["a_t"][hd:hd + 1, :] - gt["big_m"][:, hd:hd + 1], -jnp.inf))
        p = (s_qk * w_intra).astype(BF16)
        kw_t = (kc * gt["w_state"][:, hd:hd + 1]).T.astype(BF16)
        yield
        pv = _dot(p, v_ext)
        state[bb, hd] = gt["decay"][:, hd:hd + 1] * st + _dot(kw_t, v_ext)
        yield
        numden = gt["w_inter"][:, hd:hd + 1] * inter + pv
        den = jnp.maximum(jnp.abs(numden[:, MLSTM_HEAD_DIM:MLSTM_HEAD_DIM + 1]), gt["e_negm"][:, hd:hd + 1])
        hh = numden[:, :MLSTM_HEAD_DIM] / den
        hn = _rmsnorm(hh, mnorm_ref[:, c0:c1])
        o_ref[bb, r0:r1, POOL_WIDTH + c0:POOL_WIDTH + c1] = (t["o"][r0:r1, c0:c1] * hn).astype(o_ref.dtype)

    for c in range(ts // CHUNK):
        r0, r1 = c * CHUNK, (c + 1) * CHUNK
        gates = [chunk_gates(bb, r0, r1) for bb in range(EVEN_SEQS)]
        _staggered([head_pipeline(bb, hd, r0, r1, gates[bb])
                    for bb in range(EVEN_SEQS) for hd in range(MLSTM_HEADS)])


def _even_mixer(x, norm_w, w_in, pool_w, pool_scale, conv_w, conv_b, gate_bias, mnorm, *, ts=EVEN_TS):
    bsz, s, d = x.shape
    nb = EVEN_SEQS
    const2 = lambda b, i: (0, 0)
    return pl.pallas_call(
        functools.partial(_even_kernel, ts=ts),
        grid=(bsz // nb, s // ts),
        in_specs=[
            pl.BlockSpec((nb, ts, d), lambda b, i: (b, i, 0)),
            pl.BlockSpec((1, d), const2),
            pl.BlockSpec(w_in.shape, const2),
            pl.BlockSpec(pool_w.shape, lambda b, i: (0, 0, 0)),
            pl.BlockSpec(pool_scale.shape, const2),
            pl.BlockSpec(conv_w.shape, const2),
            pl.BlockSpec(conv_b.shape, const2),
            pl.BlockSpec(gate_bias.shape, const2),
            pl.BlockSpec(mnorm.shape, const2),
        ],
        out_specs=pl.BlockSpec((nb, ts, POOL_WIDTH + MLSTM_WIDTH), lambda b, i: (b, i, 0)),
        out_shape=jax.ShapeDtypeStruct((bsz, s, POOL_WIDTH + MLSTM_WIDTH), BF16),
        scratch_shapes=[
            pltpu.VMEM((nb, POOL_HALO, POOL_WIDTH), F32),
            pltpu.VMEM((nb, CONV_HALO, 2 * MLSTM_WIDTH), F32),
            pltpu.VMEM((nb, MLSTM_HEADS, MLSTM_HEAD_DIM, 2 * MLSTM_HEAD_DIM), F32),
            pltpu.VMEM((nb, 1, LANES), F32),
        ],
        compiler_params=_cparams(("parallel", "arbitrary")),
        name="even_mixer",
    )(x, norm_w, w_in, pool_w, pool_scale, conv_w, conv_b, gate_bias, mnorm)


ODD_Z = 0
ODD_XBC = ODD_Z + SSD_WIDTH
ODD_XS = ODD_XBC
ODD_B = ODD_XS + SSD_WIDTH
ODD_C = ODD_B + SSD_GROUPS * SSD_STATE
ODD_Q = ODD_C + SSD_GROUPS * SSD_STATE
ODD_K = ODD_Q + SB_WIDTH
ODD_V = ODD_K + SB_WIDTH
ODD_DT = ODD_V + SB_WIDTH
ODD_COLS = ODD_DT + LANES
SSD_XBC = SSD_WIDTH + 2 * SSD_GROUPS * SSD_STATE


def _odd_kernel(x_ref, nw_ref, win_ref, convw_ref, convb_ref, dtb_ref, alog_ref, dskip_ref,
                snorm_ref, qn_ref, kn_ref, seg_ref,
                c_ref, q_ref, k_ref, v_ref,
                halo, hstate, *, ts):
    @pl.when(pl.program_id(1) == 0)
    def _():
        halo[...] = jnp.zeros_like(halo)
        hstate[...] = jnp.zeros_like(hstate)

    h = _rmsnorm(x_ref[...], nw_ref[...]).astype(BF16)
    proj = lambda c0, n: _dot(h, win_ref[:, c0:c0 + n])

    seg = seg_ref[...]
    for src, nref, dst, scale in ((ODD_Q, qn_ref, q_ref, SB_HEAD_DIM ** -0.5), (ODD_K, kn_ref, k_ref, 1.0)):
        t = proj(src, SB_WIDTH)
        hi, lo = _split_bf16(t * t)
        ssq = _dot(hi, seg) + _dot(lo, seg)
        dst[...] = (t * lax.rsqrt(ssq * (1.0 / SB_HEAD_DIM) + EPS) * (nref[...] * scale)).astype(dst.dtype)
    v_ref[...] = proj(ODD_V, SB_WIDTH).astype(v_ref.dtype)

    raw = proj(ODD_XBC, SSD_XBC)
    conv = _causal_conv(halo, raw, convw_ref, convb_ref)
    xbc = _silu_tanh(conv)
    xs_all = xbc[:, :SSD_WIDTH]
    bm_all = xbc[:, SSD_WIDTH:SSD_WIDTH + SSD_GROUPS * SSD_STATE].astype(BF16)
    cm_all = xbc[:, SSD_WIDTH + SSD_GROUPS * SSD_STATE:].astype(BF16)

    dt_all = _softplus(proj(ODD_DT, LANES) + dtb_ref[...])
    a_all = dt_all * (-jnp.exp(alog_ref[...]))
    zg = proj(ODD_Z, SSD_WIDTH)
    zg = _silu_tanh(zg)

    causal = _row_iota((CHUNK, CHUNK)) >= _lane_iota((CHUNK, CHUNK))
    lane = _lane_iota((CHUNK, LANES))
    heads_per_group = SSD_HEADS // SSD_GROUPS
    for c in range(ts // CHUNK):
        r0, r1 = c * CHUNK, (c + 1) * CHUNK
        xs = xs_all[r0:r1, :]
        a_cum = _chunk_scan(a_all[r0:r1, :], jnp.add, 0.0)
        a_last = a_cum[CHUNK - 1:CHUNK, :]
        a_t = a_cum.T
        xdt = xs * _expand_heads(dt_all[r0:r1, :], SSD_HEADS, SSD_HEAD_DIM)
        a_wide = _expand_heads(a_cum, SSD_HEADS, SSD_HEAD_DIM)
        a_last_wide = a_wide[CHUNK - 1:CHUNK, :]
        xw = (xdt * jnp.exp(a_last_wide - a_wide)).astype(BF16)
        xdt = xdt.astype(BF16)
        from_start = jnp.exp(a_wide)
        chunk_decay = jnp.exp(a_last_wide)

        for g in range(SSD_GROUPS):
            bg = bm_all[r0:r1, g * SSD_STATE:(g + 1) * SSD_STATE]
            cg = cm_all[r0:r1, g * SSD_STATE:(g + 1) * SSD_STATE]
            gs = slice(g * SSD_GROUP_WIDTH, (g + 1) * SSD_GROUP_WIDTH)
            cb = _dot_nt(cg, bg)
            hprev = hstate[g]
            y_off = _dot(cg, hprev.astype(BF16))
            bg_t = bg.astype(F32).T.astype(BF16)
            hstate[g] = chunk_decay[:, gs] * hprev + _dot(bg_t, xw[:, gs])
            masked = []
            for hh in range(g * heads_per_group, (g + 1) * heads_per_group):
                dec = jnp.exp(jnp.where(causal, a_cum[:, hh:hh + 1] - a_t[hh:hh + 1, :], -jnp.inf))
                masked.append((cb * dec).astype(BF16))
            diag = []
            for pair in range(heads_per_group // 2):
                ps = slice(g * SSD_GROUP_WIDTH + pair * LANES, g * SSD_GROUP_WIDTH + (pair + 1) * LANES)
                outs = [_dot(masked[2 * pair + j], xdt[:, ps]) for j in range(2)]
                diag.append(jnp.where(lane < SSD_HEAD_DIM, outs[0], outs[1]))
            y = jnp.concatenate(diag, axis=1) + y_off * from_start[:, gs] + dskip_ref[:, gs] * xs[:, gs]
            c_ref[r0:r1, gs] = _rmsnorm(y * zg[r0:r1, gs], snorm_ref[:, gs]).astype(c_ref.dtype)


def _odd_mixer(x, norm_w, w_in, conv_w, conv_b, dt_bias, a_log, d_skip, snorm, qn, kn, seg, *, ts=MIXER_TS):
    bsz, s, d = x.shape
    const2 = lambda b, i: (0, 0)
    tile = lambda w: pl.BlockSpec((None, ts, w), lambda b, i: (b, i, 0))
    return pl.pallas_call(
        functools.partial(_odd_kernel, ts=ts),
        grid=(bsz, s // ts),
        in_specs=[tile(d)] + [pl.BlockSpec(a.shape, const2) for a in
                              (norm_w, w_in, conv_w, conv_b, dt_bias, a_log, d_skip, snorm, qn, kn, seg)],
        out_specs=[tile(SSD_WIDTH), tile(SB_WIDTH), tile(SB_WIDTH), tile(SB_WIDTH)],
        out_shape=[jax.ShapeDtypeStruct((bsz, s, SSD_WIDTH), BF16)]
        + [jax.ShapeDtypeStruct((bsz, s, SB_WIDTH), BF16)] * 3,
        scratch_shapes=[
            pltpu.VMEM((CONV_HALO, SSD_XBC), F32),
            pltpu.VMEM((SSD_GROUPS, SSD_STATE, SSD_GROUP_WIDTH), F32),
        ],
        compiler_params=_cparams(("parallel", "arbitrary")),
        name="odd_mixer",
    )(x, norm_w, w_in, conv_w, conv_b, dt_bias, a_log, d_skip, snorm, qn, kn, seg)


def _sb_kernel(q_ref, k_ref, v_ref, tri_ref, o_ref, acc_ref, r_ref, rmin_ref):
    step = pl.program_id(1)
    blk = SB_BLOCK
    qr = SB_QROWS
    per_blk = blk // qr
    pairs = SB_WIDTH // LANES
    first = _lane_iota((qr, LANES)) < SB_HEAD_DIM
    zero = jnp.zeros((qr, LANES), q_ref.dtype)
    q_stack = []
    for u in range(SB_UNITS):
        q = q_ref[u * qr:(u + 1) * qr, :]
        for p in range(pairs):
            qp = q[:, p * LANES:(p + 1) * LANES]
            q_stack.append(jnp.concatenate([jnp.where(first, qp, zero), jnp.where(first, zero, qp)], axis=0))
    tri = tri_ref[...]
    strict = [jnp.concatenate([_row_iota((qr, blk)) + o * qr > _lane_iota((qr, blk))] * 2, axis=0)
              for o in range(per_blk)]

    def pair_pipeline(u, p, j_lo, nblk, diag, out):
        k0 = pl.multiple_of(j_lo * blk, blk)
        cols = slice(p * LANES, (p + 1) * LANES)
        mask = strict[u % per_blk]
        r = None if diag else r_ref[u, 2 * p * qr:2 * (p + 1) * qr, :]
        z = _dot_nt(q_stack[u * pairs + p], k_ref[pl.ds(k0, nblk * blk), cols])
        yield
        sp = _softplus(z)
        sp_near_first = [sp[:, b * blk:(b + 1) * blk] for b in reversed(range(nblk))]
        if diag:
            sp_near_first[0] = jnp.where(mask, sp_near_first[0], 0.0)
        hi, lo = _split_bf16(jnp.concatenate(sp_near_first, axis=0))
        yield
        cs = _dot(jnp.concatenate([hi, lo], axis=1), tri)
        yield
        ws = []
        for i in range(nblk):
            b = nblk - 1 - i
            cs_b = cs[2 * i * qr:2 * (i + 1) * qr]
            arg = z[:, b * blk:(b + 1) * blk] - cs_b[:, :blk]
            w_b = jnp.exp(arg if r is None else arg - r)
            if diag and i == 0:
                w_b = jnp.where(mask, w_b, 0.0)
            ws.append(w_b.astype(BF16))
            r = cs_b[:, blk:] if r is None else r + cs_b[:, blk:]
        w = jnp.concatenate(ws[::-1], axis=1)
        yield
        pv = _dot(w, v_ref[pl.ds(k0, nblk * blk), cols])
        out(u, p, jnp.where(first, pv[:qr], pv[qr:]), r)

    def process(jobs, diag):
        rmins = {u: [] for u, _, _ in jobs}

        def out(u, p, acc, r):
            cols = slice(p * LANES, (p + 1) * LANES)
            if diag:
                acc_ref[u, :, cols] = acc
            else:
                acc_ref[u, :, cols] += acc
            r_ref[u, 2 * p * qr:2 * (p + 1) * qr, :] = r
            rmins[u].append(jnp.min(r))

        _staggered([pair_pipeline(u, p, j_lo, nblk, diag, out) for u, j_lo, nblk in jobs for p in range(pairs)])
        return {u: functools.reduce(jnp.minimum, v) for u, v in rmins.items()}

    def own_block(u):
        return step * (SB_UNITS // per_blk) + u // per_blk

    def near_blocks(u):
        return SB_NEAR - (u % per_blk)

    def near(nblk_of):
        jobs = [(u, own_block(u) - (nblk_of(u) - 1), nblk_of(u)) for u in range(SB_UNITS)]
        for u, m in process(jobs, True).items():
            rmin_ref[u] = m

    blocks_per_step = SB_UNITS // per_blk
    head_steps = -(-(SB_NEAR - 1) // blocks_per_step)

    @pl.when(step >= head_steps)
    def _():
        near(near_blocks)

    for s0 in range(head_steps):
        @pl.when(step == s0)
        def _():
            near(lambda u: min(near_blocks(u), s0 * blocks_per_step + u // per_blk + 1))

    for u in range(SB_UNITS):
        qi = own_block(u)

        def cond(c):
            return jnp.logical_and(c[0] >= 0, c[1] < SB_EXIT)

        def body(c, u=u):
            return c[0] - 1, process([(u, c[0], 1)], False)[u]

        lax.while_loop(cond, body, (qi - jnp.minimum(qi + 1, near_blocks(u)), rmin_ref[u]))
        o_ref[u * qr:(u + 1) * qr, :] = acc_ref[u].astype(o_ref.dtype)


def _stickbreak(q, k, v, tri):
    bsz, s, w = q.shape
    rows = SB_UNITS * SB_QROWS
    return pl.pallas_call(
        _sb_kernel,
        grid=(bsz, s // rows),
        in_specs=[
            pl.BlockSpec((None, rows, w), lambda b, i: (b, i, 0)),
            pl.BlockSpec((None, s, w), lambda b, i: (b, 0, 0)),
            pl.BlockSpec((None, s, w), lambda b, i: (b, 0, 0)),
            pl.BlockSpec(tri.shape, lambda b, i: (0, 0)),
        ],
        out_specs=pl.BlockSpec((None, rows, w), lambda b, i: (b, i, 0)),
        out_shape=jax.ShapeDtypeStruct((bsz, s, w), BF16),
        scratch_shapes=[pltpu.VMEM((SB_UNITS, SB_QROWS, w), F32),
                        pltpu.VMEM((SB_UNITS, 2 * (w // LANES) * SB_QROWS, SB_BLOCK), F32),
                        pltpu.SMEM((SB_UNITS,), F32)],
        compiler_params=_cparams(("parallel", "arbitrary")),
        name="stickbreak",
    )(q, k, v, tri)


def _pad_cols(w, n):
    return jnp.pad(w, ((0, 0), (0, n - w.shape[1])))


def kernel(x, l0_ffn1_norm, l0_ffn1_wg, l0_ffn1_wu, l0_ffn1_wd, l0_mix_norm, l0_w_in, l0_pool_w, l0_pool_scale, l0_qk_conv_w, l0_qk_conv_b, l0_gate_bias, l0_mlstm_norm, l0_w_out, l0_ffn2_norm, l0_ffn2_wg, l0_ffn2_wu, l0_ffn2_wd, l1_ffn1_norm, l1_ffn1_wg, l1_ffn1_wu, l1_ffn1_wd, l1_mix_norm, l1_w_in, l1_ssd_conv_w, l1_ssd_conv_b, l1_ssd_dt_bias, l1_ssd_A_log, l1_ssd_D, l1_ssd_norm, l1_sb_q_norm, l1_sb_k_norm, l1_w_out, l1_ffn2_norm, l1_ffn2_wg, l1_ffn2_wu, l1_ffn2_wd):
    bsz, s, d = x.shape
    t = bsz * s
    row = lambda a: a.reshape(1, -1).astype(F32)
    bf = lambda a: a.astype(BF16)

    def ffn(xf, mixes, norm_w, w3, next_w3=()):
        return _ffn(xf, mixes, row(norm_w), *[bf(w) for w in w3], casts=tuple(next_w3))

    xf, w_l0f2 = ffn(x.reshape(t, d), [], l0_ffn1_norm, (l0_ffn1_wg, l0_ffn1_wu, l0_ffn1_wd),
                     (l0_ffn2_wg, l0_ffn2_wu, l0_ffn2_wd))
    n_main = EVEN_GI
    w_in0 = jnp.concatenate([
        bf(l0_w_in[:, :n_main]),
        _pad_cols(bf(l0_w_in[:, n_main:n_main + MLSTM_HEADS]), LANES),
        _pad_cols(bf(l0_w_in[:, n_main + MLSTM_HEADS:]), LANES)], axis=1)
    gbias = jnp.stack([jnp.pad(l0_gate_bias[:MLSTM_HEADS], (0, LANES - MLSTM_HEADS)),
                       jnp.pad(l0_gate_bias[MLSTM_HEADS:], (0, LANES - MLSTM_HEADS))]).astype(F32)
    mix0 = _even_mixer(xf.reshape(bsz, s, d), row(l0_mix_norm), w_in0, bf(l0_pool_w),
                       row(l0_pool_scale), l0_qk_conv_w.astype(F32), row(l0_qk_conv_b), gbias,
                       row(l0_mlstm_norm))
    xf, w_l1f1 = ffn(xf, [(mix0.reshape(t, -1), bf(l0_w_out))], l0_ffn2_norm, w_l0f2,
                     (l1_ffn1_wg, l1_ffn1_wu, l1_ffn1_wd))

    xf, w_l1f2 = ffn(xf, [], l1_ffn1_norm, w_l1f1, (l1_ffn2_wg, l1_ffn2_wu, l1_ffn2_wd))
    c_dt = SSD_WIDTH + SSD_XBC
    c_q = c_dt + SSD_HEADS
    w_in1 = jnp.concatenate([bf(l1_w_in[:, :c_dt]), bf(l1_w_in[:, c_q:]),
                             _pad_cols(bf(l1_w_in[:, c_dt:c_q]), LANES)], axis=1)
    pad_heads = lambda a: jnp.pad(a.astype(F32), (0, LANES - SSD_HEADS)).reshape(1, LANES)
    seg_id = jnp.arange(SB_WIDTH) // SB_HEAD_DIM
    seg = (seg_id[:, None] == seg_id[None, :]).astype(BF16)
    c_out, qn, kn, vv = _odd_mixer(
        xf.reshape(bsz, s, d), row(l1_mix_norm), w_in1, l1_ssd_conv_w.astype(F32), row(l1_ssd_conv_b),
        pad_heads(l1_ssd_dt_bias), pad_heads(l1_ssd_A_log), row(jnp.repeat(l1_ssd_D, SSD_HEAD_DIM)),
        row(l1_ssd_norm), row(jnp.tile(l1_sb_q_norm, SB_HEADS)), row(jnp.tile(l1_sb_k_norm, SB_HEADS)), seg)
    idx = jnp.arange(SB_BLOCK)
    tri = jnp.concatenate([(idx[:, None] >= idx[None, :]).astype(BF16),
                           jnp.ones((SB_BLOCK, SB_BLOCK), BF16)], axis=1)
    tri = jnp.concatenate([tri, tri], axis=0)
    d_out = _stickbreak(qn, kn, vv, tri)
    xf, _ = ffn(xf, [(c_out.reshape(t, -1), bf(l1_w_out[:SSD_WIDTH])), (d_out.reshape(t, -1), bf(l1_w_out[SSD_WIDTH:]))],
                l1_ffn2_norm, w_l1f2)
    return xf.reshape(bsz, s, d)
```

```python
import functools

import jax
import jax.numpy as jnp
from jax import lax
from jax.experimental import pallas as pl
from jax.experimental.pallas import tpu as pltpu

F32 = jnp.float32
BF16 = jnp.bfloat16

EPS = 1e-6
FFN_RES = 0.5
SHORT_CONV = 4
LANES = 128
BF16_SUBLANES = 16

POOL_WINDOWS = (2, 4, 8, 16)
POOL_WIDTH = 512
MLSTM_HEADS = 4
MLSTM_HEAD_DIM = 128
MLSTM_WIDTH = MLSTM_HEADS * MLSTM_HEAD_DIM
CHUNK = 128

SSD_HEADS = 16
SSD_HEAD_DIM = 64
SSD_WIDTH = SSD_HEADS * SSD_HEAD_DIM
SSD_GROUPS = 4
SSD_STATE = 128
SSD_GROUP_WIDTH = SSD_WIDTH // SSD_GROUPS
SB_HEADS = 8
SB_HEAD_DIM = 64
SB_WIDTH = SB_HEADS * SB_HEAD_DIM
SB_BLOCK = 128
SB_EXIT = 104.0
SB_QROWS = 64
SB_UNITS = 8
SB_NEAR = 3

VMEM_LIMIT_BYTES = 56 * 1024 * 1024

FFN_TM = 512
FFN_CHUNK = 512
MIXER_TS = 512
EVEN_SEQS = 4
EVEN_TS = 256


def _cparams(semantics):
    return pltpu.CompilerParams(dimension_semantics=semantics, vmem_limit_bytes=VMEM_LIMIT_BYTES)


def _rmsnorm(x, w):
    return x * lax.rsqrt(jnp.mean(x * x, axis=-1, keepdims=True) + EPS) * w


def _sigmoid(x):
    return 1.0 / (1.0 + jnp.exp(-x))


def _sigmoid_tanh(x):
    return 0.5 + 0.5 * jnp.tanh(0.5 * x)


def _silu_tanh(x):
    h = 0.5 * x
    return h + h * jnp.tanh(h)


def _softplus(x):
    return jnp.maximum(x, 0.0) + jnp.log(1.0 + jnp.exp(-jnp.abs(x)))


def _dot(a, b):
    return jnp.dot(a, b, preferred_element_type=F32)


def _dot_nt(a, b):
    return lax.dot_general(a, b, (((1,), (1,)), ((), ())), preferred_element_type=F32)


def _split_bf16(x):
    hi = x.astype(BF16)
    lo = (x - hi.astype(F32)).astype(BF16)
    return hi, lo


def _row_iota(shape):
    return lax.broadcasted_iota(jnp.int32, shape, 0)


def _lane_iota(shape):
    return lax.broadcasted_iota(jnp.int32, shape, 1)


def _chunk_scan(x, op, fill):
    rows = _row_iota(x.shape)
    sh = 1
    while sh < x.shape[0]:
        shifted = jnp.where(rows >= sh, pltpu.roll(x, sh, 0), fill)
        x = op(x, shifted)
        sh *= 2
    return x


def _causal_conv(halo_ref, raw, w_ref, b_ref):
    assert w_ref.shape[0] == SHORT_CONV == 4
    ts = raw.shape[0]
    xe = jnp.concatenate([halo_ref[...], raw], axis=0)
    halo_ref[...] = raw[ts - CONV_HALO:, :]
    d1 = pltpu.roll(xe, 1, 0)
    pair = xe * w_ref[1:2, :] + d1 * w_ref[0:1, :]
    out = xe * w_ref[3:4, :] + d1 * w_ref[2:3, :] + pltpu.roll(pair, 2, 0)
    return out[CONV_HALO:, :] + b_ref[...]


def _staggered(gens):
    pending, live = list(gens), []
    while pending or live:
        if pending:
            live.append(pending.pop(0))
        for g in list(live):
            try:
                next(g)
            except StopIteration:
                live.remove(g)


def _expand_heads(slab, n_heads, head_dim):
    per = LANES // head_dim
    m = slab.shape[0]
    lane = _lane_iota((m, LANES))
    pieces = []
    for p in range(n_heads // per):
        piece = jnp.broadcast_to(slab[:, p * per:p * per + 1], (m, LANES))
        for j in range(1, per):
            piece = jnp.where(lane >= j * head_dim, slab[:, p * per + j:p * per + j + 1], piece)
        pieces.append(piece)
    return jnp.concatenate(pieces, axis=1)


def _ffn_kernel(*refs, n_mix, n_cast, ff_chunk):
    x_ref = refs[0]
    mix_refs = refs[1:1 + 2 * n_mix]
    nw_ref, wg_ref, wu_ref, wd_ref = refs[1 + 2 * n_mix:5 + 2 * n_mix]
    cast_in = refs[5 + 2 * n_mix:5 + 2 * n_mix + n_cast]
    o_ref = refs[5 + 2 * n_mix + n_cast]
    cast_out = refs[6 + 2 * n_mix + n_cast:]
    for src, dst in zip(cast_in, cast_out):
        dst[...] = src[...].astype(dst.dtype)
    x = x_ref[...]
    for i in range(n_mix):
        x = x + _dot(mix_refs[2 * i][...], mix_refs[2 * i + 1][...])
    h = _rmsnorm(x, nw_ref[...]).astype(BF16)
    d_ff = wg_ref.shape[1]
    y = None
    for c0 in range(0, d_ff, ff_chunk):
        c1 = min(c0 + ff_chunk, d_ff)
        g = _dot(h, wg_ref[:, c0:c1])
        u = _dot(h, wu_ref[:, c0:c1])
        a = (g * _sigmoid(g) * u).astype(BF16)
        part = _dot(a, wd_ref[c0:c1, :])
        y = part if y is None else y + part
    o_ref[...] = x + FFN_RES * y


def _cast_block_rows(n_rows, n_steps):
    rb = BF16_SUBLANES * pl.cdiv(pl.cdiv(n_rows, n_steps), BF16_SUBLANES)
    while n_rows % rb:
        rb += BF16_SUBLANES
    return rb


def _ffn(x, mixes, norm_w, wg, wu, wd, casts=(), *, tm=FFN_TM, ff_chunk=FFN_CHUNK):
    t, d = x.shape
    d_ff = wg.shape[1]
    n_steps = t // tm
    const = lambda i: (0, 0)
    in_specs = [pl.BlockSpec((tm, d), lambda i: (i, 0))]
    args = [x]
    for mix, w_out in mixes:
        in_specs += [pl.BlockSpec((tm, mix.shape[1]), lambda i: (i, 0)),
                     pl.BlockSpec(w_out.shape, const)]
        args += [mix, w_out]
    in_specs += [pl.BlockSpec((1, d), const), pl.BlockSpec((d, d_ff), const),
                 pl.BlockSpec((d, d_ff), const), pl.BlockSpec((d_ff, d), const)]
    args += [norm_w, wg, wu, wd]
    out_specs = [pl.BlockSpec((tm, d), lambda i: (i, 0))]
    out_shape = [jax.ShapeDtypeStruct((t, d), F32)]
    for w in casts:
        rb = _cast_block_rows(w.shape[0], n_steps)
        spec = pl.BlockSpec((rb, w.shape[1]), lambda i, last=w.shape[0] // rb - 1: (jnp.minimum(i, last), 0))
        in_specs.append(spec)
        args.append(w)
        out_specs.append(spec)
        out_shape.append(jax.ShapeDtypeStruct(w.shape, BF16))
    outs = pl.pallas_call(
        functools.partial(_ffn_kernel, n_mix=len(mixes), n_cast=len(casts), ff_chunk=ff_chunk),
        grid=(n_steps,),
        in_specs=in_specs,
        out_specs=out_specs,
        out_shape=out_shape,
        compiler_params=_cparams(("arbitrary",)),
        name="ffn",
    )(*args)
    return outs[0], outs[1:]


EVEN_U = 0
EVEN_QK = EVEN_U + POOL_WIDTH
EVEN_V = EVEN_QK + 2 * MLSTM_WIDTH
EVEN_O = EVEN_V + MLSTM_WIDTH
EVEN_GI = EVEN_O + MLSTM_WIDTH
EVEN_GF = EVEN_GI + LANES
EVEN_COLS = EVEN_GF + LANES
POOL_HALO = 16
CONV_HALO = 8


def _even_kernel(x_ref, nw_ref, win_ref, poolw_ref, pscale_ref, convw_ref, convb_ref,
                 gbias_ref, mnorm_ref, o_ref,
                 halo_u, halo_qk, state, m_state, *, ts):
    s_idx = pl.program_id(1)

    @pl.when(s_idx == 0)
    def _():
        halo_u[...] = jnp.zeros_like(halo_u)
        halo_qk[...] = jnp.zeros_like(halo_qk)
        state[...] = jnp.zeros_like(state)
        m_state[...] = jnp.zeros_like(m_state)

    pos = (s_idx * ts + 1 + _row_iota((ts, 1))).astype(F32)

    def tile_inputs(bb):
        h = _rmsnorm(x_ref[bb], nw_ref[...]).astype(BF16)
        full_proj = _dot(h, win_ref[...])
        proj = lambda c0, n: full_proj[:, c0:c0 + n]

        u = proj(EVEN_U, POOL_WIDTH)
        ue = jnp.concatenate([halo_u[bb], u], axis=0)
        halo_u[bb] = u[ts - POOL_HALO:, :]
        for g, win in enumerate(POOL_WINDOWS):
            acc = ue[:, g * LANES:(g + 1) * LANES]
            sh = 1
            while sh < win:
                acc = acc + pltpu.roll(acc, sh, 0)
                sh *= 2
            win_sum = acc[POOL_HALO:, :]
            pooled = win_sum / jnp.minimum(pos, float(win)) - u[:, g * LANES:(g + 1) * LANES]
            mixed = _dot(pooled.astype(BF16), poolw_ref[g])
            o_ref[bb, :, g * LANES:(g + 1) * LANES] = (
                mixed * pscale_ref[:, g * LANES:(g + 1) * LANES]).astype(o_ref.dtype)

        qk = _silu_tanh(_causal_conv(halo_qk.at[bb], proj(EVEN_QK, 2 * MLSTM_WIDTH), convw_ref, convb_ref))
        return dict(
            q=qk[:, :MLSTM_WIDTH].astype(BF16), k=qk[:, MLSTM_WIDTH:] * (MLSTM_HEAD_DIM ** -0.5),
            v=proj(EVEN_V, MLSTM_WIDTH).astype(BF16), o=_sigmoid_tanh(proj(EVEN_O, MLSTM_WIDTH)),
            g_i=proj(EVEN_GI, LANES) + gbias_ref[0:1, :],
            g_f=-_softplus(-(proj(EVEN_GF, LANES) + gbias_ref[1:2, :])))

    tiles = [tile_inputs(bb) for bb in range(EVEN_SEQS)]
    ones_blk = jnp.ones((CHUNK, MLSTM_HEAD_DIM), BF16)
    causal = _row_iota((CHUNK, CHUNK)) >= _lane_iota((CHUNK, CHUNK))

    def chunk_gates(bb, r0, r1):
        t = tiles[bb]
        li = t["g_i"][r0:r1, :]
        b = _chunk_scan(t["g_f"][r0:r1, :], jnp.add, 0.0)
        a = li - b
        m_prev = m_state[bb]
        big_m = jnp.maximum(m_prev, _chunk_scan(a, jnp.maximum, -jnp.inf))
        b_last = b[CHUNK - 1:CHUNK, :]
        m_new = b_last + big_m[CHUNK - 1:CHUNK, :]
        m_state[bb] = m_new
        return dict(big_m=big_m, w_inter=jnp.exp(m_prev - big_m), e_negm=jnp.exp(-(b + big_m)),
                    w_state=jnp.exp(b_last + a - m_new), decay=jnp.exp(b_last + m_prev - m_new),
                    a_t=a.T)

    def head_pipeline(bb, hd, r0, r1, gt):
        t = tiles[bb]
        c0, c1 = hd * MLSTM_HEAD_DIM, (hd + 1) * MLSTM_HEAD_DIM
        qc = t["q"][r0:r1, c0:c1]
        kc = t["k"][r0:r1, c0:c1]
        v_ext = jnp.concatenate([t["v"][r0:r1, c0:c1], ones_blk], axis=1)
        st = state[bb, hd]
        s_qk = _dot_nt(qc, kc.astype(BF16))
        inter = _dot(qc, st.astype(BF16))
        yield
        w_intra = jnp.exp(jnp.where(causal, gt["a_t"][hd:hd + 1, :] - gt["big_m"][:, hd:hd + 1], -jnp.inf))
        p = (s_qk * w_intra).astype(BF16)
        kw_t = (kc * gt["w_state"][:, hd:hd + 1]).T.astype(BF16)
        yield
        pv = _dot(p, v_ext)
        state[bb, hd] = gt["decay"][:, hd:hd + 1] * st + _dot(kw_t, v_ext)
        yield
        numden = gt["w_inter"][:, hd:hd + 1] * inter + pv
        den = jnp.maximum(jnp.abs(numden[:, MLSTM_HEAD_DIM:MLSTM_HEAD_DIM + 1]), gt["e_negm"][:, hd:hd + 1])
        hh = numden[:, :MLSTM_HEAD_DIM] / den
        hn = _rmsnorm(hh, mnorm_ref[:, c0:c1])
        o_ref[bb, r0:r1, POOL_WIDTH + c0:POOL_WIDTH + c1] = (t["o"][r0:r1, c0:c1] * hn).astype(o_ref.dtype)

    for c in range(ts // CHUNK):
        r0, r1 = c * CHUNK, (c + 1) * CHUNK
        gates = [chunk_gates(bb, r0, r1) for bb in range(EVEN_SEQS)]
        _staggered([head_pipeline(bb, hd, r0, r1, gates[bb])
                    for bb in range(EVEN_SEQS) for hd in range(MLSTM_HEADS)])


def _even_mixer(x, norm_w, w_in, pool_w, pool_scale, conv_w, conv_b, gate_bias, mnorm, *, ts=EVEN_TS):
    bsz, s, d = x.shape
    nb = EVEN_SEQS
    const2 = lambda b, i: (0, 0)
    return pl.pallas_call(
        functools.partial(_even_kernel, ts=ts),
        grid=(bsz // nb, s // ts),
        in_specs=[
            pl.BlockSpec((nb, ts, d), lambda b, i: (b, i, 0)),
            pl.BlockSpec((1, d), const2),
            pl.BlockSpec(w_in.shape, const2),
            pl.BlockSpec(pool_w.shape, lambda b, i: (0, 0, 0)),
            pl.BlockSpec(pool_scale.shape, const2),
            pl.BlockSpec(conv_w.shape, const2),
            pl.BlockSpec(conv_b.shape, const2),
            pl.BlockSpec(gate_bias.shape, const2),
            pl.BlockSpec(mnorm.shape, const2),
        ],
        out_specs=pl.BlockSpec((nb, ts, POOL_WIDTH + MLSTM_WIDTH), lambda b, i: (b, i, 0)),
        out_shape=jax.ShapeDtypeStruct((bsz, s, POOL_WIDTH + MLSTM_WIDTH), BF16),
        scratch_shapes=[
            pltpu.VMEM((nb, POOL_HALO, POOL_WIDTH), F32),
            pltpu.VMEM((nb, CONV_HALO, 2 * MLSTM_WIDTH), F32),
            pltpu.VMEM((nb, MLSTM_HEADS, MLSTM_HEAD_DIM, 2 * MLSTM_HEAD_DIM), F32),
            pltpu.VMEM((nb, 1, LANES), F32),
        ],
        compiler_params=_cparams(("parallel", "arbitrary")),
        name="even_mixer",
    )(x, norm_w, w_in, pool_w, pool_scale, conv_w, conv_b, gate_bias, mnorm)


ODD_Z = 0
ODD_XBC = ODD_Z + SSD_WIDTH
ODD_XS = ODD_XBC
ODD_B = ODD_XS + SSD_WIDTH
ODD_C = ODD_B + SSD_GROUPS * SSD_STATE
ODD_Q = ODD_C + SSD_GROUPS * SSD_STATE
ODD_K = ODD_Q + SB_WIDTH
ODD_V = ODD_K + SB_WIDTH
ODD_DT = ODD_V + SB_WIDTH
ODD_COLS = ODD_DT + LANES
SSD_XBC = SSD_WIDTH + 2 * SSD_GROUPS * SSD_STATE


def _odd_kernel(x_ref, nw_ref, win_ref, convw_ref, convb_ref, dtb_ref, alog_ref, dskip_ref,
                snorm_ref, qn_ref, kn_ref, seg_ref,
                c_ref, q_ref, k_ref, v_ref,
                halo, hstate, *, ts):
    @pl.when(pl.program_id(1) == 0)
    def _():
        halo[...] = jnp.zeros_like(halo)
        hstate[...] = jnp.zeros_like(hstate)

    h = _rmsnorm(x_ref[...], nw_ref[...]).astype(BF16)
    proj = lambda c0, n: _dot(h, win_ref[:, c0:c0 + n])

    seg = seg_ref[...]
    for src, nref, dst, scale in ((ODD_Q, qn_ref, q_ref, SB_HEAD_DIM ** -0.5), (ODD_K, kn_ref, k_ref, 1.0)):
        t = proj(src, SB_WIDTH)
        hi, lo = _split_bf16(t * t)
        ssq = _dot(hi, seg) + _dot(lo, seg)
        dst[...] = (t * lax.rsqrt(ssq * (1.0 / SB_HEAD_DIM) + EPS) * (nref[...] * scale)).astype(dst.dtype)
    v_ref[...] = proj(ODD_V, SB_WIDTH).astype(v_ref.dtype)

    raw = proj(ODD_XBC, SSD_XBC)
    conv = _causal_conv(halo, raw, convw_ref, convb_ref)
    xbc = _silu_tanh(conv)
    xs_all = xbc[:, :SSD_WIDTH]
    bm_all = xbc[:, SSD_WIDTH:SSD_WIDTH + SSD_GROUPS * SSD_STATE].astype(BF16)
    cm_all = xbc[:, SSD_WIDTH + SSD_GROUPS * SSD_STATE:].astype(BF16)

    dt_all = _softplus(proj(ODD_DT, LANES) + dtb_ref[...])
    a_all = dt_all * (-jnp.exp(alog_ref[...]))
    zg = proj(ODD_Z, SSD_WIDTH)
    zg = _silu_tanh(zg)

    causal = _row_iota((CHUNK, CHUNK)) >= _lane_iota((CHUNK, CHUNK))
    lane = _lane_iota((CHUNK, LANES))
    heads_per_group = SSD_HEADS // SSD_GROUPS
    for c in range(ts // CHUNK):
        r0, r1 = c * CHUNK, (c + 1) * CHUNK
        xs = xs_all[r0:r1, :]
        a_cum = _chunk_scan(a_all[r0:r1, :], jnp.add, 0.0)
        a_last = a_cum[CHUNK - 1:CHUNK, :]
        a_t = a_cum.T
        xdt = xs * _expand_heads(dt_all[r0:r1, :], SSD_HEADS, SSD_HEAD_DIM)
        a_wide = _expand_heads(a_cum, SSD_HEADS, SSD_HEAD_DIM)
        a_last_wide = a_wide[CHUNK - 1:CHUNK, :]
        xw = (xdt * jnp.exp(a_last_wide - a_wide)).astype(BF16)
        xdt = xdt.astype(BF16)
        from_start = jnp.exp(a_wide)
        chunk_decay = jnp.exp(a_last_wide)

        for g in range(SSD_GROUPS):
            bg = bm_all[r0:r1, g * SSD_STATE:(g + 1) * SSD_STATE]
            cg = cm_all[r0:r1, g * SSD_STATE:(g + 1) * SSD_STATE]
            gs = slice(g * SSD_GROUP_WIDTH, (g + 1) * SSD_GROUP_WIDTH)
            cb = _dot_nt(cg, bg)
            hprev = hstate[g]
            y_off = _dot(cg, hprev.astype(BF16))
            bg_t = bg.astype(F32).T.astype(BF16)
            hstate[g] = chunk_decay[:, gs] * hprev + _dot(bg_t, xw[:, gs])
            masked = []
            for hh in range(g * heads_per_group, (g + 1) * heads_per_group):
                dec = jnp.exp(jnp.where(causal, a_cum[:, hh:hh + 1] - a_t[hh:hh + 1, :], -jnp.inf))
                masked.append((cb * dec).astype(BF16))
            diag = []
            for pair in range(heads_per_group // 2):
                ps = slice(g * SSD_GROUP_WIDTH + pair * LANES, g * SSD_GROUP_WIDTH + (pair + 1) * LANES)
                outs = [_dot(masked[2 * pair + j], xdt[:, ps]) for j in range(2)]
                diag.append(jnp.where(lane < SSD_HEAD_DIM, outs[0], outs[1]))
            y = jnp.concatenate(diag, axis=1) + y_off * from_start[:, gs] + dskip_ref[:, gs] * xs[:, gs]
            c_ref[r0:r1, gs] = _rmsnorm(y * zg[r0:r1, gs], snorm_ref[:, gs]).astype(c_ref.dtype)


def _odd_mixer(x, norm_w, w_in, conv_w, conv_b, dt_bias, a_log, d_skip, snorm, qn, kn, seg, *, ts=MIXER_TS):
    bsz, s, d = x.shape
    const2 = lambda b, i: (0, 0)
    tile = lambda w: pl.BlockSpec((None, ts, w), lambda b, i: (b, i, 0))
    return pl.pallas_call(
        functools.partial(_odd_kernel, ts=ts),
        grid=(bsz, s // ts),
        in_specs=[tile(d)] + [pl.BlockSpec(a.shape, const2) for a in
                              (norm_w, w_in, conv_w, conv_b, dt_bias, a_log, d_skip, snorm, qn, kn, seg)],
        out_specs=[tile(SSD_WIDTH), tile(SB_WIDTH), tile(SB_WIDTH), tile(SB_WIDTH)],
        out_shape=[jax.ShapeDtypeStruct((bsz, s, SSD_WIDTH), BF16)]
        + [jax.ShapeDtypeStruct((bsz, s, SB_WIDTH), BF16)] * 3,
        scratch_shapes=[
            pltpu.VMEM((CONV_HALO, SSD_XBC), F32),
            pltpu.VMEM((SSD_GROUPS, SSD_STATE, SSD_GROUP_WIDTH), F32),
        ],
        compiler_params=_cparams(("parallel", "arbitrary")),
        name="odd_mixer",
    )(x, norm_w, w_in, conv_w, conv_b, dt_bias, a_log, d_skip, snorm, qn, kn, seg)


def _sb_kernel(q_ref, k_ref, v_ref, tri_ref, o_ref, acc_ref, r_ref, rmin_ref):
    step = pl.program_id(1)
    blk = SB_BLOCK
    qr = SB_QROWS
    per_blk = blk // qr
    pairs = SB_WIDTH // LANES
    first = _lane_iota((qr, LANES)) < SB_HEAD_DIM
    zero = jnp.zeros((qr, LANES), q_ref.dtype)
    q_stack = []
    for u in range(SB_UNITS):
        q = q_ref[u * qr:(u + 1) * qr, :]
        for p in range(pairs):
            qp = q[:, p * LANES:(p + 1) * LANES]
            q_stack.append(jnp.concatenate([jnp.where(first, qp, zero), jnp.where(first, zero, qp)], axis=0))
    tri = tri_ref[...]
    strict = [jnp.concatenate([_row_iota((qr, blk)) + o * qr > _lane_iota((qr, blk))] * 2, axis=0)
              for o in range(per_blk)]

    def pair_pipeline(u, p, j_lo, nblk, diag, out):
        k0 = pl.multiple_of(j_lo * blk, blk)
        cols = slice(p * LANES, (p + 1) * LANES)
        mask = strict[u % per_blk]
        r = None if diag else r_ref[u, 2 * p * qr:2 * (p + 1) * qr, :]
        z = _dot_nt(q_stack[u * pairs + p], k_ref[pl.ds(k0, nblk * blk), cols])
        yield
        sp = _softplus(z)
        sp_near_first = [sp[:, b * blk:(b + 1) * blk] for b in reversed(range(nblk))]
        if diag:
            sp_near_first[0] = jnp.where(mask, sp_near_first[0], 0.0)
        hi, lo = _split_bf16(jnp.concatenate(sp_near_first, axis=0))
        yield
        cs = _dot(jnp.concatenate([hi, lo], axis=1), tri)
        yield
        ws = []
        for i in range(nblk):
            b = nblk - 1 - i
            cs_b = cs[2 * i * qr:2 * (i + 1) * qr]
            arg = z[:, b * blk:(b + 1) * blk] - cs_b[:, :blk]
            w_b = jnp.exp(arg if r is None else arg - r)
            if diag and i == 0:
                w_b = jnp.where(mask, w_b, 0.0)
            ws.append(w_b.astype(BF16))
            r = cs_b[:, blk:] if r is None else r + cs_b[:, blk:]
        w = jnp.concatenate(ws[::-1], axis=1)
        yield
        pv = _dot(w, v_ref[pl.ds(k0, nblk * blk), cols])
        out(u, p, jnp.where(first, pv[:qr], pv[qr:]), r)

    def process(jobs, diag):
        rmins = {u: [] for u, _, _ in jobs}

        def out(u, p, acc, r):
            cols = slice(p * LANES, (p + 1) * LANES)
            if diag:
                acc_ref[u, :, cols] = acc
            else:
                acc_ref[u, :, cols] += acc
            r_ref[u, 2 * p * qr:2 * (p + 1) * qr, :] = r
            rmins[u].append(jnp.min(r))

        _staggered([pair_pipeline(u, p, j_lo, nblk, diag, out) for u, j_lo, nblk in jobs for p in range(pairs)])
        return {u: functools.reduce(jnp.minimum, v) for u, v in rmins.items()}

    def own_block(u):
        return step * (SB_UNITS // per_blk) + u // per_blk

    def near_blocks(u):
        return SB_NEAR - (u % per_blk)

    def near(nblk_of):
        jobs = [(u, own_block(u) - (nblk_of(u) - 1), nblk_of(u)) for u in range(SB_UNITS)]
        for u, m in process(jobs, True).items():
            rmin_ref[u] = m

    blocks_per_step = SB_UNITS // per_blk
    head_steps = -(-(SB_NEAR - 1) // blocks_per_step)

    @pl.when(step >= head_steps)
    def _():
        near(near_blocks)

    for s0 in range(head_steps):
        @pl.when(step == s0)
        def _():
            near(lambda u: min(near_blocks(u), s0 * blocks_per_step + u // per_blk + 1))

    for u in range(SB_UNITS):
        qi = own_block(u)

        def cond(c):
            return jnp.logical_and(c[0] >= 0, c[1] < SB_EXIT)

        def body(c, u=u):
            return c[0] - 1, process([(u, c[0], 1)], False)[u]

        lax.while_loop(cond, body, (qi - jnp.minimum(qi + 1, near_blocks(u)), rmin_ref[u]))
        o_ref[u * qr:(u + 1) * qr, :] = acc_ref[u].astype(o_ref.dtype)


def _stickbreak(q, k, v, tri):
    bsz, s, w = q.shape
    rows = SB_UNITS * SB_QROWS
    return pl.pallas_call(
        _sb_kernel,
        grid=(bsz, s // rows),
        in_specs=[
            pl.BlockSpec((None, rows, w), lambda b, i: (b, i, 0)),
            pl.BlockSpec((None, s, w), lambda b, i: (b, 0, 0)),
            pl.BlockSpec((None, s, w), lambda b, i: (b, 0, 0)),
            pl.BlockSpec(tri.shape, lambda b, i: (0, 0)),
        ],
        out_specs=pl.BlockSpec((None, rows, w), lambda b, i: (b, i, 0)),
        out_shape=jax.ShapeDtypeStruct((bsz, s, w), BF16),
        scratch_shapes=[pltpu.VMEM((SB_UNITS, SB_QROWS, w), F32),
                        pltpu.VMEM((SB_UNITS, 2 * (w // LANES) * SB_QROWS, SB_BLOCK), F32),
                        pltpu.SMEM((SB_UNITS,), F32)],
        compiler_params=_cparams(("parallel", "arbitrary")),
        name="stickbreak",
    )(q, k, v, tri)


def _pad_cols(w, n):
    return jnp.pad(w, ((0, 0), (0, n - w.shape[1])))


def kernel(x, l0_ffn1_norm, l0_ffn1_wg, l0_ffn1_wu, l0_ffn1_wd, l0_mix_norm, l0_w_in, l0_pool_w, l0_pool_scale, l0_qk_conv_w, l0_qk_conv_b, l0_gate_bias, l0_mlstm_norm, l0_w_out, l0_ffn2_norm, l0_ffn2_wg, l0_ffn2_wu, l0_ffn2_wd, l1_ffn1_norm, l1_ffn1_wg, l1_ffn1_wu, l1_ffn1_wd, l1_mix_norm, l1_w_in, l1_ssd_conv_w, l1_ssd_conv_b, l1_ssd_dt_bias, l1_ssd_A_log, l1_ssd_D, l1_ssd_norm, l1_sb_q_norm, l1_sb_k_norm, l1_w_out, l1_ffn2_norm, l1_ffn2_wg, l1_ffn2_wu, l1_ffn2_wd):
    bsz, s, d = x.shape
    t = bsz * s
    row = lambda a: a.reshape(1, -1).astype(F32)
    bf = lambda a: a.astype(BF16)

    def ffn(xf, mixes, norm_w, w3, next_w3=()):
        return _ffn(xf, mixes, row(norm_w), *[bf(w) for w in w3], casts=tuple(next_w3))

    xf, w_l0f2 = ffn(x.reshape(t, d), [], l0_ffn1_norm, (l0_ffn1_wg, l0_ffn1_wu, l0_ffn1_wd),
                     (l0_ffn2_wg, l0_ffn2_wu, l0_ffn2_wd))
    n_main = EVEN_GI
    w_in0 = jnp.concatenate([
        bf(l0_w_in[:, :n_main]),
        _pad_cols(bf(l0_w_in[:, n_main:n_main + MLSTM_HEADS]), LANES),
        _pad_cols(bf(l0_w_in[:, n_main + MLSTM_HEADS:]), LANES)], axis=1)
    gbias = jnp.stack([jnp.pad(l0_gate_bias[:MLSTM_HEADS], (0, LANES - MLSTM_HEADS)),
                       jnp.pad(l0_gate_bias[MLSTM_HEADS:], (0, LANES - MLSTM_HEADS))]).astype(F32)
    mix0 = _even_mixer(xf.reshape(bsz, s, d), row(l0_mix_norm), w_in0, bf(l0_pool_w),
                       row(l0_pool_scale), l0_qk_conv_w.astype(F32), row(l0_qk_conv_b), gbias,
                       row(l0_mlstm_norm))
    xf, w_l1f1 = ffn(xf, [(mix0.reshape(t, -1), bf(l0_w_out))], l0_ffn2_norm, w_l0f2,
                     (l1_ffn1_wg, l1_ffn1_wu, l1_ffn1_wd))

    xf, w_l1f2 = ffn(xf, [], l1_ffn1_norm, w_l1f1, (l1_ffn2_wg, l1_ffn2_wu, l1_ffn2_wd))
    c_dt = SSD_WIDTH + SSD_XBC
    c_q = c_dt + SSD_HEADS
    w_in1 = jnp.concatenate([bf(l1_w_in[:, :c_dt]), bf(l1_w_in[:, c_q:]),
                             _pad_cols(bf(l1_w_in[:, c_dt:c_q]), LANES)], axis=1)
    pad_heads = lambda a: jnp.pad(a.astype(F32), (0, LANES - SSD_HEADS)).reshape(1, LANES)
    seg_id = jnp.arange(SB_WIDTH) // SB_HEAD_DIM
    seg = (seg_id[:, None] == seg_id[None, :]).astype(BF16)
    c_out, qn, kn, vv = _odd_mixer(
        xf.reshape(bsz, s, d), row(l1_mix_norm), w_in1, l1_ssd_conv_w.astype(F32), row(l1_ssd_conv_b),
        pad_heads(l1_ssd_dt_bias), pad_heads(l1_ssd_A_log), row(jnp.repeat(l1_ssd_D, SSD_HEAD_DIM)),
        row(l1_ssd_norm), row(jnp.tile(l1_sb_q_norm, SB_HEADS)), row(jnp.tile(l1_sb_k_norm, SB_HEADS)), seg)
    idx = jnp.arange(SB_BLOCK)
    tri = jnp.concatenate([(idx[:, None] >= idx[None, :]).astype(BF16),
                           jnp.ones((SB_BLOCK, SB_BLOCK), BF16)], axis=1)
    tri = jnp.concatenate([tri, tri], axis=0)
    d_out = _stickbreak(qn, kn, vv, tri)
    xf, _ = ffn(xf, [(c_out.reshape(t, -1), bf(l1_w_out[:SSD_WIDTH])), (d_out.reshape(t, -1), bf(l1_w_out[SSD_WIDTH:]))],
                l1_ffn2_norm, w_l1f2)
    return xf.reshape(bsz, s, d)
```
